```python
import math
import jax
import jax.numpy as jnp
from jax import lax
import numpy as np

D_MODEL = 1024
BATCH = 8
SEQ = 2048
DEPTH = 1
DEC_BATCH = 128
DEC_SEQ = 1
PAST_LEN = 16384
PAGE_SIZE = 128

D_LRU = D_MODEL
LRU_BLOCKS = 16
LRU_BLOCK_W = D_LRU // LRU_BLOCKS
LRU_C = 8.0
SSD_EXPAND = 2
D_SSD = SSD_EXPAND * D_MODEL
SSD_HEAD_DIM = 64
SSD_HEADS = D_SSD // SSD_HEAD_DIM
SSD_GROUPS = 4
SSD_HPG = SSD_HEADS // SSD_GROUPS
SSD_STATE = 128
SSD_CHUNK = 128
SSD_CONV_DIM = D_SSD + 2 * SSD_GROUPS * SSD_STATE
CONV_W = 4
N_IN = 2 * D_LRU + D_SSD + SSD_CONV_DIM + SSD_HEADS + 2 * D_MODEL
LN_EPS = 1e-5
RMS_EPS = 1e-5
DEEPNORM_ALPHA = (2.0 * DEPTH) ** 0.25
DEEPNORM_BETA = (8.0 * DEPTH) ** -0.25

kernel_name = "hawk_ssd_parallel_gated_deepnorm_adaln_step"


def _layer_norm(x, g, b):
    xf = x.astype(jnp.float32)
    mu = jnp.mean(xf, -1, keepdims=True)
    var = jnp.mean(jnp.square(xf - mu), -1, keepdims=True)
    return ((xf - mu) * lax.rsqrt(var + LN_EPS) * g + b).astype(x.dtype)


def _causal_conv(u, buf, w, b):
    T = u.shape[1]
    up = jnp.concatenate([buf.astype(u.dtype), u], axis=1)
    out = b + w[0] * up[:, 0:T]
    for k in range(1, CONV_W):
        out = out + w[k] * up[:, k:k + T]
    return out, up[:, -(CONV_W - 1):]


def _linear_scan(a, b, h0):
    b = b.at[:, 0].add(a[:, 0] * h0)
    def comb(l, r):
        return (l[0] * r[0], r[0] * l[1] + r[1])
    _, h = lax.associative_scan(comb, (a, b), axis=1)
    return h


def _rg_lru(u, h0, wa, ba, wx, bx, lam, seq_start):
    Bn, T, _ = u.shape
    uf = u.astype(jnp.float32)
    ub = uf.reshape(Bn, T, LRU_BLOCKS, LRU_BLOCK_W)
    r = jax.nn.sigmoid(jnp.einsum('btnk,nkj->btnj', ub, wa).reshape(Bn, T, D_LRU) + ba)
    i = jax.nn.sigmoid(jnp.einsum('btnk,nkj->btnj', ub, wx).reshape(Bn, T, D_LRU) + bx)
    log_a = -LRU_C * r * jax.nn.softplus(-lam.astype(jnp.float32))
    a = jnp.exp(log_a)
    mult = jnp.sqrt(-jnp.expm1(2.0 * log_a))
    if seq_start:
        mult = mult.at[:, 0].set(1.0)
    h = _linear_scan(a, mult * i * uf, h0.astype(jnp.float32))
    return h, h[:, -1]


def _ssd(xh, dt, A, Bm, Cm, h0):
    Bn, T = xh.shape[0], xh.shape[1]
    L = min(SSD_CHUNK, T)
    nc = -(-T // L)
    pad = nc * L - T
    if pad:
        pw = lambda t: jnp.pad(t, [(0, 0), (0, pad)] + [(0, 0)] * (t.ndim - 2))
        xh, dt, Bm, Cm = pw(xh), pw(dt), pw(Bm), pw(Cm)
    xs = (xh.astype(jnp.float32) * dt[..., None]).reshape(Bn, nc, L, SSD_GROUPS, SSD_HPG, SSD_HEAD_DIM)
    a = (dt * A).reshape(Bn, nc, L, SSD_GROUPS, SSD_HPG)
    acs = jnp.transpose(jnp.cumsum(a, axis=2), (0, 1, 3, 4, 2))
    Bc = Bm.astype(jnp.float32).reshape(Bn, nc, L, SSD_GROUPS, SSD_STATE)
    Cc = Cm.astype(jnp.float32).reshape(Bn, nc, L, SSD_GROUPS, SSD_STATE)
    mask = jnp.tril(jnp.ones((L, L), dtype=bool))
    seg = acs[..., :, None] - acs[..., None, :]
    decay = jnp.exp(jnp.where(mask, seg, -jnp.inf))
    cb = jnp.einsum('bclgn,bcsgn->bcgls', Cc, Bc)
    y_diag = jnp.einsum('bcgls,bcghls,bcsghp->bclghp', cb, decay, xs)
    decay_end = jnp.exp(acs[..., -1:] - acs)
    states = jnp.einsum('bclgn,bcghl,bclghp->bcghpn', Bc, decay_end, xs)
    chunk_decay = jnp.exp(acs[..., -1])
    h0g = h0.astype(jnp.float32).reshape(Bn, SSD_GROUPS, SSD_HPG, SSD_HEAD_DIM, SSD_STATE)
    def step(h, inp):
        dec, s = inp
        return dec[..., None, None] * h + s, h
    h_last, h_in = lax.scan(step, h0g, (jnp.moveaxis(chunk_decay, 1, 0), jnp.moveaxis(states, 1, 0)))
    h_in = jnp.moveaxis(h_in, 0, 1)
    y_off = jnp.einsum('bclgn,bcghpn,bcghl->bclghp', Cc, h_in, jnp.exp(acs))
    y = (y_diag + y_off).reshape(Bn, nc * L, SSD_HEADS, SSD_HEAD_DIM)[:, :T]
    return y, h_last.reshape(Bn, SSD_HEADS, SSD_HEAD_DIM, SSD_STATE)


def _gated_rmsnorm(y, z, w):
    g = y.astype(jnp.float32) * jax.nn.silu(z.astype(jnp.float32))
    sh = g.shape
    gg = g.reshape(sh[:-1] + (SSD_GROUPS, D_SSD // SSD_GROUPS))
    gg = gg * lax.rsqrt(jnp.mean(gg * gg, -1, keepdims=True) + RMS_EPS)
    return gg.reshape(sh) * w


def _layer(x, c, lru_h0, lru_buf, ssd_h0, ssd_buf, seq_start,
           w_cond, b_cond, w_in, lru_conv_w, lru_conv_b, lru_wa, lru_ba, lru_wx, lru_bx, lru_lambda,
           ssd_conv_w, ssd_conv_b, ssd_dt_bias, ssd_a_log, ssd_d, ssd_norm_w,
           w_lru_proj, w_ssd_proj, w_out, ln_g, ln_b):
    Bn, T, _ = x.shape
    mod = c @ w_cond + b_cond
    shift, scale, gate = mod[:, :D_MODEL], mod[:, D_MODEL:2 * D_MODEL], mod[:, 2 * D_MODEL:]
    h = x * (1.0 + scale[:, None]) + shift[:, None]
    proj = h @ w_in
    o = 0
    lru_x = proj[..., o:o + D_LRU]; o += D_LRU
    lru_z = proj[..., o:o + D_LRU]; o += D_LRU
    ssd_z = proj[..., o:o + D_SSD]; o += D_SSD
    ssd_xbc = proj[..., o:o + SSD_CONV_DIM]; o += SSD_CONV_DIM
    ssd_dt = proj[..., o:o + SSD_HEADS]; o += SSD_HEADS
    merge_logits = proj[..., o:o + 2 * D_MODEL]
    u, lru_buf_new = _causal_conv(lru_x, lru_buf, lru_conv_w, lru_conv_b)
    hs, lru_h_new = _rg_lru(u, lru_h0, lru_wa, lru_ba, lru_wx, lru_bx, lru_lambda, seq_start)
    y_lru = hs * jax.nn.silu(lru_z.astype(jnp.float32))
    xbc, ssd_buf_new = _causal_conv(ssd_xbc, ssd_buf, ssd_conv_w, ssd_conv_b)
    xbc = jax.nn.silu(xbc)
    xs = xbc[..., :D_SSD].reshape(Bn, T, SSD_HEADS, SSD_HEAD_DIM)
    Bm = xbc[..., D_SSD:D_SSD + SSD_GROUPS * SSD_STATE].reshape(Bn, T, SSD_GROUPS, SSD_STATE)
    Cm = xbc[..., D_SSD + SSD_GROUPS * SSD_STATE:].reshape(Bn, T, SSD_GROUPS, SSD_STATE)
    dt = jax.nn.softplus((ssd_dt + ssd_dt_bias).astype(jnp.float32))
    A = -jnp.exp(ssd_a_log.astype(jnp.float32))
    y, ssd_h_new = _ssd(xs, dt, A, Bm, Cm, ssd_h0)
    y = y + ssd_d[:, None] * xs
    y_ssd = _gated_rmsnorm(y.reshape(Bn, T, D_SSD), ssd_z, ssd_norm_w)
    g = jax.nn.sigmoid(merge_logits.astype(jnp.float32))
    merged = g[..., :D_MODEL] * (y_lru @ w_lru_proj) + g[..., D_MODEL:] * (y_ssd @ w_ssd_proj)
    out = merged @ w_out
    x_new = _layer_norm(DEEPNORM_ALPHA * x + gate[:, None] * out, ln_g, ln_b).astype(x.dtype)
    return x_new, lru_h_new, lru_buf_new, ssd_h_new, ssd_buf_new


def setup_inputs(seed: int = 0) -> dict:
    key = jax.random.key(seed)
    ks = jax.random.split(key, 32)
    f32 = jnp.float32
    nrm = lambda k, shape, s: (jax.random.normal(k, shape, f32) * s)
    Dp = DEPTH
    a_c = jax.random.uniform(ks[20], (Dp, D_LRU), f32, 0.9, 0.999)
    s_l = a_c ** (1.0 / LRU_C)
    lru_lambda = jnp.log(s_l) - jnp.log1p(-s_l)
    dt0 = jnp.exp(jax.random.uniform(ks[21], (Dp, SSD_HEADS), f32, math.log(1e-3), math.log(1e-1)))
    ssd_dt_bias = dt0 + jnp.log(-jnp.expm1(-dt0))
    return {
        "x_prompt": nrm(ks[0], (BATCH, SEQ, D_MODEL), 1.0),
        "x_sample": nrm(ks[1], (DEC_BATCH, DEC_SEQ, D_MODEL), 1.0),
        "state_lru_h": nrm(ks[2], (Dp, DEC_BATCH, D_LRU), 0.5),
        "state_lru_conv": nrm(ks[3], (Dp, DEC_BATCH, CONV_W - 1, D_LRU), 1.0),
        "state_ssd_h": nrm(ks[4], (Dp, DEC_BATCH, SSD_HEADS, SSD_HEAD_DIM, SSD_STATE), 0.1),
        "state_ssd_conv": nrm(ks[5], (Dp, DEC_BATCH, CONV_W - 1, SSD_CONV_DIM), 1.0),
        "c_prompt": nrm(ks[6], (BATCH, D_MODEL), 1.0),
        "c_sample": nrm(ks[7], (DEC_BATCH, D_MODEL), 1.0),
        "w_cond": nrm(ks[8], (Dp, D_MODEL, 3 * D_MODEL), 0.2 * D_MODEL ** -0.5),
        "b_cond": nrm(ks[9], (Dp, 3 * D_MODEL), 0.02),
        "w_in": nrm(ks[10], (Dp, D_MODEL, N_IN), D_MODEL ** -0.5),
        "lru_conv_w": nrm(ks[11], (Dp, CONV_W, D_LRU), CONV_W ** -0.5),
        "lru_conv_b": nrm(ks[12], (Dp, D_LRU), 0.02),
        "lru_wa": nrm(ks[13], (Dp, LRU_BLOCKS, LRU_BLOCK_W, LRU_BLOCK_W), LRU_BLOCK_W ** -0.5),
        "lru_ba": nrm(ks[14], (Dp, D_LRU), 0.02),
        "lru_wx": nrm(ks[15], (Dp, LRU_BLOCKS, LRU_BLOCK_W, LRU_BLOCK_W), LRU_BLOCK_W ** -0.5),
        "lru_bx": nrm(ks[16], (Dp, D_LRU), 0.02),
        "lru_lambda": lru_lambda,
        "ssd_conv_w": nrm(ks[17], (Dp, CONV_W, SSD_CONV_DIM), CONV_W ** -0.5),
        "ssd_conv_b": nrm(ks[18], (Dp, SSD_CONV_DIM), 0.02),
        "ssd_dt_bias": ssd_dt_bias,
        "ssd_a_log": jnp.log(jax.random.uniform(ks[22], (Dp, SSD_HEADS), f32, 1.0, 16.0)),
        "ssd_d": 1.0 + nrm(ks[23], (Dp, SSD_HEADS), 0.1),
        "ssd_norm_w": 1.0 + nrm(ks[24], (Dp, D_SSD), 0.1),
        "w_lru_proj": nrm(ks[25], (Dp, D_LRU, D_MODEL), DEEPNORM_BETA * D_LRU ** -0.5),
        "w_ssd_proj": nrm(ks[26], (Dp, D_SSD, D_MODEL), DEEPNORM_BETA * D_SSD ** -0.5),
        "w_out": nrm(ks[27], (Dp, D_MODEL, D_MODEL), DEEPNORM_BETA * D_MODEL ** -0.5),
        "ln_g": 1.0 + nrm(ks[28], (Dp, D_MODEL), 0.1),
        "ln_b": nrm(ks[29], (Dp, D_MODEL), 0.02),
    }


def reference(x_prompt, x_sample, state_lru_h, state_lru_conv, state_ssd_h, state_ssd_conv,
              c_prompt, c_sample, w_cond, b_cond, w_in, lru_conv_w, lru_conv_b, lru_wa, lru_ba,
              lru_wx, lru_bx, lru_lambda, ssd_conv_w, ssd_conv_b, ssd_dt_bias, ssd_a_log, ssd_d,
              ssd_norm_w, w_lru_proj, w_ssd_proj, w_out, ln_g, ln_b):
    xp, xs = x_prompt, x_sample
    Bp = x_prompt.shape[0]
    dt_ = x_prompt.dtype
    p_lh, p_lc, p_sh, p_sc = [], [], [], []
    s_lh, s_lc, s_sh, s_sc = [], [], [], []
    for l in range(DEPTH):
        weights = (w_cond[l], b_cond[l], w_in[l], lru_conv_w[l], lru_conv_b[l], lru_wa[l], lru_ba[l],
                   lru_wx[l], lru_bx[l], lru_lambda[l], ssd_conv_w[l], ssd_conv_b[l], ssd_dt_bias[l],
                   ssd_a_log[l], ssd_d[l], ssd_norm_w[l], w_lru_proj[l], w_ssd_proj[l], w_out[l],
                   ln_g[l], ln_b[l])
        xp, a1, a2, a3, a4 = _layer(
            xp, c_prompt,
            jnp.zeros((Bp, D_LRU), jnp.float32),
            jnp.zeros((Bp, CONV_W - 1, D_LRU), dt_),
            jnp.zeros((Bp, SSD_HEADS, SSD_HEAD_DIM, SSD_STATE), jnp.float32),
            jnp.zeros((Bp, CONV_W - 1, SSD_CONV_DIM), dt_),
            True, *weights)
        p_lh.append(a1); p_lc.append(a2); p_sh.append(a3); p_sc.append(a4)
        xs, b1, b2, b3, b4 = _layer(
            xs, c_sample, state_lru_h[l], state_lru_conv[l], state_ssd_h[l], state_ssd_conv[l],
            False, *weights)
        s_lh.append(b1); s_lc.append(b2); s_sh.append(b3); s_sc.append(b4)
    return (xp, xs,
            jnp.stack(p_lh), jnp.stack(p_lc), jnp.stack(p_sh), jnp.stack(p_sc),
            jnp.stack(s_lh), jnp.stack(s_lc), jnp.stack(s_sh), jnp.stack(s_sc))
```

```python
import functools

import jax
import jax.numpy as jnp
from jax import lax
from jax.experimental import pallas as pl
from jax.experimental.pallas import tpu as pltpu

F32 = jnp.float32
BF16 = jnp.bfloat16

D_MODEL = 1024
D_LRU = 1024
LRU_BLOCK_W = 64
LRU_C = 8.0
D_SSD = 2048
HEAD_DIM = 64
HEADS = 32
GROUPS = 4
HEADS_PER_GROUP = HEADS // GROUPS
STATE = 128
GROUP_W = D_SSD // GROUPS
CHUNK = 128
CONV_DIM = D_SSD + 2 * GROUPS * STATE
CONV_W = 4
LN_EPS = 1e-5
RMS_EPS = 1e-5
ALPHA = 2.0 ** 0.25

LANES = 128
SUBLANES = 8
N_MAIN = 2 * D_LRU + D_SSD + CONV_DIM
OFF_LRU_X, OFF_LRU_Z, OFF_SSD_Z, OFF_XBC = 0, D_LRU, 2 * D_LRU, 2 * D_LRU + D_SSD

PROMPT_TC = 256
SAMPLE_BB = 8
VMEM_LIMIT = 60 * 1024 * 1024


def _sigmoid(x):
    return jax.nn.sigmoid(x)


def _silu(x):
    return x * _sigmoid(x)


def _softplus(x):
    return jnp.maximum(x, 0.0) + jnp.log1p(jnp.exp(-jnp.abs(x)))


def _dot(a, b):
    return jnp.dot(a, b, preferred_element_type=F32)


def _dot_nt(a, b):
    return lax.dot_general(a, b, (((1,), (1,)), ((), ())), preferred_element_type=F32)


def _dot_tn(a, b):
    return lax.dot_general(a, b, (((0,), (0,)), ((), ())), preferred_element_type=F32)


def _split3(x):
    hi = x.astype(BF16).astype(F32)
    r = x - hi
    mid = r.astype(BF16).astype(F32)
    lo = r - mid
    return hi, mid, lo


def _dot_exact_rhs(lhs_bf16, x):
    hi, mid, lo = _split3(x)
    return (_dot(lhs_bf16, hi.astype(BF16)) + _dot(lhs_bf16, mid.astype(BF16))
            + _dot(lhs_bf16, lo.astype(BF16)))


def _dot_exact_lhs(x, rhs_bf16):
    hi, mid, lo = _split3(x)
    return (_dot(hi.astype(BF16), rhs_bf16) + _dot(mid.astype(BF16), rhs_bf16)
            + _dot(lo.astype(BF16), rhs_bf16))


def _expand_heads(v, e3):
    lane = lax.broadcasted_iota(jnp.int32, v.shape, 1)
    v = jnp.where(lane < HEADS, v, 0.0)
    hi, mid, lo = _split3(v)
    packed = hi + pltpu.roll(mid, HEADS, axis=1) + pltpu.roll(lo, 2 * HEADS, axis=1)
    return _dot(packed.astype(BF16), e3)


def _lru_gates(u, w_gate_ref, ba, bx, lam):
    ub = u.astype(BF16)
    r_parts, i_parts = [], []
    for q in range(D_LRU // LANES):
        gq = _dot(ub[:, q * LANES:(q + 1) * LANES], w_gate_ref[q])
        r_parts.append(gq[:, :LANES])
        i_parts.append(gq[:, LANES:])
    r = _sigmoid(jnp.concatenate(r_parts, axis=1) + ba)
    i = _sigmoid(jnp.concatenate(i_parts, axis=1) + bx)
    log_a = (-LRU_C) * r * _softplus(-lam)
    a = jnp.exp(log_a)
    mult = jnp.sqrt(-jnp.tanh(log_a) * (a * a + 1.0))
    return a, mult, i


def _gated_rmsnorm(y, z, norm_w):
    g = y * _silu(z)
    parts = []
    for gi in range(GROUPS):
        gg = g[:, gi * GROUP_W:(gi + 1) * GROUP_W]
        ms = jnp.mean(gg * gg, axis=-1, keepdims=True)
        parts.append(gg * lax.rsqrt(ms + RMS_EPS))
    return jnp.concatenate(parts, axis=1) * norm_w


def _layer_norm(v, g, b):
    mu = jnp.mean(v, axis=-1, keepdims=True)
    d = v - mu
    var = jnp.mean(d * d, axis=-1, keepdims=True)
    return d * lax.rsqrt(var + LN_EPS) * g + b


def _cond_kernel(c_ref, w_ref, b_ref, o_ref):
    o_ref[...] = _dot(c_ref[...].astype(BF16), w_ref[...]) + b_ref[...]


def _prompt_kernel(x_ref, mod_ref, w_main, w_dt, w_dtT, w_mg, lcw, lcb, w_gate, ba, bx, lam,
                   scw, scb, dtb, dtbT, alog, alogT, dexp, normw, wlp, wsp, wo, lng, lnb, e3,
                   y_ref, lh_ref, lc_ref, sh_ref, sc_ref,
                   lbuf, sbuf, hl, hT, a_s, b_s, hs_s):
    tc = x_ref.shape[1]
    j = pl.program_id(1)
    last = pl.num_programs(1) - 1

    @pl.when(j == 0)
    def _():
        lbuf[0:SUBLANES, :] = jnp.zeros((SUBLANES, D_LRU), F32)
        sbuf[0:SUBLANES, :] = jnp.zeros((SUBLANES, CONV_DIM), F32)
        hl[...] = jnp.zeros_like(hl)
        hT[...] = jnp.zeros_like(hT)

    x = x_ref[0]
    mod = mod_ref[0]
    shift = mod[:, :D_MODEL]
    scale = mod[:, D_MODEL:2 * D_MODEL]
    gate = mod[:, 2 * D_MODEL:]
    hb = (x * (1.0 + scale) + shift).astype(BF16)

    lbuf[SUBLANES:SUBLANES + tc, :] = _dot(hb, w_main[:, OFF_LRU_X:OFF_LRU_X + D_LRU])
    sbuf[SUBLANES:SUBLANES + tc, :] = _dot(hb, w_main[:, OFF_XBC:OFF_XBC + CONV_DIM])
    base = SUBLANES - (CONV_W - 1)
    u = lcb[...] + lcw[0:1, :] * lbuf[base:base + tc, :]
    xbc = scb[...] + scw[0:1, :] * sbuf[base:base + tc, :]
    for k in range(1, CONV_W):
        u = u + lcw[k:k + 1, :] * lbuf[base + k:base + k + tc, :]
        xbc = xbc + scw[k:k + 1, :] * sbuf[base + k:base + k + tc, :]
    xbc = _silu(xbc)

    @pl.when(j == last)
    def _():
        lc_ref[0, 0] = lbuf[SUBLANES + tc - (CONV_W - 1):SUBLANES + tc, :]
        sc_ref[0, 0] = sbuf[SUBLANES + tc - (CONV_W - 1):SUBLANES + tc, :]

    lbuf[0:SUBLANES, :] = lbuf[tc:tc + SUBLANES, :]
    sbuf[0:SUBLANES, :] = sbuf[tc:tc + SUBLANES, :]

    a, mult, ig = _lru_gates(u, w_gate, ba[...], bx[...], lam[...])
    iu = ig * u
    row = lax.broadcasted_iota(jnp.int32, (tc, 1), 0)
    first = jnp.logical_and(row == 0, j == 0)
    a_s[...] = a
    b_s[...] = jnp.where(first, iu, mult * iu)

    def scan_body(t, h):
        h = a_s[pl.ds(t, 1), :] * h + b_s[pl.ds(t, 1), :]
        hs_s[pl.ds(t, 1), :] = h
        return h

    h_last = lax.fori_loop(0, tc, scan_body, hl[...], unroll=8)
    hl[...] = h_last

    @pl.when(j == last)
    def _():
        lh_ref[0] = h_last

    lru_z = _dot(hb, w_main[:, OFF_LRU_Z:OFF_LRU_Z + D_LRU])
    y_lru = hs_s[...] * _silu(lru_z)
    p_lru = _dot(y_lru.astype(BF16), wlp[...])

    xs = xbc[:, :D_SSD]
    bm = xbc[:, D_SSD:D_SSD + GROUPS * STATE]
    cm = xbc[:, D_SSD + GROUPS * STATE:]
    dt = _softplus(_dot(hb, w_dt[...]) + dtb[...])
    dtT = _softplus(_dot_nt(w_dtT[...], hb) + dtbT[...])
    a_row = -jnp.exp(alog[...])
    a_col = -jnp.exp(alogT[...])

    ri = lax.broadcasted_iota(jnp.int32, (CHUNK, CHUNK), 0)
    ci = lax.broadcasted_iota(jnp.int32, (CHUNK, CHUNK), 1)
    causal = ri >= ci
    tril = causal.astype(BF16)
    triu = (ri <= ci).astype(BF16)
    lane = lax.broadcasted_iota(jnp.int32, (CHUNK, LANES), 1)
    lo_half = lane < HEAD_DIM

    y_chunks = []
    for c in range(tc // CHUNK):
        r0 = c * CHUNK
        xs_c = xs[r0:r0 + CHUNK]
        xs_cb = xs_c.astype(BF16)
        bm_c = bm[r0:r0 + CHUNK].astype(BF16)
        cm_c = cm[r0:r0 + CHUNK].astype(BF16)
        dt_c = dt[r0:r0 + CHUNK]
        dtT_c = dtT[:, r0:r0 + CHUNK]
        acs = _dot_exact_rhs(tril, dt_c * a_row)
        acsT = _dot_exact_lhs(dtT_c * a_col, triu)
        e_exp = _expand_heads(jnp.exp(acs), e3[...])
        w_exp = _expand_heads(jnp.exp(acs[CHUNK - 1:CHUNK, :] - acs) * dt_c, e3[...])

        y_parts = []
        for g in range(GROUPS):
            cg = cm_c[:, g * STATE:(g + 1) * STATE]
            bg = bm_c[:, g * STATE:(g + 1) * STATE]
            cb = _dot_nt(cg, bg)
            for q in range(HEADS_PER_GROUP // 2):
                pair = g * (HEADS_PER_GROUP // 2) + q
                xp = xs_cb[:, pair * LANES:(pair + 1) * LANES]
                yp = None
                for half in range(2):
                    h = 2 * pair + half
                    seg = acs[:, h:h + 1] - acsT[h:h + 1, :]
                    m = jnp.exp(jnp.where(causal, seg, -jnp.inf)) * cb * dtT_c[h:h + 1, :]
                    keep = lo_half if half == 0 else jnp.logical_not(lo_half)
                    xh = jnp.where(keep, xp, jnp.zeros_like(xp))
                    t = _dot(m.astype(BF16), xh)
                    yp = t if yp is None else yp + t
                y_parts.append(yp)
        y_diag = jnp.concatenate(y_parts, axis=1)

        h_prev = hT[...]
        h_prev_b = h_prev.astype(BF16)
        xsw = (xs_c * w_exp).astype(BF16)
        off_parts, st_parts = [], []
        for g in range(GROUPS):
            sl = slice(g * GROUP_W, (g + 1) * GROUP_W)
            off_parts.append(_dot(cm_c[:, g * STATE:(g + 1) * STATE], h_prev_b[:, sl]))
            st_parts.append(_dot_tn(bm_c[:, g * STATE:(g + 1) * STATE], xsw[:, sl]))
        y_off = jnp.concatenate(off_parts, axis=1) * e_exp
        hT[...] = e_exp[CHUNK - 1:CHUNK, :] * h_prev + jnp.concatenate(st_parts, axis=1)
        y_chunks.append(y_diag + y_off + dexp[...] * xs_c)
    y = jnp.concatenate(y_chunks, axis=0) if len(y_chunks) > 1 else y_chunks[0]

    @pl.when(j == last)
    def _():
        hfin = hT[...]
        for q in range(D_SSD // LANES):
            sh_ref[0, q * LANES:(q + 1) * LANES, :] = hfin[:, q * LANES:(q + 1) * LANES].T

    ssd_z = _dot(hb, w_main[:, OFF_SSD_Z:OFF_SSD_Z + D_SSD])
    y_ssd = _gated_rmsnorm(y, ssd_z, normw[...])
    p_ssd = _dot(y_ssd.astype(BF16), wsp[...])

    gm = _sigmoid(_dot(hb, w_mg[...]))
    merged = gm[:, :D_MODEL] * p_lru + gm[:, D_MODEL:] * p_ssd
    out = _dot(merged.astype(BF16), wo[...])
    y_ref[0] = _layer_norm(ALPHA * x + gate * out, lng[...], lnb[...])


def _sample_in_kernel(x_ref, mod_ref, lh0_ref, lst_ref, sst_ref, w_main, w_dt, w_mg, lcw, lcb,
                      w_gate, ba, bx, lam, scw, scb, dtb, alog, wlp, e3,
                      lh_ref, lco_ref, sco_ref, plru_ref, xbc_ref, xdt_ref, da_ref, z_ref, gm_ref):
    x = x_ref[...]
    mod = mod_ref[...]
    shift = mod[:, :D_MODEL]
    scale = mod[:, D_MODEL:2 * D_MODEL]
    hb = (x * (1.0 + scale) + shift).astype(BF16)

    lru_x = _dot(hb, w_main[:, OFF_LRU_X:OFF_LRU_X + D_LRU])
    xbc_raw = _dot(hb, w_main[:, OFF_XBC:OFF_XBC + CONV_DIM])
    u = lcb[...] + lcw[CONV_W - 1:CONV_W, :] * lru_x
    xbc = scb[...] + scw[CONV_W - 1:CONV_W, :] * xbc_raw
    for k in range(CONV_W - 1):
        u = u + lcw[k:k + 1, :] * lst_ref[k]
        xbc = xbc + scw[k:k + 1, :] * sst_ref[k]
    for k in range(CONV_W - 2):
        lco_ref[k] = lst_ref[k + 1]
        sco_ref[k] = sst_ref[k + 1]
    lco_ref[CONV_W - 2] = lru_x
    sco_ref[CONV_W - 2] = xbc_raw
    xbc = _silu(xbc)
    xbc_ref[...] = xbc

    a, mult, ig = _lru_gates(u, w_gate, ba[...], bx[...], lam[...])
    h_new = a * lh0_ref[...] + mult * ig * u
    lh_ref[...] = h_new
    lru_z = _dot(hb, w_main[:, OFF_LRU_Z:OFF_LRU_Z + D_LRU])
    plru_ref[...] = _dot((h_new * _silu(lru_z)).astype(BF16), wlp[...])

    dt = _softplus(_dot(hb, w_dt[...]) + dtb[...])
    da_ref[...] = jnp.exp(dt * (-jnp.exp(alog[...])))
    xdt_ref[...] = xbc[:, :D_SSD] * _expand_heads(dt, e3[...])
    z_ref[...] = _dot(hb, w_main[:, OFF_SSD_Z:OFF_SSD_Z + D_SSD])
    gm_ref[...] = _sigmoid(_dot(hb, w_mg[...]))


def _sample_state_kernel(da_ref, h0_ref, xdt_ref, b_ref, c_ref, ho_ref, y_ref, xT_s, yT_s):
    i = pl.program_id(0)
    bb = h0_ref.shape[0]
    nb = xdt_ref.shape[0]

    @pl.when(i == 0)
    def _():
        yT_s[...] = jnp.zeros_like(yT_s)

    xT_s[...] = pltpu.roll(xdt_ref[...].T, (nb - i * bb) % nb, axis=1)

    for b in range(bb):
        def head_body(h, carry, b=b):
            g = h // HEADS_PER_GROUP
            r0 = pl.multiple_of(h * HEAD_DIM, HEAD_DIM)
            xcol = xT_s[pl.ds(r0, HEAD_DIM), b:b + 1]
            brow = b_ref[b, pl.ds(g, 1), :]
            crow = c_ref[b, pl.ds(g, 1), :]
            hn = da_ref[i * bb + b, h] * h0_ref[b, h] + xcol * brow
            ho_ref[b, h] = hn
            yT_s[pl.ds(r0, HEAD_DIM), b:b + 1] = jnp.sum(hn * crow, axis=1, keepdims=True)
            return carry
        lax.fori_loop(0, HEADS, head_body, 0)

    y_ref[...] = yT_s[...].T[0:bb, :]


def _sample_out_kernel(x_ref, mod_ref, yraw_ref, xbc_ref, z_ref, gm_ref, plru_ref,
                       dexp, normw, wsp, wo, lng, lnb, o_ref):
    x = x_ref[...]
    gate = mod_ref[:, 2 * D_MODEL:]
    y = yraw_ref[...] + dexp[...] * xbc_ref[:, :D_SSD]
    y_ssd = _gated_rmsnorm(y, z_ref[...], normw[...])
    p_ssd = _dot(y_ssd.astype(BF16), wsp[...])
    gm = gm_ref[...]
    merged = gm[:, :D_MODEL] * plru_ref[...] + gm[:, D_MODEL:] * p_ssd
    out = _dot(merged.astype(BF16), wo[...])
    o_ref[...] = _layer_norm(ALPHA * x + gate * out, lng[...], lnb[...])


def _resident(shape, grid_rank):
    zeros = (0,) * len(shape)
    if grid_rank == 1:
        imap = lambda i: zeros
    else:
        imap = lambda b, j: zeros
    return pl.BlockSpec(shape, imap, pipeline_mode=pl.Buffered(1))


def _whole(shape):
    return jax.ShapeDtypeStruct(shape, F32)


def kernel(x_prompt, x_sample, state_lru_h, state_lru_conv, state_ssd_h, state_ssd_conv, c_prompt, c_sample, w_cond, b_cond, w_in, lru_conv_w, lru_conv_b, lru_wa, lru_ba, lru_wx, lru_bx, lru_lambda, ssd_conv_w, ssd_conv_b, ssd_dt_bias, ssd_a_log, ssd_d, ssd_norm_w, w_lru_proj, w_ssd_proj, w_out, ln_g, ln_b):
    nbp, seq, _ = x_prompt.shape
    nbs = x_sample.shape[0]
    tc = PROMPT_TC
    assert seq % tc == 0 and tc % CHUNK == 0 and nbs % SAMPLE_BB == 0 and nbs == LANES

    w_in0 = w_in[0]
    w_main = w_in0[:, :N_MAIN].astype(BF16)
    w_dt_cols = w_in0[:, N_MAIN:N_MAIN + HEADS]
    w_dt = jnp.pad(w_dt_cols, ((0, 0), (0, LANES - HEADS))).astype(BF16)
    w_dtT = w_dt_cols.T.astype(BF16)
    w_mg = w_in0[:, N_MAIN + HEADS:].astype(BF16)
    nq = D_LRU // LANES
    wa = lru_wa[0].reshape(nq, 2, LRU_BLOCK_W, LRU_BLOCK_W)
    wx = lru_wx[0].reshape(nq, 2, LRU_BLOCK_W, LRU_BLOCK_W)
    zb = jnp.zeros((nq, LRU_BLOCK_W, LRU_BLOCK_W), F32)

    def blockdiag(w):
        top = jnp.concatenate([w[:, 0], zb], axis=2)
        bot = jnp.concatenate([zb, w[:, 1]], axis=2)
        return jnp.concatenate([top, bot], axis=1)

    w_gate = jnp.concatenate([blockdiag(wa), blockdiag(wx)], axis=2).astype(BF16)
    row = lambda v: v.reshape(1, -1)
    ba, bx, lam = row(lru_ba[0]), row(lru_bx[0]), row(lru_lambda[0])
    lcw, lcb = lru_conv_w[0], row(lru_conv_b[0])
    scw, scb = ssd_conv_w[0], row(ssd_conv_b[0])
    dtb = jnp.pad(row(ssd_dt_bias[0]), ((0, 0), (0, LANES - HEADS)))
    alog = jnp.pad(row(ssd_a_log[0]), ((0, 0), (0, LANES - HEADS)))
    dtbT = ssd_dt_bias[0].reshape(HEADS, 1)
    alogT = ssd_a_log[0].reshape(HEADS, 1)
    dexp = row(jnp.repeat(ssd_d[0], HEAD_DIM))
    normw = row(ssd_norm_w[0])
    wlp = w_lru_proj[0].astype(BF16)
    wsp = w_ssd_proj[0].astype(BF16)
    wo = w_out[0].astype(BF16)
    lng, lnb = row(ln_g[0]), row(ln_b[0])
    k_idx = jnp.arange(LANES)[:, None]
    c_idx = jnp.arange(D_SSD)[None, :]
    e3 = jnp.logical_and(k_idx % HEADS == c_idx // HEAD_DIM, k_idx < 3 * HEADS).astype(BF16)

    c_all = jnp.concatenate([c_prompt, c_sample], axis=0)
    mod = pl.pallas_call(
        _cond_kernel,
        out_shape=_whole((nbp + nbs, 3 * D_MODEL)),
        name="cond",
    )(c_all, w_cond[0].astype(BF16), row(b_cond[0]))
    mod_p = mod[:nbp].reshape(nbp, 1, 3 * D_MODEL)
    mod_s = mod[nbp:]

    weights = (w_main, w_dt, w_dtT, w_mg, lcw, lcb, w_gate, ba, bx, lam, scw, scb, dtb, dtbT,
               alog, alogT, dexp, normw, wlp, wsp, wo, lng, lnb, e3)
    y_p, lh_p, lc_p, sh_p, sc_p = pl.pallas_call(
        _prompt_kernel,
        grid=(nbp, seq // tc),
        in_specs=[pl.BlockSpec((1, tc, D_MODEL), lambda b, j: (b, j, 0)),
                  pl.BlockSpec((1, 1, 3 * D_MODEL), lambda b, j: (b, 0, 0))]
                 + [_resident(w.shape, 2) for w in weights],
        out_specs=[pl.BlockSpec((1, tc, D_MODEL), lambda b, j: (b, j, 0)),
                   pl.BlockSpec((1, 1, D_LRU), lambda b, j: (b, 0, 0)),
                   pl.BlockSpec((1, 1, CONV_W - 1, D_LRU), lambda b, j: (0, b, 0, 0)),
                   pl.BlockSpec((1, D_SSD, STATE), lambda b, j: (b, 0, 0)),
                   pl.BlockSpec((1, 1, CONV_W - 1, CONV_DIM), lambda b, j: (0, b, 0, 0))],
        out_shape=[_whole((nbp, seq, D_MODEL)), _whole((nbp, 1, D_LRU)),
                   _whole((1, nbp, CONV_W - 1, D_LRU)), _whole((nbp, D_SSD, STATE)),
                   _whole((1, nbp, CONV_W - 1, CONV_DIM))],
        scratch_shapes=[pltpu.VMEM((tc + SUBLANES, D_LRU), F32),
                        pltpu.VMEM((tc + SUBLANES, CONV_DIM), F32),
                        pltpu.VMEM((1, D_LRU), F32),
                        pltpu.VMEM((STATE, D_SSD), F32),
                        pltpu.VMEM((tc, D_LRU), F32),
                        pltpu.VMEM((tc, D_LRU), F32),
                        pltpu.VMEM((tc, D_LRU), F32)],
        compiler_params=pltpu.CompilerParams(
            dimension_semantics=("arbitrary", "arbitrary"), vmem_limit_bytes=VMEM_LIMIT),
        name="prompt",
    )(x_prompt, mod_p, *weights)

    xs2 = x_sample.reshape(nbs, D_MODEL)
    lst = jnp.transpose(state_lru_conv[0], (1, 0, 2))
    sst = jnp.transpose(state_ssd_conv[0], (1, 0, 2))
    lh_s, lco, sco, p_lru, xbc_s, xdt, da, z_s, gm_s = pl.pallas_call(
        _sample_in_kernel,
        out_shape=[_whole((nbs, D_LRU)), _whole((CONV_W - 1, nbs, D_LRU)),
                   _whole((CONV_W - 1, nbs, CONV_DIM)), _whole((nbs, D_MODEL)),
                   _whole((nbs, CONV_DIM)), _whole((nbs, D_SSD)), _whole((nbs, LANES)),
                   _whole((nbs, D_SSD)), _whole((nbs, 2 * D_MODEL))],
        compiler_params=pltpu.CompilerParams(vmem_limit_bytes=VMEM_LIMIT),
        name="sample_in",
    )(xs2, mod_s, state_lru_h[0], lst, sst, w_main, w_dt, w_mg, lcw, lcb, w_gate, ba, bx, lam,
      scw, scb, dtb, alog, wlp, e3)

    bb = SAMPLE_BB
    bm3 = xbc_s[:, D_SSD:D_SSD + GROUPS * STATE].reshape(nbs, GROUPS, STATE)
    cm3 = xbc_s[:, D_SSD + GROUPS * STATE:].reshape(nbs, GROUPS, STATE)
    sh_s, y_raw = pl.pallas_call(
        _sample_state_kernel,
        grid=(nbs // bb,),
        in_specs=[pl.BlockSpec(memory_space=pltpu.SMEM),
                  pl.BlockSpec((bb, HEADS, HEAD_DIM, STATE), lambda i: (i, 0, 0, 0)),
                  _resident((nbs, D_SSD), 1),
                  pl.BlockSpec((bb, GROUPS, STATE), lambda i: (i, 0, 0)),
                  pl.BlockSpec((bb, GROUPS, STATE), lambda i: (i, 0, 0))],
        out_specs=[pl.BlockSpec((bb, HEADS, HEAD_DIM, STATE), lambda i: (i, 0, 0, 0)),
                   pl.BlockSpec((bb, D_SSD), lambda i: (i, 0))],
        out_shape=[_whole((nbs, HEADS, HEAD_DIM, STATE)), _whole((nbs, D_SSD))],
        scratch_shapes=[pltpu.VMEM((D_SSD, nbs), F32), pltpu.VMEM((D_SSD, nbs), F32)],
        compiler_params=pltpu.CompilerParams(
            dimension_semantics=("arbitrary",), vmem_limit_bytes=VMEM_LIMIT),
        name="sample_state",
    )(da[:, :HEADS], state_ssd_h[0], xdt, bm3, cm3)

    y_s = pl.pallas_call(
        _sample_out_kernel,
        out_shape=_whole((nbs, D_MODEL)),
        compiler_params=pltpu.CompilerParams(vmem_limit_bytes=VMEM_LIMIT),
        name="sample_out",
    )(xs2, mod_s, y_raw, xbc_s, z_s, gm_s, p_lru, dexp, normw, wsp, wo, lng, lnb)

    return (y_p,
            y_s.reshape(nbs, 1, D_MODEL),
            lh_p.reshape(1, nbp, D_LRU),
            lc_p,
            sh_p.reshape(1, nbp, HEADS, HEAD_DIM, STATE),
            sc_p,
            lh_s.reshape(1, nbs, D_LRU),
            jnp.transpose(lco, (1, 0, 2)).reshape(1, nbs, CONV_W - 1, D_LRU),
            sh_s.reshape(1, nbs, HEADS, HEAD_DIM, STATE),
            jnp.transpose(sco, (1, 0, 2)).reshape(1, nbs, CONV_W - 1, CONV_DIM))
```

```python
import functools

import jax
import jax.numpy as jnp
from jax import lax
from jax.experimental import pallas as pl
from jax.experimental.pallas import tpu as pltpu

F32 = jnp.float32
BF16 = jnp.bfloat16

D_MODEL = 1024
D_LRU = 1024
LRU_BLOCK_W = 64
LRU_C = 8.0
D_SSD = 2048
HEAD_DIM = 64
HEADS = 32
GROUPS = 4
HEADS_PER_GROUP = HEADS // GROUPS
STATE = 128
GROUP_W = D_SSD // GROUPS
CHUNK = 128
CONV_DIM = D_SSD + 2 * GROUPS * STATE
CONV_W = 4
LN_EPS = 1e-5
RMS_EPS = 1e-5
ALPHA = 2.0 ** 0.25

LANES = 128
SUBLANES = 8
N_MAIN = 2 * D_LRU + D_SSD + CONV_DIM
OFF_LRU_X, OFF_LRU_Z, OFF_SSD_Z, OFF_XBC = 0, D_LRU, 2 * D_LRU, 2 * D_LRU + D_SSD

PROMPT_TC = 256
SAMPLE_BB = 8
VMEM_LIMIT = 60 * 1024 * 1024


def _sigmoid(x):
    return jax.nn.sigmoid(x)


def _silu(x):
    return x * _sigmoid(x)


def _softplus(x):
    return jnp.maximum(x, 0.0) + jnp.log1p(jnp.exp(-jnp.abs(x)))


def _dot(a, b):
    return jnp.dot(a, b, preferred_element_type=F32)


def _dot_nt(a, b):
    return lax.dot_general(a, b, (((1,), (1,)), ((), ())), preferred_element_type=F32)


def _dot_tn(a, b):
    return lax.dot_general(a, b, (((0,), (0,)), ((), ())), preferred_element_type=F32)


def _split3(x):
    hi = x.astype(BF16).astype(F32)
    r = x - hi
    mid = r.astype(BF16).astype(F32)
    lo = r - mid
    return hi, mid, lo


def _dot_exact_rhs(lhs_bf16, x):
    hi, mid, lo = _split3(x)
    return (_dot(lhs_bf16, hi.astype(BF16)) + _dot(lhs_bf16, mid.astype(BF16))
            + _dot(lhs_bf16, lo.astype(BF16)))


def _dot_exact_lhs(x, rhs_bf16):
    hi, mid, lo = _split3(x)
    return (_dot(hi.astype(BF16), rhs_bf16) + _dot(mid.astype(BF16), rhs_bf16)
            + _dot(lo.astype(BF16), rhs_bf16))


def _expand_heads(v, e3):
    lane = lax.broadcasted_iota(jnp.int32, v.shape, 1)
    v = jnp.where(lane < HEADS, v, 0.0)
    hi, mid, lo = _split3(v)
    packed = hi + pltpu.roll(mid, HEADS, axis=1) + pltpu.roll(lo, 2 * HEADS, axis=1)
    return _dot(packed.astype(BF16), e3)


def _lru_gates(u, w_gate_ref, ba, bx, lam):
    ub = u.astype(BF16)
    r_parts, i_parts = [], []
    for q in range(D_LRU // LANES):
        gq = _dot(ub[:, q * LANES:(q + 1) * LANES], w_gate_ref[q])
        r_parts.append(gq[:, :LANES])
        i_parts.append(gq[:, LANES:])
    r = _sigmoid(jnp.concatenate(r_parts, axis=1) + ba)
    i = _sigmoid(jnp.concatenate(i_parts, axis=1) + bx)
    log_a = (-LRU_C) * r * _softplus(-lam)
    a = jnp.exp(log_a)
    mult = jnp.sqrt(-jnp.tanh(log_a) * (a * a + 1.0))
    return a, mult, i


def _gated_rmsnorm(y, z, norm_w):
    g = y * _silu(z)
    parts = []
    for gi in range(GROUPS):
        gg = g[:, gi * GROUP_W:(gi + 1) * GROUP_W]
        ms = jnp.mean(gg * gg, axis=-1, keepdims=True)
        parts.append(gg * lax.rsqrt(ms + RMS_EPS))
    return jnp.concatenate(parts, axis=1) * norm_w


def _layer_norm(v, g, b):
    mu = jnp.mean(v, axis=-1, keepdims=True)
    d = v - mu
    var = jnp.mean(d * d, axis=-1, keepdims=True)
    return d * lax.rsqrt(var + LN_EPS) * g + b


def _cond_kernel(c_ref, w_ref, b_ref, o_ref):
    o_ref[...] = _dot(c_ref[...].astype(BF16), w_ref[...]) + b_ref[...]


def _prompt_kernel(x_ref, mod_ref, w_main, w_dt, w_dtT, w_mg, lcw, lcb, w_gate, ba, bx, lam,
                   scw, scb, dtb, dtbT, alog, alogT, dexp, normw, wlp, wsp, wo, lng, lnb, e3,
                   y_ref, lh_ref, lc_ref, sh_ref, sc_ref,
                   lbuf, sbuf, hl, hT, a_s, b_s, hs_s):
    tc = x_ref.shape[1]
    j = pl.program_id(1)
    last = pl.num_programs(1) - 1

    @pl.when(j == 0)
    def _():
        lbuf[0:SUBLANES, :] = jnp.zeros((SUBLANES, D_LRU), F32)
        sbuf[0:SUBLANES, :] = jnp.zeros((SUBLANES, CONV_DIM), F32)
        hl[...] = jnp.zeros_like(hl)
        hT[...] = jnp.zeros_like(hT)

    x = x_ref[0]
    mod = mod_ref[0]
    shift = mod[:, :D_MODEL]
    scale = mod[:, D_MODEL:2 * D_MODEL]
    gate = mod[:, 2 * D_MODEL:]
    hb = (x * (1.0 + scale) + shift).astype(BF16)

    lbuf[SUBLANES:SUBLANES + tc, :] = _dot(hb, w_main[:, OFF_LRU_X:OFF_LRU_X + D_LRU])
    sbuf[SUBLANES:SUBLANES + tc, :] = _dot(hb, w_main[:, OFF_XBC:OFF_XBC + CONV_DIM])
    base = SUBLANES - (CONV_W - 1)
    u = lcb[...] + lcw[0:1, :] * lbuf[base:base + tc, :]
    xbc = scb[...] + scw[0:1, :] * sbuf[base:base + tc, :]
    for k in range(1, CONV_W):
        u = u + lcw[k:k + 1, :] * lbuf[base + k:base + k + tc, :]
        xbc = xbc + scw[k:k + 1, :] * sbuf[base + k:base + k + tc, :]
    xbc = _silu(xbc)

    @pl.when(j == last)
    def _():
        lc_ref[0, 0] = lbuf[SUBLANES + tc - (CONV_W - 1):SUBLANES + tc, :]
        sc_ref[0, 0] = sbuf[SUBLANES + tc - (CONV_W - 1):SUBLANES + tc, :]

    lbuf[0:SUBLANES, :] = lbuf[tc:tc + SUBLANES, :]
    sbuf[0:SUBLANES, :] = sbuf[tc:tc + SUBLANES, :]

    a, mult, ig = _lru_gates(u, w_gate, ba[...], bx[...], lam[...])
    iu = ig * u
    row = lax.broadcasted_iota(jnp.int32, (tc, 1), 0)
    first = jnp.logical_and(row == 0, j == 0)
    a_s[...] = a
    b_s[...] = jnp.where(first, iu, mult * iu)

    def scan_body(t, h):
        h = a_s[pl.ds(t, 1), :] * h + b_s[pl.ds(t, 1), :]
        hs_s[pl.ds(t, 1), :] = h
        return h

    h_last = lax.fori_loop(0, tc, scan_body, hl[...], unroll=8)
    hl[...] = h_last

    @pl.when(j == last)
    def _():
        lh_ref[0] = h_last

    lru_z = _dot(hb, w_main[:, OFF_LRU_Z:OFF_LRU_Z + D_LRU])
    y_lru = hs_s[...] * _silu(lru_z)
    p_lru = _dot(y_lru.astype(BF16), wlp[...])

    xs = xbc[:, :D_SSD]
    bm = xbc[:, D_SSD:D_SSD + GROUPS * STATE]
    cm = xbc[:, D_SSD + GROUPS * STATE:]
    dt = _softplus(_dot(hb, w_dt[...]) + dtb[...])
    dtT = _softplus(_dot_nt(w_dtT[...], hb) + dtbT[...])
    a_row = -jnp.exp(alog[...])
    a_col = -jnp.exp(alogT[...])

    ri = lax.broadcasted_iota(jnp.int32, (CHUNK, CHUNK), 0)
    ci = lax.broadcasted_iota(jnp.int32, (CHUNK, CHUNK), 1)
    causal = ri >= ci
    tril = causal.astype(BF16)
    triu = (ri <= ci).astype(BF16)
    lane = lax.broadcasted_iota(jnp.int32, (CHUNK, LANES), 1)
    lo_half = lane < HEAD_DIM

    y_chunks = []
    for c in range(tc // CHUNK):
        r0 = c * CHUNK
        xs_c = xs[r0:r0 + CHUNK]
        xs_cb = xs_c.astype(BF16)
        bm_c = bm[r0:r0 + CHUNK].astype(BF16)
        cm_c = cm[r0:r0 + CHUNK].astype(BF16)
        dt_c = dt[r0:r0 + CHUNK]
        dtT_c = dtT[:, r0:r0 + CHUNK]
        acs = _dot_exact_rhs(tril, dt_c * a_row)
        acsT = _dot_exact_lhs(dtT_c * a_col, triu)
        e_exp = _expand_heads(jnp.exp(acs), e3[...])
        w_exp = _expand_heads(jnp.exp(acs[CHUNK - 1:CHUNK, :] - acs) * dt_c, e3[...])

        y_parts = []
        for g in range(GROUPS):
            cg = cm_c[:, g * STATE:(g + 1) * STATE]
            bg = bm_c[:, g * STATE:(g + 1) * STATE]
            cb = _dot_nt(cg, bg)
            for q in range(HEADS_PER_GROUP // 2):
                pair = g * (HEADS_PER_GROUP // 2) + q
                xp = xs_cb[:, pair * LANES:(pair + 1) * LANES]
                yp = None
                for half in range(2):
                    h = 2 * pair + half
                    seg = acs[:, h:h + 1] - acsT[h:h + 1, :]
                    m = jnp.exp(jnp.where(causal, seg, -jnp.inf)) * cb * dtT_c[h:h + 1, :]
                    keep = lo_half if half == 0 else jnp.logical_not(lo_half)
                    xh = jnp.where(keep, xp, jnp.zeros_like(xp))
                    t = _dot(m.astype(BF16), xh)
                    yp = t if yp is None else yp + t
                y_parts.append(yp)
        y_diag = jnp.concatenate(y_parts, axis=1)

        h_prev = hT[...]
        h_prev_b = h_prev.astype(BF16)
        xsw = (xs_c * w_exp).astype(BF16)
        off_parts, st_parts = [], []
        for g in range(GROUPS):
            sl = slice(g * GROUP_W, (g + 1) * GROUP_W)
            off_parts.append(_dot(cm_c[:, g * STATE:(g + 1) * STATE], h_prev_b[:, sl]))
            st_parts.append(_dot_tn(bm_c[:, g * STATE:(g + 1) * STATE], xsw[:, sl]))
        y_off = jnp.concatenate(off_parts, axis=1) * e_exp
        hT[...] = e_exp[CHUNK - 1:CHUNK, :] * h_prev + jnp.concatenate(st_parts, axis=1)
        y_chunks.append(y_diag + y_off + dexp[...] * xs_c)
    y = jnp.concatenate(y_chunks, axis=0) if len(y_chunks) > 1 else y_chunks[0]

    @pl.when(j == last)
    def _():
        hfin = hT[...]
        for q in range(D_SSD // LANES):
            sh_ref[0, q * LANES:(q + 1) * LANES, :] = hfin[:, q * LANES:(q + 1) * LANES].T

    ssd_z = _dot(hb, w_main[:, OFF_SSD_Z:OFF_SSD_Z + D_SSD])
    y_ssd = _gated_rmsnorm(y, ssd_z, normw[...])
    p_ssd = _dot(y_ssd.astype(BF16), wsp[...])

    gm = _sigmoid(_dot(hb, w_mg[...]))
    merged = gm[:, :D_MODEL] * p_lru + gm[:, D_MODEL:] * p_ssd
    out = _dot(merged.astype(BF16), wo[...])
    y_ref[0] = _layer_norm(ALPHA * x + gate * out, lng[...], lnb[...])


def _sample_in_kernel(x_ref, mod_ref, lh0_ref, lst_ref, sst_ref, w_main, w_dt, w_mg, lcw, lcb,
                      w_gate, ba, bx, lam, scw, scb, dtb, alog, wlp, e3,
                      lh_ref, lco_ref, sco_ref, plru_ref, xbc_ref, xdt_ref, da_ref, z_ref, gm_ref):
    x = x_ref[...]
    mod = mod_ref[...]
    shift = mod[:, :D_MODEL]
    scale = mod[:, D_MODEL:2 * D_MODEL]
    hb = (x * (1.0 + scale) + shift).astype(BF16)

    lru_x = _dot(hb, w_main[:, OFF_LRU_X:OFF_LRU_X + D_LRU])
    xbc_raw = _dot(hb, w_main[:, OFF_XBC:OFF_XBC + CONV_DIM])
    u = lcb[...] + lcw[CONV_W - 1:CONV_W, :] * lru_x
    xbc = scb[...] + scw[CONV_W - 1:CONV_W, :] * xbc_raw
    for k in range(CONV_W - 1):
        u = u + lcw[k:k + 1, :] * lst_ref[:, k, :]
        xbc = xbc + scw[k:k + 1, :] * sst_ref[:, k, :]
    for k in range(CONV_W - 2):
        lco_ref[:, k, :] = lst_ref[:, k + 1, :]
        sco_ref[:, k, :] = sst_ref[:, k + 1, :]
    lco_ref[:, CONV_W - 2, :] = lru_x
    sco_ref[:, CONV_W - 2, :] = xbc_raw
    xbc = _silu(xbc)
    xbc_ref[...] = xbc

    a, mult, ig = _lru_gates(u, w_gate, ba[...], bx[...], lam[...])
    h_new = a * lh0_ref[...] + mult * ig * u
    lh_ref[...] = h_new
    lru_z = _dot(hb, w_main[:, OFF_LRU_Z:OFF_LRU_Z + D_LRU])
    plru_ref[...] = _dot((h_new * _silu(lru_z)).astype(BF16), wlp[...])

    dt = _softplus(_dot(hb, w_dt[...]) + dtb[...])
    da_ref[...] = jnp.exp(dt * (-jnp.exp(alog[...])))
    xdt_ref[...] = xbc[:, :D_SSD] * _expand_heads(dt, e3[...])
    z_ref[...] = _dot(hb, w_main[:, OFF_SSD_Z:OFF_SSD_Z + D_SSD])
    gm_ref[...] = _sigmoid(_dot(hb, w_mg[...]))


def _sample_state_kernel(da_ref, h0_ref, xdt_ref, b_ref, c_ref, ho_ref, y_ref):
    i = pl.program_id(0)
    bb = h0_ref.shape[0]
    n_terms = 6
    pad_rows = LANES - n_terms * bb

    def stack(terms, width):
        return jnp.concatenate(list(terms) + [jnp.zeros((pad_rows, width), F32)], axis=0)

    xh, xm, xl = _split3(xdt_ref[...])
    bh, bm, bl = _split3(b_ref[...])
    x6t = stack((xh, xh, xm, xm, xh, xl), D_SSD).T.astype(BF16)
    b6 = stack((bh, bm, bh, bm, bl, bh), GROUPS * STATE)
    ch, cmid, cl = _split3(c_ref[...])
    c3 = jnp.concatenate([ch, cmid, cl], axis=0).astype(BF16)
    krow = lax.broadcasted_iota(jnp.int32, (LANES, 1), 0)
    yrow = lax.broadcasted_iota(jnp.int32, (3 * bb, 1), 0)
    for b in range(bb):
        own = jnp.logical_and(jnp.bitwise_and(krow, bb - 1) == b, krow < n_terms * bb)
        rhs = jnp.where(own, b6, 0.0).astype(BF16)
        y_own = jnp.bitwise_and(yrow, bb - 1) == b
        y_parts = []
        for g in range(GROUPS):
            st = _dot(x6t[g * GROUP_W:(g + 1) * GROUP_W, :], rhs[:, g * STATE:(g + 1) * STATE])
            hn_parts = []
            for hh in range(HEADS_PER_GROUP):
                h = g * HEADS_PER_GROUP + hh
                rows = slice(h * HEAD_DIM, (h + 1) * HEAD_DIM)
                hn = da_ref[i * bb + b, h] * h0_ref[b, rows, :] + st[hh * HEAD_DIM:(hh + 1) * HEAD_DIM, :]
                ho_ref[b, rows, :] = hn
                hn_parts.append(hn.astype(BF16))
            hn_g = jnp.concatenate(hn_parts, axis=0)
            r = _dot_nt(c3[:, g * STATE:(g + 1) * STATE], hn_g)
            y_parts.append(jnp.sum(jnp.where(y_own, r, 0.0), axis=0, keepdims=True))
        y_ref[b:b + 1, :] = jnp.concatenate(y_parts, axis=1)


def _sample_out_kernel(x_ref, mod_ref, yraw_ref, xbc_ref, z_ref, gm_ref, plru_ref,
                       dexp, normw, wsp, wo, lng, lnb, o_ref):
    x = x_ref[...]
    gate = mod_ref[:, 2 * D_MODEL:]
    y = yraw_ref[...] + dexp[...] * xbc_ref[:, :D_SSD]
    y_ssd = _gated_rmsnorm(y, z_ref[...], normw[...])
    p_ssd = _dot(y_ssd.astype(BF16), wsp[...])
    gm = gm_ref[...]
    merged = gm[:, :D_MODEL] * plru_ref[...] + gm[:, D_MODEL:] * p_ssd
    out = _dot(merged.astype(BF16), wo[...])
    o_ref[...] = _layer_norm(ALPHA * x + gate * out, lng[...], lnb[...])


def _resident(shape, grid_rank):
    zeros = (0,) * len(shape)
    if grid_rank == 1:
        imap = lambda i: zeros
    else:
        imap = lambda b, j: zeros
    return pl.BlockSpec(shape, imap, pipeline_mode=pl.Buffered(1))


def _whole(shape):
    return jax.ShapeDtypeStruct(shape, F32)


def kernel(x_prompt, x_sample, state_lru_h, state_lru_conv, state_ssd_h, state_ssd_conv, c_prompt, c_sample, w_cond, b_cond, w_in, lru_conv_w, lru_conv_b, lru_wa, lru_ba, lru_wx, lru_bx, lru_lambda, ssd_conv_w, ssd_conv_b, ssd_dt_bias, ssd_a_log, ssd_d, ssd_norm_w, w_lru_proj, w_ssd_proj, w_out, ln_g, ln_b):
    nbp, seq, _ = x_prompt.shape
    nbs = x_sample.shape[0]
    tc = PROMPT_TC
    assert seq % tc == 0 and tc % CHUNK == 0 and nbs % SAMPLE_BB == 0 and SAMPLE_BB == SUBLANES

    w_in0 = w_in[0]
    w_main = w_in0[:, :N_MAIN].astype(BF16)
    w_dt_cols = w_in0[:, N_MAIN:N_MAIN + HEADS]
    w_dt = jnp.pad(w_dt_cols, ((0, 0), (0, LANES - HEADS))).astype(BF16)
    w_dtT = w_dt_cols.T.astype(BF16)
    w_mg = w_in0[:, N_MAIN + HEADS:].astype(BF16)
    nq = D_LRU // LANES
    wa = lru_wa[0].reshape(nq, 2, LRU_BLOCK_W, LRU_BLOCK_W)
    wx = lru_wx[0].reshape(nq, 2, LRU_BLOCK_W, LRU_BLOCK_W)
    zb = jnp.zeros((nq, LRU_BLOCK_W, LRU_BLOCK_W), F32)

    def blockdiag(w):
        top = jnp.concatenate([w[:, 0], zb], axis=2)
        bot = jnp.concatenate([zb, w[:, 1]], axis=2)
        return jnp.concatenate([top, bot], axis=1)

    w_gate = jnp.concatenate([blockdiag(wa), blockdiag(wx)], axis=2).astype(BF16)
    row = lambda v: v.reshape(1, -1)
    ba, bx, lam = row(lru_ba[0]), row(lru_bx[0]), row(lru_lambda[0])
    lcw, lcb = lru_conv_w[0], row(lru_conv_b[0])
    scw, scb = ssd_conv_w[0], row(ssd_conv_b[0])
    dtb = jnp.pad(row(ssd_dt_bias[0]), ((0, 0), (0, LANES - HEADS)))
    alog = jnp.pad(row(ssd_a_log[0]), ((0, 0), (0, LANES - HEADS)))
    dtbT = ssd_dt_bias[0].reshape(HEADS, 1)
    alogT = ssd_a_log[0].reshape(HEADS, 1)
    dexp = row(jnp.repeat(ssd_d[0], HEAD_DIM))
    normw = row(ssd_norm_w[0])
    wlp = w_lru_proj[0].astype(BF16)
    wsp = w_ssd_proj[0].astype(BF16)
    wo = w_out[0].astype(BF16)
    lng, lnb = row(ln_g[0]), row(ln_b[0])
    k_idx = jnp.arange(LANES)[:, None]
    c_idx = jnp.arange(D_SSD)[None, :]
    e3 = jnp.logical_and(k_idx % HEADS == c_idx // HEAD_DIM, k_idx < 3 * HEADS).astype(BF16)

    c_all = jnp.concatenate([c_prompt, c_sample], axis=0)
    mod = pl.pallas_call(
        _cond_kernel,
        out_shape=_whole((nbp + nbs, 3 * D_MODEL)),
        name="cond",
    )(c_all, w_cond[0].astype(BF16), row(b_cond[0]))
    mod_p = mod[:nbp].reshape(nbp, 1, 3 * D_MODEL)
    mod_s = mod[nbp:]

    weights = (w_main, w_dt, w_dtT, w_mg, lcw, lcb, w_gate, ba, bx, lam, scw, scb, dtb, dtbT,
               alog, alogT, dexp, normw, wlp, wsp, wo, lng, lnb, e3)
    y_p, lh_p, lc_p, sh_p, sc_p = pl.pallas_call(
        _prompt_kernel,
        grid=(nbp, seq // tc),
        in_specs=[pl.BlockSpec((1, tc, D_MODEL), lambda b, j: (b, j, 0)),
                  pl.BlockSpec((1, 1, 3 * D_MODEL), lambda b, j: (b, 0, 0))]
                 + [_resident(w.shape, 2) for w in weights],
        out_specs=[pl.BlockSpec((1, tc, D_MODEL), lambda b, j: (b, j, 0)),
                   pl.BlockSpec((1, 1, D_LRU), lambda b, j: (b, 0, 0)),
                   pl.BlockSpec((1, 1, CONV_W - 1, D_LRU), lambda b, j: (0, b, 0, 0)),
                   pl.BlockSpec((1, D_SSD, STATE), lambda b, j: (b, 0, 0)),
                   pl.BlockSpec((1, 1, CONV_W - 1, CONV_DIM), lambda b, j: (0, b, 0, 0))],
        out_shape=[_whole((nbp, seq, D_MODEL)), _whole((nbp, 1, D_LRU)),
                   _whole((1, nbp, CONV_W - 1, D_LRU)), _whole((nbp, D_SSD, STATE)),
                   _whole((1, nbp, CONV_W - 1, CONV_DIM))],
        scratch_shapes=[pltpu.VMEM((tc + SUBLANES, D_LRU), F32),
                        pltpu.VMEM((tc + SUBLANES, CONV_DIM), F32),
                        pltpu.VMEM((1, D_LRU), F32),
                        pltpu.VMEM((STATE, D_SSD), F32),
                        pltpu.VMEM((tc, D_LRU), F32),
                        pltpu.VMEM((tc, D_LRU), F32),
                        pltpu.VMEM((tc, D_LRU), F32)],
        compiler_params=pltpu.CompilerParams(
            dimension_semantics=("arbitrary", "arbitrary"), vmem_limit_bytes=VMEM_LIMIT),
        name="prompt",
    )(x_prompt, mod_p, *weights)

    xs2 = x_sample.reshape(nbs, D_MODEL)
    lh_s, lco, sco, p_lru, xbc_s, xdt, da, z_s, gm_s = pl.pallas_call(
        _sample_in_kernel,
        out_shape=[_whole((nbs, D_LRU)), _whole((nbs, CONV_W - 1, D_LRU)),
                   _whole((nbs, CONV_W - 1, CONV_DIM)), _whole((nbs, D_MODEL)),
                   _whole((nbs, CONV_DIM)), _whole((nbs, D_SSD)), _whole((nbs, LANES)),
                   _whole((nbs, D_SSD)), _whole((nbs, 2 * D_MODEL))],
        compiler_params=pltpu.CompilerParams(vmem_limit_bytes=VMEM_LIMIT),
        name="sample_in",
    )(xs2, mod_s, state_lru_h[0], state_lru_conv[0], state_ssd_conv[0], w_main, w_dt, w_mg,
      lcw, lcb, w_gate, ba, bx, lam, scw, scb, dtb, alog, wlp, e3)

    bb = SAMPLE_BB
    n_bc = GROUPS * STATE
    sh_s, y_raw = pl.pallas_call(
        _sample_state_kernel,
        grid=(nbs // bb,),
        in_specs=[pl.BlockSpec(memory_space=pltpu.SMEM),
                  pl.BlockSpec((bb, D_SSD, STATE), lambda i: (i, 0, 0)),
                  pl.BlockSpec((bb, D_SSD), lambda i: (i, 0)),
                  pl.BlockSpec((bb, n_bc), lambda i: (i, D_SSD // n_bc)),
                  pl.BlockSpec((bb, n_bc), lambda i: (i, D_SSD // n_bc + 1))],
        out_specs=[pl.BlockSpec((bb, D_SSD, STATE), lambda i: (i, 0, 0)),
                   pl.BlockSpec((bb, D_SSD), lambda i: (i, 0))],
        out_shape=[_whole((nbs, D_SSD, STATE)), _whole((nbs, D_SSD))],
        compiler_params=pltpu.CompilerParams(
            dimension_semantics=("arbitrary",), vmem_limit_bytes=VMEM_LIMIT),
        name="sample_state",
    )(da[:, :HEADS], state_ssd_h[0].reshape(nbs, D_SSD, STATE), xdt, xbc_s, xbc_s)

    y_s = pl.pallas_call(
        _sample_out_kernel,
        out_shape=_whole((nbs, D_MODEL)),
        compiler_params=pltpu.CompilerParams(vmem_limit_bytes=VMEM_LIMIT),
        name="sample_out",
    )(xs2, mod_s, y_raw, xbc_s, z_s, gm_s, p_lru, dexp, normw, wsp, wo, lng, lnb)

    return (y_p,
            y_s.reshape(nbs, 1, D_MODEL),
            lh_p.reshape(1, nbp, D_LRU),
            lc_p,
            sh_p.reshape(1, nbp, HEADS, HEAD_DIM, STATE),
            sc_p,
            lh_s.reshape(1, nbs, D_LRU),
            lco.reshape(1, nbs, CONV_W - 1, D_LRU),
            sh_s.reshape(1, nbs, HEADS, HEAD_DIM, STATE),
            sco.reshape(1, nbs, CONV_W - 1, CONV_DIM))
```

```python
import functools

import jax
import jax.numpy as jnp
from jax import lax
from jax.experimental import pallas as pl
from jax.experimental.pallas import tpu as pltpu

F32 = jnp.float32
BF16 = jnp.bfloat16

D_MODEL = 1024
D_LRU = 1024
LRU_BLOCK_W = 64
LRU_C = 8.0
D_SSD = 2048
HEAD_DIM = 64
HEADS = 32
GROUPS = 4
HEADS_PER_GROUP = HEADS // GROUPS
STATE = 128
GROUP_W = D_SSD // GROUPS
CHUNK = 128
CONV_DIM = D_SSD + 2 * GROUPS * STATE
CONV_W = 4
LN_EPS = 1e-5
RMS_EPS = 1e-5
ALPHA = 2.0 ** 0.25
LOG2_E = 1.4426950408889634

LANES = 128
SUBLANES = 8
N_MAIN = 2 * D_LRU + D_SSD + CONV_DIM
OFF_LRU_X, OFF_LRU_Z, OFF_SSD_Z, OFF_XBC = 0, D_LRU, 2 * D_LRU, 2 * D_LRU + D_SSD

PROMPT_TC = 256
SAMPLE_BB = 8
VMEM_LIMIT = 60 * 1024 * 1024


def _sigmoid(x):
    return 0.5 + 0.5 * jnp.tanh(0.5 * x)


def _silu(x):
    h = 0.5 * x
    return h + h * jnp.tanh(h)


def _softplus(x):
    return jnp.maximum(x, 0.0) + jnp.log1p(jnp.exp(-jnp.abs(x)))


def _dot(a, b):
    return jnp.dot(a, b, preferred_element_type=F32)


def _dot_nt(a, b):
    return lax.dot_general(a, b, (((1,), (1,)), ((), ())), preferred_element_type=F32)


def _dot_tn(a, b):
    return lax.dot_general(a, b, (((0,), (0,)), ((), ())), preferred_element_type=F32)


def _split3(x):
    hi = x.astype(BF16).astype(F32)
    r = x - hi
    mid = r.astype(BF16).astype(F32)
    lo = r - mid
    return hi, mid, lo


def _dot_exact_rhs(lhs_bf16, x):
    hi, mid, lo = _split3(x)
    return (_dot(lhs_bf16, hi.astype(BF16)) + _dot(lhs_bf16, mid.astype(BF16))
            + _dot(lhs_bf16, lo.astype(BF16)))


def _dot_exact_lhs(x, rhs_bf16):
    hi, mid, lo = _split3(x)
    return (_dot(hi.astype(BF16), rhs_bf16) + _dot(mid.astype(BF16), rhs_bf16)
            + _dot(lo.astype(BF16), rhs_bf16))


def _expand_heads(v, e3):
    lane = lax.broadcasted_iota(jnp.int32, v.shape, 1)
    v = jnp.where(lane < HEADS, v, 0.0)
    hi, mid, lo = _split3(v)
    packed = hi + pltpu.roll(mid, HEADS, axis=1) + pltpu.roll(lo, 2 * HEADS, axis=1)
    return _dot(packed.astype(BF16), e3)


def _lru_gates(u, w_gate_ref, ba, bx, lam):
    ub = u.astype(BF16)
    r_parts, i_parts = [], []
    for q in range(D_LRU // LANES):
        gq = _dot(ub[:, q * LANES:(q + 1) * LANES], w_gate_ref[q])
        r_parts.append(gq[:, :LANES])
        i_parts.append(gq[:, LANES:])
    r = _sigmoid(jnp.concatenate(r_parts, axis=1) + ba)
    i = _sigmoid(jnp.concatenate(i_parts, axis=1) + bx)
    log_a = (-LRU_C) * r * _softplus(-lam)
    a = jnp.exp(log_a)
    v = -jnp.tanh(log_a) * (a * a + 1.0)
    mult = jnp.where(v > 0.0, v * lax.rsqrt(v), 0.0)
    return a, mult, i


def _gated_rmsnorm(y, z, norm_w):
    g = y * _silu(z)
    parts = []
    for gi in range(GROUPS):
        gg = g[:, gi * GROUP_W:(gi + 1) * GROUP_W]
        ms = jnp.mean(gg * gg, axis=-1, keepdims=True)
        parts.append(gg * lax.rsqrt(ms + RMS_EPS))
    return jnp.concatenate(parts, axis=1) * norm_w


def _layer_norm(v, g, b):
    mu = jnp.mean(v, axis=-1, keepdims=True)
    d = v - mu
    var = jnp.mean(d * d, axis=-1, keepdims=True)
    return d * lax.rsqrt(var + LN_EPS) * g + b


def _cond_kernel(c_ref, w_ref, b_ref, o_ref):
    o_ref[...] = _dot(c_ref[...].astype(BF16), w_ref[...]) + b_ref[...]


def _prompt_kernel(x_ref, mod_ref, w_main, w_dt, w_dtT, w_mg, lcw, lcb, w_gate, ba, bx, lam,
                   scw, scb, dtb, dtbT, alog, alogT, dexp, normw, wlp, wsp, wo, lng, lnb, e3,
                   y_ref, lh_ref, lc_ref, sh_ref, sc_ref,
                   lbuf, sbuf, hl, hT, a_s, b_s, hs_s):
    tc = x_ref.shape[1]
    j = pl.program_id(1)
    last = pl.num_programs(1) - 1

    @pl.when(j == 0)
    def _():
        lbuf[0:SUBLANES, :] = jnp.zeros((SUBLANES, D_LRU), F32)
        sbuf[0:SUBLANES, :] = jnp.zeros((SUBLANES, CONV_DIM), F32)
        hl[...] = jnp.zeros_like(hl)
        hT[...] = jnp.zeros_like(hT)

    x = x_ref[0]
    mod = mod_ref[0]
    shift = mod[:, :D_MODEL]
    scale = mod[:, D_MODEL:2 * D_MODEL]
    gate = mod[:, 2 * D_MODEL:]
    hb = (x * (1.0 + scale) + shift).astype(BF16)

    lbuf[SUBLANES:SUBLANES + tc, :] = _dot(hb, w_main[:, OFF_LRU_X:OFF_LRU_X + D_LRU])
    sbuf[SUBLANES:SUBLANES + tc, :] = _dot(hb, w_main[:, OFF_XBC:OFF_XBC + CONV_DIM])
    base = SUBLANES - (CONV_W - 1)
    u = lcb[...] + lcw[0:1, :] * lbuf[base:base + tc, :]
    xbc = scb[...] + scw[0:1, :] * sbuf[base:base + tc, :]
    for k in range(1, CONV_W):
        u = u + lcw[k:k + 1, :] * lbuf[base + k:base + k + tc, :]
        xbc = xbc + scw[k:k + 1, :] * sbuf[base + k:base + k + tc, :]
    xbc = _silu(xbc)

    @pl.when(j == last)
    def _():
        lc_ref[0, 0] = lbuf[SUBLANES + tc - (CONV_W - 1):SUBLANES + tc, :]
        sc_ref[0, 0] = sbuf[SUBLANES + tc - (CONV_W - 1):SUBLANES + tc, :]

    lbuf[0:SUBLANES, :] = lbuf[tc:tc + SUBLANES, :]
    sbuf[0:SUBLANES, :] = sbuf[tc:tc + SUBLANES, :]

    a, mult, ig = _lru_gates(u, w_gate, ba[...], bx[...], lam[...])
    iu = ig * u
    row = lax.broadcasted_iota(jnp.int32, (tc, 1), 0)
    first = jnp.logical_and(row == 0, j == 0)
    a_s[...] = a
    b_s[...] = jnp.where(first, iu, mult * iu)

    def scan_body(t, h):
        h = a_s[pl.ds(t, 1), :] * h + b_s[pl.ds(t, 1), :]
        hs_s[pl.ds(t, 1), :] = h
        return h

    h_last = lax.fori_loop(0, tc, scan_body, hl[...], unroll=8)
    hl[...] = h_last

    @pl.when(j == last)
    def _():
        lh_ref[0] = h_last

    lru_z = _dot(hb, w_main[:, OFF_LRU_Z:OFF_LRU_Z + D_LRU])
    y_lru = hs_s[...] * _silu(lru_z)
    p_lru = _dot(y_lru.astype(BF16), wlp[...])

    xs = xbc[:, :D_SSD]
    bm = xbc[:, D_SSD:D_SSD + GROUPS * STATE]
    cm = xbc[:, D_SSD + GROUPS * STATE:]
    dt = _softplus(_dot(hb, w_dt[...]) + dtb[...])
    dtT = _softplus(_dot_nt(w_dtT[...], hb) + dtbT[...])
    a_row = -LOG2_E * jnp.exp(alog[...])
    a_col = -LOG2_E * jnp.exp(alogT[...])

    ri = lax.broadcasted_iota(jnp.int32, (CHUNK, CHUNK), 0)
    ci = lax.broadcasted_iota(jnp.int32, (CHUNK, CHUNK), 1)
    causal = ri >= ci
    tril = causal.astype(BF16)
    triu = (ri <= ci).astype(BF16)
    lane = lax.broadcasted_iota(jnp.int32, (CHUNK, LANES), 1)
    lo_half = lane < HEAD_DIM

    y_chunks = []
    for c in range(tc // CHUNK):
        r0 = c * CHUNK
        xs_c = xs[r0:r0 + CHUNK]
        xs_cb = xs_c.astype(BF16)
        bm_c = bm[r0:r0 + CHUNK].astype(BF16)
        cm_c = cm[r0:r0 + CHUNK].astype(BF16)
        dt_c = dt[r0:r0 + CHUNK]
        dtT_c = dtT[:, r0:r0 + CHUNK]
        acs = _dot_exact_rhs(tril, dt_c * a_row)
        acsT = _dot_exact_lhs(dtT_c * a_col, triu)
        e_exp = _expand_heads(jnp.exp2(acs), e3[...])
        w_exp = _expand_heads(jnp.exp2(acs[CHUNK - 1:CHUNK, :] - acs) * dt_c, e3[...])

        y_parts = []
        for g in range(GROUPS):
            cg = cm_c[:, g * STATE:(g + 1) * STATE]
            bg = bm_c[:, g * STATE:(g + 1) * STATE]
            cb = _dot_nt(cg, bg)
            for q in range(HEADS_PER_GROUP // 2):
                pair = g * (HEADS_PER_GROUP // 2) + q
                xp = xs_cb[:, pair * LANES:(pair + 1) * LANES]
                zero = jnp.zeros_like(xp)
                ms = []
                for half in range(2):
                    h = 2 * pair + half
                    seg = acs[:, h:h + 1] - acsT[h:h + 1, :]
                    m = jnp.exp2(jnp.where(causal, seg, -jnp.inf)) * cb * dtT_c[h:h + 1, :]
                    ms.append(m.astype(BF16))
                x2 = jnp.concatenate([jnp.where(lo_half, xp, zero), jnp.where(lo_half, zero, xp)], axis=0)
                y_parts.append(_dot(jnp.concatenate(ms, axis=1), x2))
        y_diag = jnp.concatenate(y_parts, axis=1)

        h_prev = hT[...]
        h_prev_b = h_prev.astype(BF16)
        xsw = (xs_c * w_exp).astype(BF16)
        off_parts, st_parts = [], []
        for g in range(GROUPS):
            sl = slice(g * GROUP_W, (g + 1) * GROUP_W)
            off_parts.append(_dot(cm_c[:, g * STATE:(g + 1) * STATE], h_prev_b[:, sl]))
            st_parts.append(_dot_tn(bm_c[:, g * STATE:(g + 1) * STATE], xsw[:, sl]))
        y_off = jnp.concatenate(off_parts, axis=1) * e_exp
        hT[...] = e_exp[CHUNK - 1:CHUNK, :] * h_prev + jnp.concatenate(st_parts, axis=1)
        y_chunks.append(y_diag + y_off + dexp[...] * xs_c)
    y = jnp.concatenate(y_chunks, axis=0) if len(y_chunks) > 1 else y_chunks[0]

    @pl.when(j == last)
    def _():
        hfin = hT[...]
        for q in range(D_SSD // LANES):
            sh_ref[0, q * LANES:(q + 1) * LANES, :] = hfin[:, q * LANES:(q + 1) * LANES].T

    ssd_z = _dot(hb, w_main[:, OFF_SSD_Z:OFF_SSD_Z + D_SSD])
    y_ssd = _gated_rmsnorm(y, ssd_z, normw[...])
    p_ssd = _dot(y_ssd.astype(BF16), wsp[...])

    gm = _sigmoid(_dot(hb, w_mg[...]))
    merged = gm[:, :D_MODEL] * p_lru + gm[:, D_MODEL:] * p_ssd
    out = _dot(merged.astype(BF16), wo[...])
    y_ref[0] = _layer_norm(ALPHA * x + gate * out, lng[...], lnb[...])


def _sample_in_kernel(x_ref, mod_ref, lh0_ref, lst_ref, sst_ref, w_main, w_dt, w_mg, lcw, lcb,
                      w_gate, ba, bx, lam, scw, scb, dtb, alog, wlp, e3,
                      lh_ref, lco_ref, sco_ref, plru_ref, xbc_ref, xdt_ref, da_ref, z_ref, gm_ref):
    x = x_ref[...]
    mod = mod_ref[...]
    shift = mod[:, :D_MODEL]
    scale = mod[:, D_MODEL:2 * D_MODEL]
    hb = (x * (1.0 + scale) + shift).astype(BF16)

    lru_x = _dot(hb, w_main[:, OFF_LRU_X:OFF_LRU_X + D_LRU])
    xbc_raw = _dot(hb, w_main[:, OFF_XBC:OFF_XBC + CONV_DIM])
    u = lcb[...] + lcw[CONV_W - 1:CONV_W, :] * lru_x
    xbc = scb[...] + scw[CONV_W - 1:CONV_W, :] * xbc_raw
    for k in range(CONV_W - 1):
        u = u + lcw[k:k + 1, :] * lst_ref[k]
        xbc = xbc + scw[k:k + 1, :] * sst_ref[k]
    for k in range(CONV_W - 2):
        lco_ref[k] = lst_ref[k + 1]
        sco_ref[k] = sst_ref[k + 1]
    lco_ref[CONV_W - 2] = lru_x
    sco_ref[CONV_W - 2] = xbc_raw
    xbc = _silu(xbc)
    xbc_ref[...] = xbc

    a, mult, ig = _lru_gates(u, w_gate, ba[...], bx[...], lam[...])
    h_new = a * lh0_ref[...] + mult * ig * u
    lh_ref[...] = h_new
    lru_z = _dot(hb, w_main[:, OFF_LRU_Z:OFF_LRU_Z + D_LRU])
    plru_ref[...] = _dot((h_new * _silu(lru_z)).astype(BF16), wlp[...])

    dt = _softplus(_dot(hb, w_dt[...]) + dtb[...])
    da_ref[...] = jnp.exp(dt * (-jnp.exp(alog[...])))
    xdt_ref[...] = xbc[:, :D_SSD] * _expand_heads(dt, e3[...])
    z_ref[...] = _dot(hb, w_main[:, OFF_SSD_Z:OFF_SSD_Z + D_SSD])
    gm_ref[...] = _sigmoid(_dot(hb, w_mg[...]))


def _sample_state_kernel(da_ref, h0_ref, xdt_ref, b_ref, c_ref, ho_ref, y_ref):
    i = pl.program_id(0)
    bb = h0_ref.shape[0]
    n_terms = 6
    pad_rows = LANES - n_terms * bb

    def stack(terms, width):
        return jnp.concatenate(list(terms) + [jnp.zeros((pad_rows, width), F32)], axis=0)

    xh, xm, xl = _split3(xdt_ref[...])
    bh, bm, bl = _split3(b_ref[...])
    x6t = stack((xh, xh, xm, xm, xh, xl), D_SSD).T.astype(BF16)
    b6 = stack((bh, bm, bh, bm, bl, bh), GROUPS * STATE)
    ch, cmid, cl = _split3(c_ref[...])
    c3 = jnp.concatenate([ch, cmid, cl], axis=0).astype(BF16)
    krow = lax.broadcasted_iota(jnp.int32, (LANES, 1), 0)
    yrow = lax.broadcasted_iota(jnp.int32, (3 * bb, 1), 0)
    for b in range(bb):
        own = jnp.logical_and(jnp.bitwise_and(krow, bb - 1) == b, krow < n_terms * bb)
        rhs = jnp.where(own, b6, 0.0).astype(BF16)
        y_own = jnp.bitwise_and(yrow, bb - 1) == b
        y_parts = []
        for g in range(GROUPS):
            st = _dot(x6t[g * GROUP_W:(g + 1) * GROUP_W, :], rhs[:, g * STATE:(g + 1) * STATE])
            hn_parts = []
            for hh in range(HEADS_PER_GROUP):
                h = g * HEADS_PER_GROUP + hh
                rows = slice(h * HEAD_DIM, (h + 1) * HEAD_DIM)
                hn = da_ref[i * bb + b, h] * h0_ref[b, rows, :] + st[hh * HEAD_DIM:(hh + 1) * HEAD_DIM, :]
                ho_ref[b, rows, :] = hn
                hn_parts.append(hn.astype(BF16))
            hn_g = jnp.concatenate(hn_parts, axis=0)
            r = _dot_nt(c3[:, g * STATE:(g + 1) * STATE], hn_g)
            y_parts.append(jnp.sum(jnp.where(y_own, r, 0.0), axis=0, keepdims=True))
        y_ref[b:b + 1, :] = jnp.concatenate(y_parts, axis=1)


def _sample_out_kernel(x_ref, mod_ref, yraw_ref, xbc_ref, z_ref, gm_ref, plru_ref,
                       dexp, normw, wsp, wo, lng, lnb, o_ref):
    x = x_ref[...]
    gate = mod_ref[:, 2 * D_MODEL:]
    y = yraw_ref[...] + dexp[...] * xbc_ref[:, :D_SSD]
    y_ssd = _gated_rmsnorm(y, z_ref[...], normw[...])
    p_ssd = _dot(y_ssd.astype(BF16), wsp[...])
    gm = gm_ref[...]
    merged = gm[:, :D_MODEL] * plru_ref[...] + gm[:, D_MODEL:] * p_ssd
    out = _dot(merged.astype(BF16), wo[...])
    o_ref[...] = _layer_norm(ALPHA * x + gate * out, lng[...], lnb[...])


def _resident(shape, grid_rank):
    zeros = (0,) * len(shape)
    if grid_rank == 1:
        imap = lambda i: zeros
    else:
        imap = lambda b, j: zeros
    return pl.BlockSpec(shape, imap, pipeline_mode=pl.Buffered(1))


def _whole(shape):
    return jax.ShapeDtypeStruct(shape, F32)


def kernel(x_prompt, x_sample, state_lru_h, state_lru_conv, state_ssd_h, state_ssd_conv, c_prompt, c_sample, w_cond, b_cond, w_in, lru_conv_w, lru_conv_b, lru_wa, lru_ba, lru_wx, lru_bx, lru_lambda, ssd_conv_w, ssd_conv_b, ssd_dt_bias, ssd_a_log, ssd_d, ssd_norm_w, w_lru_proj, w_ssd_proj, w_out, ln_g, ln_b):
    nbp, seq, _ = x_prompt.shape
    nbs = x_sample.shape[0]
    tc = PROMPT_TC
    assert seq % tc == 0 and tc % CHUNK == 0 and nbs % SAMPLE_BB == 0 and SAMPLE_BB == SUBLANES

    w_in0 = w_in[0]
    w_main = w_in0[:, :N_MAIN].astype(BF16)
    w_dt_cols = w_in0[:, N_MAIN:N_MAIN + HEADS]
    w_dt = jnp.pad(w_dt_cols, ((0, 0), (0, LANES - HEADS))).astype(BF16)
    w_dtT = w_dt_cols.T.astype(BF16)
    w_mg = w_in0[:, N_MAIN + HEADS:].astype(BF16)
    nq = D_LRU // LANES
    wa = lru_wa[0].reshape(nq, 2, LRU_BLOCK_W, LRU_BLOCK_W)
    wx = lru_wx[0].reshape(nq, 2, LRU_BLOCK_W, LRU_BLOCK_W)
    zb = jnp.zeros((nq, LRU_BLOCK_W, LRU_BLOCK_W), F32)

    def blockdiag(w):
        top = jnp.concatenate([w[:, 0], zb], axis=2)
        bot = jnp.concatenate([zb, w[:, 1]], axis=2)
        return jnp.concatenate([top, bot], axis=1)

    w_gate = jnp.concatenate([blockdiag(wa), blockdiag(wx)], axis=2).astype(BF16)
    row = lambda v: v.reshape(1, -1)
    ba, bx, lam = row(lru_ba[0]), row(lru_bx[0]), row(lru_lambda[0])
    lcw, lcb = lru_conv_w[0], row(lru_conv_b[0])
    scw, scb = ssd_conv_w[0], row(ssd_conv_b[0])
    dtb = jnp.pad(row(ssd_dt_bias[0]), ((0, 0), (0, LANES - HEADS)))
    alog = jnp.pad(row(ssd_a_log[0]), ((0, 0), (0, LANES - HEADS)))
    dtbT = ssd_dt_bias[0].reshape(HEADS, 1)
    alogT = ssd_a_log[0].reshape(HEADS, 1)
    dexp = row(jnp.repeat(ssd_d[0], HEAD_DIM))
    normw = row(ssd_norm_w[0])
    wlp = w_lru_proj[0].astype(BF16)
    wsp = w_ssd_proj[0].astype(BF16)
    wo = w_out[0].astype(BF16)
    lng, lnb = row(ln_g[0]), row(ln_b[0])
    k_idx = jnp.arange(LANES)[:, None]
    c_idx = jnp.arange(D_SSD)[None, :]
    e3 = jnp.logical_and(k_idx % HEADS == c_idx // HEAD_DIM, k_idx < 3 * HEADS).astype(BF16)

    c_all = jnp.concatenate([c_prompt, c_sample], axis=0)
    mod = pl.pallas_call(
        _cond_kernel,
        out_shape=_whole((nbp + nbs, 3 * D_MODEL)),
        name="cond",
    )(c_all, w_cond[0].astype(BF16), row(b_cond[0]))
    mod_p = mod[:nbp].reshape(nbp, 1, 3 * D_MODEL)
    mod_s = mod[nbp:]

    weights = (w_main, w_dt, w_dtT, w_mg, lcw, lcb, w_gate, ba, bx, lam, scw, scb, dtb, dtbT,
               alog, alogT, dexp, normw, wlp, wsp, wo, lng, lnb, e3)
    y_p, lh_p, lc_p, sh_p, sc_p = pl.pallas_call(
        _prompt_kernel,
        grid=(nbp, seq // tc),
        in_specs=[pl.BlockSpec((1, tc, D_MODEL), lambda b, j: (b, j, 0)),
                  pl.BlockSpec((1, 1, 3 * D_MODEL), lambda b, j: (b, 0, 0))]
                 + [_resident(w.shape, 2) for w in weights],
        out_specs=[pl.BlockSpec((1, tc, D_MODEL), lambda b, j: (b, j, 0)),
                   pl.BlockSpec((1, 1, D_LRU), lambda b, j: (b, 0, 0)),
                   pl.BlockSpec((1, 1, CONV_W - 1, D_LRU), lambda b, j: (0, b, 0, 0)),
                   pl.BlockSpec((1, D_SSD, STATE), lambda b, j: (b, 0, 0)),
                   pl.BlockSpec((1, 1, CONV_W - 1, CONV_DIM), lambda b, j: (0, b, 0, 0))],
        out_shape=[_whole((nbp, seq, D_MODEL)), _whole((nbp, 1, D_LRU)),
                   _whole((1, nbp, CONV_W - 1, D_LRU)), _whole((nbp, D_SSD, STATE)),
                   _whole((1, nbp, CONV_W - 1, CONV_DIM))],
        scratch_shapes=[pltpu.VMEM((tc + SUBLANES, D_LRU), F32),
                        pltpu.VMEM((tc + SUBLANES, CONV_DIM), F32),
                        pltpu.VMEM((1, D_LRU), F32),
                        pltpu.VMEM((STATE, D_SSD), F32),
                        pltpu.VMEM((tc, D_LRU), F32),
                        pltpu.VMEM((tc, D_LRU), F32),
                        pltpu.VMEM((tc, D_LRU), F32)],
        compiler_params=pltpu.CompilerParams(
            dimension_semantics=("arbitrary", "arbitrary"), vmem_limit_bytes=VMEM_LIMIT),
        name="prompt",
    )(x_prompt, mod_p, *weights)

    xs2 = x_sample.reshape(nbs, D_MODEL)
    lh_s, lco, sco, p_lru, xbc_s, xdt, da, z_s, gm_s = pl.pallas_call(
        _sample_in_kernel,
        out_shape=[_whole((nbs, D_LRU)), _whole((CONV_W - 1, nbs, D_LRU)),
                   _whole((CONV_W - 1, nbs, CONV_DIM)), _whole((nbs, D_MODEL)),
                   _whole((nbs, CONV_DIM)), _whole((nbs, D_SSD)), _whole((nbs, LANES)),
                   _whole((nbs, D_SSD)), _whole((nbs, 2 * D_MODEL))],
        compiler_params=pltpu.CompilerParams(vmem_limit_bytes=VMEM_LIMIT),
        name="sample_in",
    )(xs2, mod_s, state_lru_h[0], jnp.transpose(state_lru_conv[0], (1, 0, 2)),
      jnp.transpose(state_ssd_conv[0], (1, 0, 2)), w_main, w_dt, w_mg,
      lcw, lcb, w_gate, ba, bx, lam, scw, scb, dtb, alog, wlp, e3)

    bb = SAMPLE_BB
    n_bc = GROUPS * STATE
    sh_s, y_raw = pl.pallas_call(
        _sample_state_kernel,
        grid=(nbs // bb,),
        in_specs=[pl.BlockSpec(memory_space=pltpu.SMEM),
                  pl.BlockSpec((bb, D_SSD, STATE), lambda i: (i, 0, 0)),
                  pl.BlockSpec((bb, D_SSD), lambda i: (i, 0)),
                  pl.BlockSpec((bb, n_bc), lambda i: (i, D_SSD // n_bc)),
                  pl.BlockSpec((bb, n_bc), lambda i: (i, D_SSD // n_bc + 1))],
        out_specs=[pl.BlockSpec((bb, D_SSD, STATE), lambda i: (i, 0, 0)),
                   pl.BlockSpec((bb, D_SSD), lambda i: (i, 0))],
        out_shape=[_whole((nbs, D_SSD, STATE)), _whole((nbs, D_SSD))],
        compiler_params=pltpu.CompilerParams(
            dimension_semantics=("arbitrary",), vmem_limit_bytes=VMEM_LIMIT),
        name="sample_state",
    )(da[:, :HEADS], state_ssd_h[0].reshape(nbs, D_SSD, STATE), xdt, xbc_s, xbc_s)

    y_s = pl.pallas_call(
        _sample_out_kernel,
        out_shape=_whole((nbs, D_MODEL)),
        compiler_params=pltpu.CompilerParams(vmem_limit_bytes=VMEM_LIMIT),
        name="sample_out",
    )(xs2, mod_s, y_raw, xbc_s, z_s, gm_s, p_lru, dexp, normw, wsp, wo, lng, lnb)

    return (y_p,
            y_s.reshape(nbs, 1, D_MODEL),
            lh_p.reshape(1, nbp, D_LRU),
            lc_p,
            sh_p.reshape(1, nbp, HEADS, HEAD_DIM, STATE),
            sc_p,
            lh_s.reshape(1, nbs, D_LRU),
            jnp.transpose(lco, (1, 0, 2)).reshape(1, nbs, CONV_W - 1, D_LRU),
            sh_s.reshape(1, nbs, HEADS, HEAD_DIM, STATE),
            jnp.transpose(sco, (1, 0, 2)).reshape(1, nbs, CONV_W - 1, CONV_DIM))
```

```python
import functools

import jax
import jax.numpy as jnp
from jax import lax
from jax.experimental import pallas as pl
from jax.experimental.pallas import tpu as pltpu

F32 = jnp.float32
BF16 = jnp.bfloat16

D_MODEL = 1024
D_LRU = 1024
LRU_BLOCK_W = 64
LRU_C = 8.0
D_SSD = 2048
HEAD_DIM = 64
HEADS = 32
GROUPS = 4
HEADS_PER_GROUP = HEADS // GROUPS
STATE = 128
GROUP_W = D_SSD // GROUPS
CHUNK = 128
CONV_DIM = D_SSD + 2 * GROUPS * STATE
CONV_W = 4
LN_EPS = 1e-5
RMS_EPS = 1e-5
ALPHA = 2.0 ** 0.25
LOG2_E = 1.4426950408889634

LANES = 128
SUBLANES = 8
N_MAIN = 2 * D_LRU + D_SSD + CONV_DIM
OFF_LRU_X, OFF_LRU_Z, OFF_SSD_Z, OFF_XBC = 0, D_LRU, 2 * D_LRU, 2 * D_LRU + D_SSD

PROMPT_TC = 256
SAMPLE_BB = 8
STAGE_ROWS, STAGE_COLS = 1024, 256
VMEM_LIMIT = 60 * 1024 * 1024


def _sigmoid(x):
    return 0.5 + 0.5 * jnp.tanh(0.5 * x)


def _silu(x):
    h = 0.5 * x
    return h + h * jnp.tanh(h)


def _softplus(x):
    return jnp.maximum(x, 0.0) + jnp.log1p(jnp.exp(-jnp.abs(x)))


def _dot(a, b):
    return jnp.dot(a, b, preferred_element_type=F32)


def _dot_nt(a, b):
    return lax.dot_general(a, b, (((1,), (1,)), ((), ())), preferred_element_type=F32)


def _dot_tn(a, b):
    return lax.dot_general(a, b, (((0,), (0,)), ((), ())), preferred_element_type=F32)


def _slab_copies(src_view, dst_ref, n_rows, n_cols):
    pairs = []
    for r0 in range(0, n_rows, STAGE_ROWS):
        for c0 in range(0, n_cols, STAGE_COLS):
            pairs.append((src_view(r0, c0),
                          dst_ref.at[pl.ds(r0, STAGE_ROWS), pl.ds(c0, STAGE_COLS)]))
    return pairs


def _stage_bf16(pairs, stage, sem):
    def dma(i):
        return pltpu.make_async_copy(pairs[i][0], stage.at[i % 2], sem.at[i % 2])

    dma(0).start()
    for i in range(len(pairs)):
        if i + 1 < len(pairs):
            dma(i + 1).start()
        dma(i).wait()
        w = stage[i % 2].astype(BF16)
        sink = pairs[i][1]
        if callable(sink):
            sink(w)
        else:
            sink[...] = w


def _split3(x):
    hi = x.astype(BF16).astype(F32)
    r = x - hi
    mid = r.astype(BF16).astype(F32)
    lo = r - mid
    return hi, mid, lo


def _dot_exact_rhs(lhs_bf16, x):
    hi, mid, lo = _split3(x)
    return (_dot(lhs_bf16, hi.astype(BF16)) + _dot(lhs_bf16, mid.astype(BF16))
            + _dot(lhs_bf16, lo.astype(BF16)))


def _dot_exact_lhs(x, rhs_bf16):
    hi, mid, lo = _split3(x)
    return (_dot(hi.astype(BF16), rhs_bf16) + _dot(mid.astype(BF16), rhs_bf16)
            + _dot(lo.astype(BF16), rhs_bf16))


def _expand_heads(v, e3):
    lane = lax.broadcasted_iota(jnp.int32, v.shape, 1)
    v = jnp.where(lane < HEADS, v, 0.0)
    hi, mid, lo = _split3(v)
    packed = hi + pltpu.roll(mid, HEADS, axis=1) + pltpu.roll(lo, 2 * HEADS, axis=1)
    return _dot(packed.astype(BF16), e3)


def _lru_gates(u, w_gate_ref, ba, bx, lam):
    ub = u.astype(BF16)
    r_parts, i_parts = [], []
    for q in range(D_LRU // LANES):
        gq = _dot(ub[:, q * LANES:(q + 1) * LANES], w_gate_ref[q])
        r_parts.append(gq[:, :LANES])
        i_parts.append(gq[:, LANES:])
    r = _sigmoid(jnp.concatenate(r_parts, axis=1) + ba)
    i = _sigmoid(jnp.concatenate(i_parts, axis=1) + bx)
    log_a = (-LRU_C) * r * _softplus(-lam)
    a = jnp.exp(log_a)
    v = -jnp.tanh(log_a) * (a * a + 1.0)
    mult = jnp.where(v > 0.0, v * lax.rsqrt(v), 0.0)
    return a, mult, i


def _gated_rmsnorm(y, z, norm_w):
    g = y * _silu(z)
    parts = []
    for gi in range(GROUPS):
        gg = g[:, gi * GROUP_W:(gi + 1) * GROUP_W]
        ms = jnp.mean(gg * gg, axis=-1, keepdims=True)
        parts.append(gg * lax.rsqrt(ms + RMS_EPS))
    return jnp.concatenate(parts, axis=1) * norm_w


def _layer_norm(v, g, b):
    mu = jnp.mean(v, axis=-1, keepdims=True)
    d = v - mu
    var = jnp.mean(d * d, axis=-1, keepdims=True)
    return d * lax.rsqrt(var + LN_EPS) * g + b


def _cond_kernel(c_ref, w_ref, b_ref, o_ref):
    o_ref[...] = _dot(c_ref[...].astype(BF16), w_ref[...].astype(BF16)) + b_ref[...]


def _main_weight_copies(w_in_hbm, w_mg_hbm, wlp_hbm, w_main, w_mg, wlp):
    return (_slab_copies(lambda r, c: w_in_hbm.at[0, pl.ds(r, STAGE_ROWS), pl.ds(c, STAGE_COLS)],
                         w_main, D_MODEL, N_MAIN)
            + _slab_copies(lambda r, c: w_mg_hbm.at[pl.ds(r, STAGE_ROWS), pl.ds(c, STAGE_COLS)],
                           w_mg, D_MODEL, 2 * D_MODEL)
            + _slab_copies(lambda r, c: wlp_hbm.at[0, pl.ds(r, STAGE_ROWS), pl.ds(c, STAGE_COLS)],
                           wlp, D_LRU, D_MODEL))


def _out_weight_copies(wsp_hbm, wo_hbm, wsp, wo):
    return (_slab_copies(lambda r, c: wsp_hbm.at[0, pl.ds(r, STAGE_ROWS), pl.ds(c, STAGE_COLS)],
                         wsp, D_SSD, D_MODEL)
            + _slab_copies(lambda r, c: wo_hbm.at[0, pl.ds(r, STAGE_ROWS), pl.ds(c, STAGE_COLS)],
                           wo, D_MODEL, D_MODEL))


def _prompt_kernel(x_ref, mod_ref, w_in_hbm, w_mg_hbm, wlp_hbm, wsp_hbm, wo_hbm,
                   w_dt, w_dtT, lcw, lcb, w_gate, ba, bx, lam,
                   scw, scb, dtb, dtbT, alog, alogT, dexp, normw, lng, lnb, e3,
                   y_ref, lh_ref, lc_ref, sh_ref, sc_ref,
                   w_main, w_mg, wlp, wsp, wo, stage, sem,
                   lbuf, sbuf, hl, hT, a_s, b_s, hs_s):
    tc = x_ref.shape[1]
    j = pl.program_id(1)
    last = pl.num_programs(1) - 1

    @pl.when(jnp.logical_and(pl.program_id(0) == 0, j == 0))
    def _():
        _stage_bf16(_main_weight_copies(w_in_hbm, w_mg_hbm, wlp_hbm, w_main, w_mg, wlp)
                    + _out_weight_copies(wsp_hbm, wo_hbm, wsp, wo), stage, sem)

    @pl.when(j == 0)
    def _():
        lbuf[0:SUBLANES, :] = jnp.zeros((SUBLANES, D_LRU), F32)
        sbuf[0:SUBLANES, :] = jnp.zeros((SUBLANES, CONV_DIM), F32)
        hl[...] = jnp.zeros_like(hl)
        hT[...] = jnp.zeros_like(hT)

    x = x_ref[0]
    mod = mod_ref[0]
    shift = mod[:, :D_MODEL]
    scale = mod[:, D_MODEL:2 * D_MODEL]
    gate = mod[:, 2 * D_MODEL:]
    hb = (x * (1.0 + scale) + shift).astype(BF16)

    lbuf[SUBLANES:SUBLANES + tc, :] = _dot(hb, w_main[:, OFF_LRU_X:OFF_LRU_X + D_LRU])
    sbuf[SUBLANES:SUBLANES + tc, :] = _dot(hb, w_main[:, OFF_XBC:OFF_XBC + CONV_DIM])
    lru_z = _dot(hb, w_main[:, OFF_LRU_Z:OFF_LRU_Z + D_LRU])
    ssd_z = _dot(hb, w_main[:, OFF_SSD_Z:OFF_SSD_Z + D_SSD])
    merge_logits = _dot(hb, w_mg[...])
    base = SUBLANES - (CONV_W - 1)
    u = lcb[...] + lcw[0:1, :] * lbuf[base:base + tc, :]
    xbc = scb[...] + scw[0:1, :] * sbuf[base:base + tc, :]
    for k in range(1, CONV_W):
        u = u + lcw[k:k + 1, :] * lbuf[base + k:base + k + tc, :]
        xbc = xbc + scw[k:k + 1, :] * sbuf[base + k:base + k + tc, :]
    xbc = _silu(xbc)

    lbuf[0:SUBLANES, :] = lbuf[tc:tc + SUBLANES, :]
    sbuf[0:SUBLANES, :] = sbuf[tc:tc + SUBLANES, :]

    a, mult, ig = _lru_gates(u, w_gate, ba[...], bx[...], lam[...])
    iu = ig * u
    row = lax.broadcasted_iota(jnp.int32, (tc, 1), 0)
    first = jnp.logical_and(row == 0, j == 0)
    a_s[...] = a
    b_s[...] = jnp.where(first, iu, mult * iu)

    def scan_body(t, h):
        h = a_s[pl.ds(t, 1), :] * h + b_s[pl.ds(t, 1), :]
        hs_s[pl.ds(t, 1), :] = h
        return h

    hl[...] = lax.fori_loop(0, tc, scan_body, hl[...], unroll=True)

    y_lru = hs_s[...] * _silu(lru_z)
    p_lru = _dot(y_lru.astype(BF16), wlp[...])

    xs = xbc[:, :D_SSD]
    bm = xbc[:, D_SSD:D_SSD + GROUPS * STATE]
    cm = xbc[:, D_SSD + GROUPS * STATE:]
    dt = _softplus(_dot(hb, w_dt[...]) + dtb[...])
    dtT = _softplus(_dot_nt(w_dtT[...], hb) + dtbT[...])
    a_row = -LOG2_E * jnp.exp(alog[...])
    a_col = -LOG2_E * jnp.exp(alogT[...])

    ri = lax.broadcasted_iota(jnp.int32, (CHUNK, CHUNK), 0)
    ci = lax.broadcasted_iota(jnp.int32, (CHUNK, CHUNK), 1)
    causal = ri >= ci
    tril = causal.astype(BF16)
    triu = (ri <= ci).astype(BF16)
    lane = lax.broadcasted_iota(jnp.int32, (CHUNK, LANES), 1)
    lo_half = lane < HEAD_DIM

    y_chunks = []
    for c in range(tc // CHUNK):
        r0 = c * CHUNK
        xs_c = xs[r0:r0 + CHUNK]
        xs_cb = xs_c.astype(BF16)
        bm_c = bm[r0:r0 + CHUNK].astype(BF16)
        cm_c = cm[r0:r0 + CHUNK].astype(BF16)
        dt_c = dt[r0:r0 + CHUNK]
        dtT_c = dtT[:, r0:r0 + CHUNK]
        acs = _dot_exact_rhs(tril, dt_c * a_row)
        acsT = _dot_exact_lhs(dtT_c * a_col, triu)
        e_exp = _expand_heads(jnp.exp2(acs), e3[...])
        w_exp = _expand_heads(jnp.exp2(acs[CHUNK - 1:CHUNK, :] - acs) * dt_c, e3[...])

        y_parts = []
        for g in range(GROUPS):
            cg = cm_c[:, g * STATE:(g + 1) * STATE]
            bg = bm_c[:, g * STATE:(g + 1) * STATE]
            cb = _dot_nt(cg, bg)
            for q in range(HEADS_PER_GROUP // 2):
                pair = g * (HEADS_PER_GROUP // 2) + q
                xp = xs_cb[:, pair * LANES:(pair + 1) * LANES]
                zero = jnp.zeros_like(xp)
                ms = []
                for half in range(2):
                    h = 2 * pair + half
                    seg = acs[:, h:h + 1] - acsT[h:h + 1, :]
                    m = jnp.exp2(jnp.where(causal, seg, -jnp.inf)) * cb * dtT_c[h:h + 1, :]
                    ms.append(m.astype(BF16))
                x2 = jnp.concatenate([jnp.where(lo_half, xp, zero), jnp.where(lo_half, zero, xp)], axis=0)
                y_parts.append(_dot(jnp.concatenate(ms, axis=1), x2))
        y_diag = jnp.concatenate(y_parts, axis=1)

        h_prev = hT[...]
        h_prev_b = h_prev.astype(BF16)
        xsw = (xs_c * w_exp).astype(BF16)
        off_parts, st_parts = [], []
        for g in range(GROUPS):
            sl = slice(g * GROUP_W, (g + 1) * GROUP_W)
            off_parts.append(_dot(cm_c[:, g * STATE:(g + 1) * STATE], h_prev_b[:, sl]))
            st_parts.append(_dot_tn(bm_c[:, g * STATE:(g + 1) * STATE], xsw[:, sl]))
        y_off = jnp.concatenate(off_parts, axis=1) * e_exp
        hT[...] = e_exp[CHUNK - 1:CHUNK, :] * h_prev + jnp.concatenate(st_parts, axis=1)
        y_chunks.append(y_diag + y_off + dexp[...] * xs_c)
    y = jnp.concatenate(y_chunks, axis=0) if len(y_chunks) > 1 else y_chunks[0]

    y_ssd = _gated_rmsnorm(y, ssd_z, normw[...])
    p_ssd = _dot(y_ssd.astype(BF16), wsp[...])

    gm = _sigmoid(merge_logits)
    merged = gm[:, :D_MODEL] * p_lru + gm[:, D_MODEL:] * p_ssd
    out = _dot(merged.astype(BF16), wo[...])
    y_ref[0] = _layer_norm(ALPHA * x + gate * out, lng[...], lnb[...])

    @pl.when(j == last)
    def _():
        lc_ref[0, 0] = lbuf[SUBLANES - (CONV_W - 1):SUBLANES, :]
        sc_ref[0, 0] = sbuf[SUBLANES - (CONV_W - 1):SUBLANES, :]
        lh_ref[0] = hl[...]
        hfin = hT[...]
        for q in range(D_SSD // LANES):
            sh_ref[0, q * LANES:(q + 1) * LANES, :] = hfin[:, q * LANES:(q + 1) * LANES].T


def _sample_in_kernel(x_ref, mod_ref, lh0_ref, lst_ref, sst_ref, w_in_hbm, w_mg_hbm, wlp_hbm,
                      w_dt, lcw, lcb, w_gate, ba, bx, lam, scw, scb, dtb, alog, e3,
                      lh_ref, lco_ref, sco_ref, plru_ref, xbc_ref, xdt_ref, da_ref, z_ref, gm_ref,
                      proj, wlp, stage, sem):
    x = x_ref[...]
    mod = mod_ref[...]
    shift = mod[:, :D_MODEL]
    scale = mod[:, D_MODEL:2 * D_MODEL]
    hb = (x * (1.0 + scale) + shift).astype(BF16)

    def project_into(c0):
        def sink(w):
            proj[:, c0:c0 + STAGE_COLS] = _dot(hb, w)
        return sink

    pairs = [(w_in_hbm.at[0, pl.ds(0, STAGE_ROWS), pl.ds(c0, STAGE_COLS)], project_into(c0))
             for c0 in range(0, N_MAIN, STAGE_COLS)]
    pairs += [(w_mg_hbm.at[pl.ds(0, STAGE_ROWS), pl.ds(c0, STAGE_COLS)], project_into(N_MAIN + c0))
              for c0 in range(0, 2 * D_MODEL, STAGE_COLS)]
    pairs += _slab_copies(lambda r, c: wlp_hbm.at[0, pl.ds(r, STAGE_ROWS), pl.ds(c, STAGE_COLS)],
                          wlp, D_LRU, D_MODEL)
    _stage_bf16(pairs, stage, sem)

    lru_x = proj[:, OFF_LRU_X:OFF_LRU_X + D_LRU]
    xbc_raw = proj[:, OFF_XBC:OFF_XBC + CONV_DIM]
    u = lcb[...] + lcw[CONV_W - 1:CONV_W, :] * lru_x
    xbc = scb[...] + scw[CONV_W - 1:CONV_W, :] * xbc_raw
    for k in range(CONV_W - 1):
        u = u + lcw[k:k + 1, :] * lst_ref[k]
        xbc = xbc + scw[k:k + 1, :] * sst_ref[k]
    for k in range(CONV_W - 2):
        lco_ref[k] = lst_ref[k + 1]
        sco_ref[k] = sst_ref[k + 1]
    lco_ref[CONV_W - 2] = lru_x
    sco_ref[CONV_W - 2] = xbc_raw
    xbc = _silu(xbc)
    xbc_ref[...] = xbc

    a, mult, ig = _lru_gates(u, w_gate, ba[...], bx[...], lam[...])
    h_new = a * lh0_ref[...] + mult * ig * u
    lh_ref[...] = h_new
    lru_z = proj[:, OFF_LRU_Z:OFF_LRU_Z + D_LRU]
    plru_ref[...] = _dot((h_new * _silu(lru_z)).astype(BF16), wlp[...])

    dt = _softplus(_dot(hb, w_dt[...]) + dtb[...])
    da_ref[...] = jnp.exp(dt * (-jnp.exp(alog[...])))
    xdt_ref[...] = xbc[:, :D_SSD] * _expand_heads(dt, e3[...])
    z_ref[...] = proj[:, OFF_SSD_Z:OFF_SSD_Z + D_SSD]
    gm_ref[...] = _sigmoid(proj[:, N_MAIN:N_MAIN + 2 * D_MODEL])


def _sample_state_kernel(da_ref, h0_ref, xdt_ref, b_ref, c_ref, ho_ref, y_ref):
    i = pl.program_id(0)
    bb = h0_ref.shape[0]
    n_terms = 6
    pad_rows = LANES - n_terms * bb

    def stack(terms, width):
        return jnp.concatenate(list(terms) + [jnp.zeros((pad_rows, width), F32)], axis=0)

    xh, xm, xl = _split3(xdt_ref[...])
    bh, bm, bl = _split3(b_ref[...])
    x6t = stack((xh, xh, xm, xm, xh, xl), D_SSD).T.astype(BF16)
    b6 = stack((bh, bm, bh, bm, bl, bh), GROUPS * STATE)
    ch, cmid, cl = _split3(c_ref[...])
    c3 = jnp.concatenate([ch, cmid, cl], axis=0).astype(BF16)
    krow = lax.broadcasted_iota(jnp.int32, (LANES, 1), 0)
    yrow = lax.broadcasted_iota(jnp.int32, (3 * bb, 1), 0)
    for b in range(bb):
        own = jnp.logical_and(jnp.bitwise_and(krow, bb - 1) == b, krow < n_terms * bb)
        rhs = jnp.where(own, b6, 0.0).astype(BF16)
        y_own = jnp.bitwise_and(yrow, bb - 1) == b
        y_parts = []
        for g in range(GROUPS):
            st = _dot(x6t[g * GROUP_W:(g + 1) * GROUP_W, :], rhs[:, g * STATE:(g + 1) * STATE])
            hn_parts = []
            for hh in range(HEADS_PER_GROUP):
                h = g * HEADS_PER_GROUP + hh
                rows = slice(h * HEAD_DIM, (h + 1) * HEAD_DIM)
                hn = da_ref[i * bb + b, h] * h0_ref[b, rows, :] + st[hh * HEAD_DIM:(hh + 1) * HEAD_DIM, :]
                ho_ref[b, rows, :] = hn
                hn_parts.append(hn.astype(BF16))
            hn_g = jnp.concatenate(hn_parts, axis=0)
            r = _dot_nt(c3[:, g * STATE:(g + 1) * STATE], hn_g)
            y_parts.append(jnp.sum(jnp.where(y_own, r, 0.0), axis=0, keepdims=True))
        y_ref[b:b + 1, :] = jnp.concatenate(y_parts, axis=1)


def _sample_out_kernel(x_ref, mod_ref, yraw_ref, xbc_ref, z_ref, gm_ref, plru_ref,
                       dexp, normw, wsp_hbm, wo_hbm, lng, lnb, o_ref, wsp, wo, stage, sem):
    _stage_bf16(_out_weight_copies(wsp_hbm, wo_hbm, wsp, wo), stage, sem)
    x = x_ref[...]
    gate = mod_ref[:, 2 * D_MODEL:]
    y = yraw_ref[...] + dexp[...] * xbc_ref[:, :D_SSD]
    y_ssd = _gated_rmsnorm(y, z_ref[...], normw[...])
    p_ssd = _dot(y_ssd.astype(BF16), wsp[...])
    gm = gm_ref[...]
    merged = gm[:, :D_MODEL] * plru_ref[...] + gm[:, D_MODEL:] * p_ssd
    out = _dot(merged.astype(BF16), wo[...])
    o_ref[...] = _layer_norm(ALPHA * x + gate * out, lng[...], lnb[...])


def _resident(shape, grid_rank):
    zeros = (0,) * len(shape)
    if grid_rank == 1:
        imap = lambda i: zeros
    else:
        imap = lambda b, j: zeros
    return pl.BlockSpec(shape, imap, pipeline_mode=pl.Buffered(1))


def _whole(shape):
    return jax.ShapeDtypeStruct(shape, F32)


def kernel(x_prompt, x_sample, state_lru_h, state_lru_conv, state_ssd_h, state_ssd_conv, c_prompt, c_sample, w_cond, b_cond, w_in, lru_conv_w, lru_conv_b, lru_wa, lru_ba, lru_wx, lru_bx, lru_lambda, ssd_conv_w, ssd_conv_b, ssd_dt_bias, ssd_a_log, ssd_d, ssd_norm_w, w_lru_proj, w_ssd_proj, w_out, ln_g, ln_b):
    nbp, seq, _ = x_prompt.shape
    nbs = x_sample.shape[0]
    tc = PROMPT_TC
    assert seq % tc == 0 and tc % CHUNK == 0 and nbs % SAMPLE_BB == 0 and SAMPLE_BB == SUBLANES

    w_in0 = w_in[0]
    w_dt_cols = w_in0[:, N_MAIN:N_MAIN + HEADS]
    w_dt = jnp.pad(w_dt_cols, ((0, 0), (0, LANES - HEADS))).astype(BF16)
    w_dtT = w_dt_cols.T.astype(BF16)
    w_mg32 = w_in0[:, N_MAIN + HEADS:]
    nq = D_LRU // LANES
    wa = lru_wa[0].reshape(nq, 2, LRU_BLOCK_W, LRU_BLOCK_W)
    wx = lru_wx[0].reshape(nq, 2, LRU_BLOCK_W, LRU_BLOCK_W)
    zb = jnp.zeros((nq, LRU_BLOCK_W, LRU_BLOCK_W), F32)

    def blockdiag(w):
        top = jnp.concatenate([w[:, 0], zb], axis=2)
        bot = jnp.concatenate([zb, w[:, 1]], axis=2)
        return jnp.concatenate([top, bot], axis=1)

    w_gate = jnp.concatenate([blockdiag(wa), blockdiag(wx)], axis=2).astype(BF16)
    row = lambda v: v.reshape(1, -1)
    ba, bx, lam = row(lru_ba[0]), row(lru_bx[0]), row(lru_lambda[0])
    lcw, lcb = lru_conv_w[0], row(lru_conv_b[0])
    scw, scb = ssd_conv_w[0], row(ssd_conv_b[0])
    dtb = jnp.pad(row(ssd_dt_bias[0]), ((0, 0), (0, LANES - HEADS)))
    alog = jnp.pad(row(ssd_a_log[0]), ((0, 0), (0, LANES - HEADS)))
    dtbT = ssd_dt_bias[0].reshape(HEADS, 1)
    alogT = ssd_a_log[0].reshape(HEADS, 1)
    dexp = row(jnp.repeat(ssd_d[0], HEAD_DIM))
    normw = row(ssd_norm_w[0])
    lng, lnb = row(ln_g[0]), row(ln_b[0])
    k_idx = jnp.arange(LANES)[:, None]
    c_idx = jnp.arange(D_SSD)[None, :]
    e3 = jnp.logical_and(k_idx % HEADS == c_idx // HEAD_DIM, k_idx < 3 * HEADS).astype(BF16)

    c_all = jnp.concatenate([c_prompt, c_sample], axis=0)
    mod = pl.pallas_call(
        _cond_kernel,
        out_shape=_whole((nbp + nbs, 3 * D_MODEL)),
        name="cond",
    )(c_all, w_cond[0], row(b_cond[0]))
    mod_p = mod[:nbp].reshape(nbp, 1, 3 * D_MODEL)
    mod_s = mod[nbp:]

    hbm = pl.BlockSpec(memory_space=pl.ANY)
    stage_scratch = [pltpu.VMEM((2, STAGE_ROWS, STAGE_COLS), F32), pltpu.SemaphoreType.DMA((2,))]
    main_scratch = [pltpu.VMEM((D_MODEL, N_MAIN), BF16), pltpu.VMEM((D_MODEL, 2 * D_MODEL), BF16),
                    pltpu.VMEM((D_LRU, D_MODEL), BF16)]
    out_scratch = [pltpu.VMEM((D_SSD, D_MODEL), BF16), pltpu.VMEM((D_MODEL, D_MODEL), BF16)]
    big = (w_in, w_mg32, w_lru_proj, w_ssd_proj, w_out)
    weights = (w_dt, w_dtT, lcw, lcb, w_gate, ba, bx, lam, scw, scb, dtb, dtbT,
               alog, alogT, dexp, normw, lng, lnb, e3)
    y_p, lh_p, lc_p, sh_p, sc_p = pl.pallas_call(
        _prompt_kernel,
        grid=(nbp, seq // tc),
        in_specs=[pl.BlockSpec((1, tc, D_MODEL), lambda b, j: (b, j, 0)),
                  pl.BlockSpec((1, 1, 3 * D_MODEL), lambda b, j: (b, 0, 0))]
                 + [hbm] * len(big) + [_resident(w.shape, 2) for w in weights],
        out_specs=[pl.BlockSpec((1, tc, D_MODEL), lambda b, j: (b, j, 0)),
                   pl.BlockSpec((1, 1, D_LRU), lambda b, j: (b, 0, 0)),
                   pl.BlockSpec((1, 1, CONV_W - 1, D_LRU), lambda b, j: (0, b, 0, 0)),
                   pl.BlockSpec((1, D_SSD, STATE), lambda b, j: (b, 0, 0)),
                   pl.BlockSpec((1, 1, CONV_W - 1, CONV_DIM), lambda b, j: (0, b, 0, 0))],
        out_shape=[_whole((nbp, seq, D_MODEL)), _whole((nbp, 1, D_LRU)),
                   _whole((1, nbp, CONV_W - 1, D_LRU)), _whole((nbp, D_SSD, STATE)),
                   _whole((1, nbp, CONV_W - 1, CONV_DIM))],
        scratch_shapes=main_scratch + out_scratch + stage_scratch + [
                        pltpu.VMEM((tc + SUBLANES, D_LRU), F32),
                        pltpu.VMEM((tc + SUBLANES, CONV_DIM), F32),
                        pltpu.VMEM((1, D_LRU), F32),
                        pltpu.VMEM((STATE, D_SSD), F32),
                        pltpu.VMEM((tc, D_LRU), F32),
                        pltpu.VMEM((tc, D_LRU), F32),
                        pltpu.VMEM((tc, D_LRU), F32)],
        compiler_params=pltpu.CompilerParams(
            dimension_semantics=("arbitrary", "arbitrary"), vmem_limit_bytes=VMEM_LIMIT),
        name="prompt",
    )(x_prompt, mod_p, *big, *weights)

    xs2 = x_sample.reshape(nbs, D_MODEL)
    vmem = pl.BlockSpec(memory_space=pltpu.VMEM)
    lh_s, lco, sco, p_lru, xbc_s, xdt, da, z_s, gm_s = pl.pallas_call(
        _sample_in_kernel,
        in_specs=[vmem] * 5 + [hbm] * 3 + [vmem] * 12,
        scratch_shapes=[pltpu.VMEM((nbs, N_MAIN + 2 * D_MODEL), F32),
                        pltpu.VMEM((D_LRU, D_MODEL), BF16)] + stage_scratch,
        out_shape=[_whole((nbs, D_LRU)), _whole((CONV_W - 1, nbs, D_LRU)),
                   _whole((CONV_W - 1, nbs, CONV_DIM)), _whole((nbs, D_MODEL)),
                   _whole((nbs, CONV_DIM)), _whole((nbs, D_SSD)), _whole((nbs, LANES)),
                   _whole((nbs, D_SSD)), _whole((nbs, 2 * D_MODEL))],
        compiler_params=pltpu.CompilerParams(vmem_limit_bytes=VMEM_LIMIT),
        name="sample_in",
    )(xs2, mod_s, state_lru_h[0], jnp.transpose(state_lru_conv[0], (1, 0, 2)),
      jnp.transpose(state_ssd_conv[0], (1, 0, 2)), w_in, w_mg32, w_lru_proj,
      w_dt, lcw, lcb, w_gate, ba, bx, lam, scw, scb, dtb, alog, e3)

    bb = SAMPLE_BB
    n_bc = GROUPS * STATE
    sh_s, y_raw = pl.pallas_call(
        _sample_state_kernel,
        grid=(nbs // bb,),
        in_specs=[pl.BlockSpec(memory_space=pltpu.SMEM),
                  pl.BlockSpec((bb, D_SSD, STATE), lambda i: (i, 0, 0)),
                  pl.BlockSpec((bb, D_SSD), lambda i: (i, 0)),
                  pl.BlockSpec((bb, n_bc), lambda i: (i, D_SSD // n_bc)),
                  pl.BlockSpec((bb, n_bc), lambda i: (i, D_SSD // n_bc + 1))],
        out_specs=[pl.BlockSpec((bb, D_SSD, STATE), lambda i: (i, 0, 0)),
                   pl.BlockSpec((bb, D_SSD), lambda i: (i, 0))],
        out_shape=[_whole((nbs, D_SSD, STATE)), _whole((nbs, D_SSD))],
        compiler_params=pltpu.CompilerParams(
            dimension_semantics=("arbitrary",), vmem_limit_bytes=VMEM_LIMIT),
        name="sample_state",
    )(da[:, :HEADS], state_ssd_h[0].reshape(nbs, D_SSD, STATE), xdt, xbc_s, xbc_s)

    y_s = pl.pallas_call(
        _sample_out_kernel,
        in_specs=[vmem] * 9 + [hbm] * 2 + [vmem] * 2,
        scratch_shapes=out_scratch + stage_scratch,
        out_shape=_whole((nbs, D_MODEL)),
        compiler_params=pltpu.CompilerParams(vmem_limit_bytes=VMEM_LIMIT),
        name="sample_out",
    )(xs2, mod_s, y_raw, xbc_s, z_s, gm_s, p_lru, dexp, normw, w_ssd_proj, w_out, lng, lnb)

    return (y_p,
            y_s.reshape(nbs, 1, D_MODEL),
            lh_p.reshape(1, nbp, D_LRU),
            lc_p,
            sh_p.reshape(1, nbp, HEADS, HEAD_DIM, STATE),
            sc_p,
            lh_s.reshape(1, nbs, D_LRU),
            jnp.transpose(lco, (1, 0, 2)).reshape(1, nbs, CONV_W - 1, D_LRU),
            sh_s.reshape(1, nbs, HEADS, HEAD_DIM, STATE),
            jnp.transpose(sco, (1, 0, 2)).reshape(1, nbs, CONV_W - 1, CONV_DIM))
```

```python
import jax
import jax.numpy as jnp
from jax import lax
from jax.experimental import pallas as pl
from jax.experimental.pallas import tpu as pltpu

F32 = jnp.float32
BF16 = jnp.bfloat16

D_MODEL = 1024
D_LRU = 1024
LRU_BLOCK_W = 64
LRU_C = 8.0
D_SSD = 2048
HEAD_DIM = 64
HEADS = 32
GROUPS = 4
HEADS_PER_GROUP = HEADS // GROUPS
STATE = 128
GROUP_W = D_SSD // GROUPS
CHUNK = 128
CONV_DIM = D_SSD + 2 * GROUPS * STATE
CONV_W = 4
LN_EPS = 1e-5
RMS_EPS = 1e-5
ALPHA = 2.0 ** 0.25
LOG2_E = 1.4426950408889634

LANES = 128
SUBLANES = 8
SEG = CHUNK // SUBLANES
HALO = (CONV_W - 1) * SUBLANES
N_MAIN = 2 * D_LRU + D_SSD + CONV_DIM
OFF_LRU_X, OFF_LRU_Z, OFF_SSD_Z, OFF_XBC = 0, D_LRU, 2 * D_LRU, 2 * D_LRU + D_SSD

PROMPT_TC = 256
SAMPLE_BB = 8
VMEM_LIMIT = 60 * 1024 * 1024


def _sigmoid(x):
    return 0.5 + 0.5 * jnp.tanh(0.5 * x)


def _silu(x):
    h = 0.5 * x
    return h + h * jnp.tanh(h)


def _softplus(x):
    return jnp.maximum(x, 0.0) + jnp.log1p(jnp.exp(-jnp.abs(x)))


def _dot(a, b):
    return jnp.dot(a, b, preferred_element_type=F32)


def _dot_nt(a, b):
    return lax.dot_general(a, b, (((1,), (1,)), ((), ())), preferred_element_type=F32)


def _dot_tn(a, b):
    return lax.dot_general(a, b, (((0,), (0,)), ((), ())), preferred_element_type=F32)


def _w(ref, c0=None, c1=None):
    return ref[...] if c0 is None else ref[:, c0:c1]


def _split3(x):
    hi = x.astype(BF16).astype(F32)
    r = x - hi
    mid = r.astype(BF16).astype(F32)
    lo = r - mid
    return hi, mid, lo


def _dot_exact_rhs(lhs_bf16, x):
    hi, mid, lo = _split3(x)
    return (_dot(lhs_bf16, hi.astype(BF16)) + _dot(lhs_bf16, mid.astype(BF16))
            + _dot(lhs_bf16, lo.astype(BF16)))


def _dot_exact_lhs(x, rhs_bf16):
    hi, mid, lo = _split3(x)
    return (_dot(hi.astype(BF16), rhs_bf16) + _dot(mid.astype(BF16), rhs_bf16)
            + _dot(lo.astype(BF16), rhs_bf16))


def _expand_heads(v, e3):
    lane = lax.broadcasted_iota(jnp.int32, v.shape, 1)
    v = jnp.where(lane < HEADS, v, 0.0)
    hi, mid, lo = _split3(v)
    packed = hi + pltpu.roll(mid, HEADS, axis=1) + pltpu.roll(lo, 2 * HEADS, axis=1)
    return _dot(packed.astype(BF16), e3)


def _lru_gates(u, w_gate_ref, ba, bx, lam):
    ub = u.astype(BF16)
    r_parts, i_parts = [], []
    for q in range(D_LRU // LANES):
        gq = _dot(ub[:, q * LANES:(q + 1) * LANES], w_gate_ref[q])
        r_parts.append(gq[:, :LANES])
        i_parts.append(gq[:, LANES:])
    r = _sigmoid(jnp.concatenate(r_parts, axis=1) + ba)
    i = _sigmoid(jnp.concatenate(i_parts, axis=1) + bx)
    log_a = (-LRU_C) * r * _softplus(-lam)
    a = jnp.exp(log_a)
    v = -jnp.tanh(log_a) * (a * a + 1.0)
    mult = jnp.where(v > 0.0, v * lax.rsqrt(v), 0.0)
    return a, mult, i


def _gated_rmsnorm(y, z, norm_w):
    g = y * _silu(z)
    parts = []
    for gi in range(GROUPS):
        gg = g[:, gi * GROUP_W:(gi + 1) * GROUP_W]
        ms = jnp.mean(gg * gg, axis=-1, keepdims=True)
        parts.append(gg * lax.rsqrt(ms + RMS_EPS))
    return jnp.concatenate(parts, axis=1) * norm_w


def _layer_norm(v, g, b):
    mu = jnp.mean(v, axis=-1, keepdims=True)
    d = v - mu
    var = jnp.mean(d * d, axis=-1, keepdims=True)
    return d * lax.rsqrt(var + LN_EPS) * g + b


def _cond_kernel(c_ref, w_ref, b_ref, o_ref):
    o_ref[...] = _dot(c_ref[...].astype(BF16), w_ref[...]) + b_ref[...]


def _time_permute(v, n_chunks, to_permuted):
    sub = lax.broadcasted_iota(jnp.int32, (SUBLANES, 1), 0)
    n_half = SEG // SUBLANES
    out = [None] * (n_chunks * SEG)
    for c in range(n_chunks):
        for h in range(n_half):
            natural = [c * SEG + n_half * s + h for s in range(SUBLANES)]
            permuted = [c * SEG + SUBLANES * h + i for i in range(SUBLANES)]
            src, dst = (natural, permuted) if to_permuted else (permuted, natural)
            rows = [v[g * SUBLANES:(g + 1) * SUBLANES] for g in src]
            d = SUBLANES // 2
            while d >= 1:
                keep = jnp.bitwise_and(sub, d) == 0
                for a in range(SUBLANES):
                    if a & d == 0:
                        lo, hi = rows[a], rows[a + d]
                        rows[a] = jnp.where(keep, lo, pltpu.roll(hi, d, axis=0))
                        rows[a + d] = jnp.where(keep, pltpu.roll(lo, SUBLANES - d, axis=0), hi)
                d //= 2
            for i, g in enumerate(dst):
                out[g] = rows[i]
    return jnp.concatenate(out, axis=0)


def _permuted_conv(raw, buf, tail, w_ref, b_ref, n_chunks):
    sub = lax.broadcasted_iota(jnp.int32, (SUBLANES, 1), 0)
    outs = []
    for c in range(n_chunks):
        buf[c, HALO:HALO + CHUNK, :] = raw[c * CHUNK:(c + 1) * CHUNK]
        for i in range(CONV_W - 1):
            r0 = HALO + CHUNK - HALO + i * SUBLANES
            cur = buf[c, r0:r0 + SUBLANES, :]
            if c == 0:
                prev = tail[i * SUBLANES:(i + 1) * SUBLANES, :]
            else:
                prev = buf[c - 1, r0:r0 + SUBLANES, :]
            buf[c, i * SUBLANES:(i + 1) * SUBLANES, :] = pltpu.roll(
                jnp.where(sub == SUBLANES - 1, prev, cur), 1, axis=0)
        acc = b_ref[...] + w_ref[0:1, :] * buf[c, 0:CHUNK, :]
        for k in range(1, CONV_W):
            acc = acc + w_ref[k:k + 1, :] * buf[c, k * SUBLANES:k * SUBLANES + CHUNK, :]
        outs.append(acc)
    tail[...] = buf[n_chunks - 1, CHUNK:CHUNK + HALO, :]
    return jnp.concatenate(outs, axis=0) if n_chunks > 1 else outs[0]


def _permuted_scan(a, b, h0, n_chunks):
    hs = []
    h_in = h0
    for c in range(n_chunks):
        ps, hz = [], []
        for q in range(SEG):
            r0 = c * CHUNK + q * SUBLANES
            aq, bq = a[r0:r0 + SUBLANES], b[r0:r0 + SUBLANES]
            if q == 0:
                p, h = aq, bq
            else:
                p, h = aq * p, aq * h + bq
            ps.append(p)
            hz.append(h)
        carries = [h_in]
        for s in range(SUBLANES):
            carries.append(h[s:s + 1] + p[s:s + 1] * carries[-1])
        h_in = carries[SUBLANES]
        carry = jnp.concatenate(carries[:SUBLANES], axis=0)
        hs.extend(hz[q] + ps[q] * carry for q in range(SEG))
    return jnp.concatenate(hs, axis=0), h_in


def _prompt_kernel(x_ref, mod_ref, w_main, w_dt, w_dtT, w_mg, lcw, lcb, w_gate, ba, bx, lam,
                   scw, scb, dtb, dtbT, alog, alogT, dexp, normw, wlp, wsp, wo, lng, lnb, e3,
                   y_ref, lh_ref, lc_ref, sh_ref, sc_ref,
                   lbuf, sbuf, ltail, stail, hl, hT):
    tc = x_ref.shape[1]
    n_chunks = tc // CHUNK
    j = pl.program_id(1)
    last = pl.num_programs(1) - 1

    @pl.when(j == 0)
    def _():
        ltail[...] = jnp.zeros_like(ltail)
        stail[...] = jnp.zeros_like(stail)
        hl[...] = jnp.zeros_like(hl)
        hT[...] = jnp.zeros_like(hT)

    x = x_ref[0]
    mod = mod_ref[0]
    shift = mod[:, :D_MODEL]
    scale = mod[:, D_MODEL:2 * D_MODEL]
    gate = mod[:, 2 * D_MODEL:]
    hb = _time_permute(x * (1.0 + scale) + shift, n_chunks, True).astype(BF16)

    lru_x = _dot(hb, _w(w_main, OFF_LRU_X, OFF_LRU_X + D_LRU))
    u = _permuted_conv(lru_x, lbuf, ltail, lcw, lcb, n_chunks)
    xbc_raw = _dot(hb, _w(w_main, OFF_XBC, OFF_XBC + CONV_DIM))

    a, mult, ig = _lru_gates(u, w_gate, ba[...], bx[...], lam[...])
    iu = ig * u
    row = lax.broadcasted_iota(jnp.int32, (tc, 1), 0)
    first = jnp.logical_and(row == 0, j == 0)
    lru_z = _dot(hb, _w(w_main, OFF_LRU_Z, OFF_LRU_Z + D_LRU))
    hs, h_end = _permuted_scan(a, jnp.where(first, iu, mult * iu), hl[...], n_chunks)
    hl[...] = h_end
    y_lru = hs * _silu(lru_z)
    xbc = _silu(_permuted_conv(xbc_raw, sbuf, stail, scw, scb, n_chunks))
    p_lru = _dot(y_lru.astype(BF16), _w(wlp))

    xs = xbc[:, :D_SSD]
    bm = xbc[:, D_SSD:D_SSD + GROUPS * STATE]
    cm = xbc[:, D_SSD + GROUPS * STATE:]
    dt = _softplus(_dot(hb, w_dt[...]) + dtb[...])
    dtT = _softplus(_dot_nt(w_dtT[...], hb) + dtbT[...])
    a_row = -LOG2_E * jnp.exp(alog[...])
    a_col = -LOG2_E * jnp.exp(alogT[...])

    ri = lax.broadcasted_iota(jnp.int32, (CHUNK, CHUNK), 0)
    ci = lax.broadcasted_iota(jnp.int32, (CHUNK, CHUNK), 1)
    t_row = jnp.bitwise_and(ri, SUBLANES - 1) * SEG + jnp.right_shift(ri, 3)
    t_col = jnp.bitwise_and(ci, SUBLANES - 1) * SEG + jnp.right_shift(ci, 3)
    causal = t_row >= t_col
    tril = causal.astype(BF16)
    triu = (t_row <= t_col).astype(BF16)
    lane = lax.broadcasted_iota(jnp.int32, (CHUNK, LANES), 1)
    lo_half = lane < HEAD_DIM

    y_chunks = []
    late_proj = []
    for c in range(n_chunks):
        r0 = c * CHUNK
        if c == 0:
            late_proj.append(_dot(hb, _w(w_main, OFF_SSD_Z, OFF_SSD_Z + D_SSD)))
        if c == n_chunks - 1:
            late_proj.append(_dot(hb, _w(w_mg)))
        xs_c = xs[r0:r0 + CHUNK]
        xs_cb = xs_c.astype(BF16)
        bm_c = bm[r0:r0 + CHUNK].astype(BF16)
        cm_c = cm[r0:r0 + CHUNK].astype(BF16)
        dt_c = dt[r0:r0 + CHUNK]
        dtT_c = dtT[:, r0:r0 + CHUNK]
        acs = _dot_exact_rhs(tril, dt_c * a_row)
        acsT = _dot_exact_lhs(dtT_c * a_col, triu)
        e_exp = _expand_heads(jnp.exp2(acs), e3[...])
        w_exp = _expand_heads(jnp.exp2(acs[CHUNK - 1:CHUNK, :] - acs) * dt_c, e3[...])

        y_parts = []
        for g in range(GROUPS):
            cg = cm_c[:, g * STATE:(g + 1) * STATE]
            bg = bm_c[:, g * STATE:(g + 1) * STATE]
            cb = _dot_nt(cg, bg)
            for q in range(HEADS_PER_GROUP // 2):
                pair = g * (HEADS_PER_GROUP // 2) + q
                xp = xs_cb[:, pair * LANES:(pair + 1) * LANES]
                zero = jnp.zeros_like(xp)
                ms = []
                for half in range(2):
                    h = 2 * pair + half
                    seg = acs[:, h:h + 1] - acsT[h:h + 1, :]
                    m = jnp.exp2(jnp.where(causal, seg, -jnp.inf)) * cb * dtT_c[h:h + 1, :]
                    ms.append(m.astype(BF16))
                x2 = jnp.concatenate([jnp.where(lo_half, xp, zero), jnp.where(lo_half, zero, xp)], axis=0)
                y_parts.append(_dot(jnp.concatenate(ms, axis=1), x2))
        y_diag = jnp.concatenate(y_parts, axis=1)

        h_prev = hT[...]
        h_prev_b = h_prev.astype(BF16)
        xsw = (xs_c * w_exp).astype(BF16)
        off_parts, st_parts = [], []
        for g in range(GROUPS):
            sl = slice(g * GROUP_W, (g + 1) * GROUP_W)
            off_parts.append(_dot(cm_c[:, g * STATE:(g + 1) * STATE], h_prev_b[:, sl]))
            st_parts.append(_dot_tn(bm_c[:, g * STATE:(g + 1) * STATE], xsw[:, sl]))
        y_off = jnp.concatenate(off_parts, axis=1) * e_exp
        hT[...] = e_exp[CHUNK - 1:CHUNK, :] * h_prev + jnp.concatenate(st_parts, axis=1)
        y_chunks.append(y_diag + y_off + dexp[...] * xs_c)
    y = jnp.concatenate(y_chunks, axis=0) if len(y_chunks) > 1 else y_chunks[0]
    ssd_z, merge_logits = late_proj

    y_ssd = _gated_rmsnorm(y, ssd_z, normw[...])
    p_ssd = _dot(y_ssd.astype(BF16), _w(wsp))

    gm = _sigmoid(merge_logits)
    merged = gm[:, :D_MODEL] * p_lru + gm[:, D_MODEL:] * p_ssd
    out = _dot(merged.astype(BF16), _w(wo))
    y_ref[0] = _layer_norm(ALPHA * x + gate * _time_permute(out, n_chunks, False), lng[...], lnb[...])

    @pl.when(j == last)
    def _():
        for i in range(CONV_W - 1):
            r = (i + 1) * SUBLANES - 1
            lc_ref[0, 0, i:i + 1, :] = ltail[r:r + 1, :]
            sc_ref[0, 0, i:i + 1, :] = stail[r:r + 1, :]
        lh_ref[0] = hl[...]
        hfin = hT[...]
        for q in range(D_SSD // LANES):
            sh_ref[0, q * LANES:(q + 1) * LANES, :] = hfin[:, q * LANES:(q + 1) * LANES].T


def _sample_in_kernel(x_ref, mod_ref, lh0_ref, lst_ref, sst_ref, w_main, w_dt, w_mg, lcw, lcb,
                      w_gate, ba, bx, lam, scw, scb, dtb, alog, wlp, e3,
                      lh_ref, lco_ref, sco_ref, plru_ref, xbc_ref, xdt_ref, da_ref, z_ref, gm_ref):
    x = x_ref[...]
    mod = mod_ref[...]
    shift = mod[:, :D_MODEL]
    scale = mod[:, D_MODEL:2 * D_MODEL]
    hb = (x * (1.0 + scale) + shift).astype(BF16)

    lru_x = _dot(hb, _w(w_main, OFF_LRU_X, OFF_LRU_X + D_LRU))
    xbc_raw = _dot(hb, _w(w_main, OFF_XBC, OFF_XBC + CONV_DIM))
    u = lcb[...] + lcw[CONV_W - 1:CONV_W, :] * lru_x
    xbc = scb[...] + scw[CONV_W - 1:CONV_W, :] * xbc_raw
    for k in range(CONV_W - 1):
        u = u + lcw[k:k + 1, :] * lst_ref[k]
        xbc = xbc + scw[k:k + 1, :] * sst_ref[k]
    for k in range(CONV_W - 2):
        lco_ref[k] = lst_ref[k + 1]
        sco_ref[k] = sst_ref[k + 1]
    lco_ref[CONV_W - 2] = lru_x
    sco_ref[CONV_W - 2] = xbc_raw
    xbc = _silu(xbc)
    xbc_ref[...] = xbc

    a, mult, ig = _lru_gates(u, w_gate, ba[...], bx[...], lam[...])
    h_new = a * lh0_ref[...] + mult * ig * u
    lh_ref[...] = h_new
    lru_z = _dot(hb, _w(w_main, OFF_LRU_Z, OFF_LRU_Z + D_LRU))
    plru_ref[...] = _dot((h_new * _silu(lru_z)).astype(BF16), _w(wlp))

    dt = _softplus(_dot(hb, w_dt[...]) + dtb[...])
    da_ref[...] = jnp.exp(dt * (-jnp.exp(alog[...])))
    xdt_ref[...] = xbc[:, :D_SSD] * _expand_heads(dt, e3[...])
    z_ref[...] = _dot(hb, _w(w_main, OFF_SSD_Z, OFF_SSD_Z + D_SSD))
    gm_ref[...] = _sigmoid(_dot(hb, _w(w_mg)))


def _sample_state_kernel(da_ref, h0_ref, xdt_ref, b_ref, c_ref, ho_ref, y_ref):
    i = pl.program_id(0)
    bb = h0_ref.shape[0]
    n_terms = 6
    pad_rows = LANES - n_terms * bb

    def stack(terms, width):
        return jnp.concatenate(list(terms) + [jnp.zeros((pad_rows, width), F32)], axis=0)

    xh, xm, xl = _split3(xdt_ref[...])
    bh, bm, bl = _split3(b_ref[...])
    x6t = stack((xh, xh, xm, xm, xh, xl), D_SSD).T.astype(BF16)
    b6 = stack((bh, bm, bh, bm, bl, bh), GROUPS * STATE)
    ch, cmid, cl = _split3(c_ref[...])
    c3 = jnp.concatenate([ch, cmid, cl], axis=0).astype(BF16)
    krow = lax.broadcasted_iota(jnp.int32, (LANES, 1), 0)
    yrow = lax.broadcasted_iota(jnp.int32, (3 * bb, 1), 0)
    for b in range(bb):
        own = jnp.logical_and(jnp.bitwise_and(krow, bb - 1) == b, krow < n_terms * bb)
        rhs = jnp.where(own, b6, 0.0).astype(BF16)
        y_own = jnp.bitwise_and(yrow, bb - 1) == b
        y_parts = []
        for g in range(GROUPS):
            st = _dot(x6t[g * GROUP_W:(g + 1) * GROUP_W, :], rhs[:, g * STATE:(g + 1) * STATE])
            hn_parts = []
            for hh in range(HEADS_PER_GROUP):
                h = g * HEADS_PER_GROUP + hh
                rows = slice(h * HEAD_DIM, (h + 1) * HEAD_DIM)
                hn = da_ref[i * bb + b, h] * h0_ref[b, rows, :] + st[hh * HEAD_DIM:(hh + 1) * HEAD_DIM, :]
                ho_ref[b, rows, :] = hn
                hn_parts.append(hn.astype(BF16))
            hn_g = jnp.concatenate(hn_parts, axis=0)
            r = _dot_nt(c3[:, g * STATE:(g + 1) * STATE], hn_g)
            y_parts.append(jnp.sum(jnp.where(y_own, r, 0.0), axis=0, keepdims=True))
        y_ref[b:b + 1, :] = jnp.concatenate(y_parts, axis=1)


def _sample_out_kernel(x_ref, mod_ref, yraw_ref, xbc_ref, z_ref, gm_ref, plru_ref,
                       dexp, normw, wsp, wo, lng, lnb, o_ref):
    x = x_ref[...]
    gate = mod_ref[:, 2 * D_MODEL:]
    y = yraw_ref[...] + dexp[...] * xbc_ref[:, :D_SSD]
    y_ssd = _gated_rmsnorm(y, z_ref[...], normw[...])
    p_ssd = _dot(y_ssd.astype(BF16), _w(wsp))
    gm = gm_ref[...]
    merged = gm[:, :D_MODEL] * plru_ref[...] + gm[:, D_MODEL:] * p_ssd
    out = _dot(merged.astype(BF16), _w(wo))
    o_ref[...] = _layer_norm(ALPHA * x + gate * out, lng[...], lnb[...])


def _resident(shape, grid_rank):
    zeros = (0,) * len(shape)
    if grid_rank == 1:
        imap = lambda i: zeros
    else:
        imap = lambda b, j: zeros
    return pl.BlockSpec(shape, imap, pipeline_mode=pl.Buffered(1))


def _whole(shape):
    return jax.ShapeDtypeStruct(shape, F32)


def kernel(x_prompt, x_sample, state_lru_h, state_lru_conv, state_ssd_h, state_ssd_conv, c_prompt, c_sample, w_cond, b_cond, w_in, lru_conv_w, lru_conv_b, lru_wa, lru_ba, lru_wx, lru_bx, lru_lambda, ssd_conv_w, ssd_conv_b, ssd_dt_bias, ssd_a_log, ssd_d, ssd_norm_w, w_lru_proj, w_ssd_proj, w_out, ln_g, ln_b):
    nbp, seq, _ = x_prompt.shape
    nbs = x_sample.shape[0]
    tc = PROMPT_TC
    assert seq % tc == 0 and tc % CHUNK == 0 and nbs % SAMPLE_BB == 0 and SAMPLE_BB == SUBLANES

    w_in0 = w_in[0]
    w_main = w_in0[:, :N_MAIN].astype(BF16)
    w_dt_cols = w_in0[:, N_MAIN:N_MAIN + HEADS]
    w_dt = jnp.pad(w_dt_cols, ((0, 0), (0, LANES - HEADS))).astype(BF16)
    w_dtT = w_dt_cols.T.astype(BF16)
    w_mg = w_in0[:, N_MAIN + HEADS:].astype(BF16)
    nq = D_LRU // LANES
    wa = lru_wa[0].reshape(nq, 2, LRU_BLOCK_W, LRU_BLOCK_W)
    wx = lru_wx[0].reshape(nq, 2, LRU_BLOCK_W, LRU_BLOCK_W)
    zb = jnp.zeros((nq, LRU_BLOCK_W, LRU_BLOCK_W), F32)

    def blockdiag(w):
        top = jnp.concatenate([w[:, 0], zb], axis=2)
        bot = jnp.concatenate([zb, w[:, 1]], axis=2)
        return jnp.concatenate([top, bot], axis=1)

    w_gate = jnp.concatenate([blockdiag(wa), blockdiag(wx)], axis=2).astype(BF16)
    row = lambda v: v.reshape(1, -1)
    ba, bx, lam = row(lru_ba[0]), row(lru_bx[0]), row(lru_lambda[0])
    lcw, lcb = lru_conv_w[0], row(lru_conv_b[0])
    scw, scb = ssd_conv_w[0], row(ssd_conv_b[0])
    dtb = jnp.pad(row(ssd_dt_bias[0]), ((0, 0), (0, LANES - HEADS)))
    alog = jnp.pad(row(ssd_a_log[0]), ((0, 0), (0, LANES - HEADS)))
    dtbT = ssd_dt_bias[0].reshape(HEADS, 1)
    alogT = ssd_a_log[0].reshape(HEADS, 1)
    dexp = row(jnp.repeat(ssd_d[0], HEAD_DIM))
    normw = row(ssd_norm_w[0])
    wlp = w_lru_proj[0].astype(BF16)
    wsp = w_ssd_proj[0].astype(BF16)
    wo = w_out[0].astype(BF16)
    lng, lnb = row(ln_g[0]), row(ln_b[0])
    k_idx = jnp.arange(LANES)[:, None]
    c_idx = jnp.arange(D_SSD)[None, :]
    e3 = jnp.logical_and(k_idx % HEADS == c_idx // HEAD_DIM, k_idx < 3 * HEADS).astype(BF16)

    c_all = jnp.concatenate([c_prompt, c_sample], axis=0)
    mod = pl.pallas_call(
        _cond_kernel,
        out_shape=_whole((nbp + nbs, 3 * D_MODEL)),
        name="cond",
    )(c_all, w_cond[0].astype(BF16), row(b_cond[0]))
    mod_p = mod[:nbp].reshape(nbp, 1, 3 * D_MODEL)
    mod_s = mod[nbp:]

    weights = (w_main, w_dt, w_dtT, w_mg, lcw, lcb, w_gate, ba, bx, lam, scw, scb, dtb, dtbT,
               alog, alogT, dexp, normw, wlp, wsp, wo, lng, lnb, e3)
    n_chunks = tc // CHUNK
    y_p, lh_p, lc_p, sh_p, sc_p = pl.pallas_call(
        _prompt_kernel,
        grid=(nbp, seq // tc),
        in_specs=[pl.BlockSpec((1, tc, D_MODEL), lambda b, j: (b, j, 0)),
                  pl.BlockSpec((1, 1, 3 * D_MODEL), lambda b, j: (b, 0, 0))]
                 + [_resident(w.shape, 2) for w in weights],
        out_specs=[pl.BlockSpec((1, tc, D_MODEL), lambda b, j: (b, j, 0)),
                   pl.BlockSpec((1, 1, D_LRU), lambda b, j: (b, 0, 0)),
                   pl.BlockSpec((1, 1, CONV_W - 1, D_LRU), lambda b, j: (0, b, 0, 0)),
                   pl.BlockSpec((1, D_SSD, STATE), lambda b, j: (b, 0, 0)),
                   pl.BlockSpec((1, 1, CONV_W - 1, CONV_DIM), lambda b, j: (0, b, 0, 0))],
        out_shape=[_whole((nbp, seq, D_MODEL)), _whole((nbp, 1, D_LRU)),
                   _whole((1, nbp, CONV_W - 1, D_LRU)), _whole((nbp, D_SSD, STATE)),
                   _whole((1, nbp, CONV_W - 1, CONV_DIM))],
        scratch_shapes=[pltpu.VMEM((n_chunks, HALO + CHUNK, D_LRU), F32),
                        pltpu.VMEM((n_chunks, HALO + CHUNK, CONV_DIM), F32),
                        pltpu.VMEM((HALO, D_LRU), F32),
                        pltpu.VMEM((HALO, CONV_DIM), F32),
                        pltpu.VMEM((1, D_LRU), F32),
                        pltpu.VMEM((STATE, D_SSD), F32)],
        compiler_params=pltpu.CompilerParams(
            dimension_semantics=("arbitrary", "arbitrary"), vmem_limit_bytes=VMEM_LIMIT),
        name="prompt",
    )(x_prompt, mod_p, *weights)

    xs2 = x_sample.reshape(nbs, D_MODEL)
    lh_s, lco, sco, p_lru, xbc_s, xdt, da, z_s, gm_s = pl.pallas_call(
        _sample_in_kernel,
        out_shape=[_whole((nbs, D_LRU)), _whole((CONV_W - 1, nbs, D_LRU)),
                   _whole((CONV_W - 1, nbs, CONV_DIM)), _whole((nbs, D_MODEL)),
                   _whole((nbs, CONV_DIM)), _whole((nbs, D_SSD)), _whole((nbs, LANES)),
                   _whole((nbs, D_SSD)), _whole((nbs, 2 * D_MODEL))],
        compiler_params=pltpu.CompilerParams(vmem_limit_bytes=VMEM_LIMIT),
        name="sample_in",
    )(xs2, mod_s, state_lru_h[0], jnp.transpose(state_lru_conv[0], (1, 0, 2)),
      jnp.transpose(state_ssd_conv[0], (1, 0, 2)), w_main, w_dt, w_mg,
      lcw, lcb, w_gate, ba, bx, lam, scw, scb, dtb, alog, wlp, e3)

    bb = SAMPLE_BB
    n_bc = GROUPS * STATE
    sh_s, y_raw = pl.pallas_call(
        _sample_state_kernel,
        grid=(nbs // bb,),
        in_specs=[pl.BlockSpec(memory_space=pltpu.SMEM),
                  pl.BlockSpec((bb, D_SSD, STATE), lambda i: (i, 0, 0)),
                  pl.BlockSpec((bb, D_SSD), lambda i: (i, 0)),
                  pl.BlockSpec((bb, n_bc), lambda i: (i, D_SSD // n_bc)),
                  pl.BlockSpec((bb, n_bc), lambda i: (i, D_SSD // n_bc + 1))],
        out_specs=[pl.BlockSpec((bb, D_SSD, STATE), lambda i: (i, 0, 0)),
                   pl.BlockSpec((bb, D_SSD), lambda i: (i, 0))],
        out_shape=[_whole((nbs, D_SSD, STATE)), _whole((nbs, D_SSD))],
        compiler_params=pltpu.CompilerParams(
            dimension_semantics=("arbitrary",), vmem_limit_bytes=VMEM_LIMIT),
        name="sample_state",
    )(da[:, :HEADS], state_ssd_h[0].reshape(nbs, D_SSD, STATE), xdt, xbc_s, xbc_s)

    y_s = pl.pallas_call(
        _sample_out_kernel,
        out_shape=_whole((nbs, D_MODEL)),
        compiler_params=pltpu.CompilerParams(vmem_limit_bytes=VMEM_LIMIT),
        name="sample_out",
    )(xs2, mod_s, y_raw, xbc_s, z_s, gm_s, p_lru, dexp, normw, wsp, wo, lng, lnb)

    return (y_p,
            y_s.reshape(nbs, 1, D_MODEL),
            lh_p.reshape(1, nbp, D_LRU),
            lc_p,
            sh_p.reshape(1, nbp, HEADS, HEAD_DIM, STATE),
            sc_p,
            lh_s.reshape(1, nbs, D_LRU),
            jnp.transpose(lco, (1, 0, 2)).reshape(1, nbs, CONV_W - 1, D_LRU),
            sh_s.reshape(1, nbs, HEADS, HEAD_DIM, STATE),
            jnp.transpose(sco, (1, 0, 2)).reshape(1, nbs, CONV_W - 1, CONV_DIM))
```

```python
import functools

import jax
import jax.numpy as jnp
from jax import lax
from jax.experimental import pallas as pl
from jax.experimental.pallas import tpu as pltpu

F32 = jnp.float32
BF16 = jnp.bfloat16

D_MODEL = 1024
D_LRU = 1024
LRU_BLOCK_W = 64
LRU_C = 8.0
D_SSD = 2048
HEAD_DIM = 64
HEADS = 32
GROUPS = 4
HEADS_PER_GROUP = HEADS // GROUPS
STATE = 128
GROUP_W = D_SSD // GROUPS
CHUNK = 128
CONV_DIM = D_SSD + 2 * GROUPS * STATE
CONV_W = 4
LN_EPS = 1e-5
RMS_EPS = 1e-5
ALPHA = 2.0 ** 0.25
LOG2_E = 1.4426950408889634

LANES = 128
SUBLANES = 8
SEG = CHUNK // SUBLANES
HALO = (CONV_W - 1) * SUBLANES
N_MAIN = 2 * D_LRU + D_SSD + CONV_DIM
OFF_LRU_X, OFF_LRU_Z, OFF_SSD_Z, OFF_XBC = 0, D_LRU, 2 * D_LRU, 2 * D_LRU + D_SSD

PROMPT_TC = 256
SAMPLE_BB = 8
VMEM_LIMIT = 60 * 1024 * 1024


def _sigmoid(x):
    return 0.5 + 0.5 * jnp.tanh(0.5 * x)


def _silu(x):
    h = 0.5 * x
    return h + h * jnp.tanh(h)


def _softplus(x):
    return jnp.maximum(x, 0.0) + jnp.log1p(jnp.exp(-jnp.abs(x)))


def _dot(a, b):
    return jnp.dot(a, b, preferred_element_type=F32)


def _dot_nt(a, b):
    return lax.dot_general(a, b, (((1,), (1,)), ((), ())), preferred_element_type=F32)


def _dot_tn(a, b):
    return lax.dot_general(a, b, (((0,), (0,)), ((), ())), preferred_element_type=F32)


def _w(ref, c0=None, c1=None):
    return ref[...] if c0 is None else ref[:, c0:c1]


def _split3(x):
    hi = x.astype(BF16).astype(F32)
    r = x - hi
    mid = r.astype(BF16).astype(F32)
    lo = r - mid
    return hi, mid, lo


def _dot_exact_rhs(lhs_bf16, x):
    hi, mid, lo = _split3(x)
    return (_dot(lhs_bf16, hi.astype(BF16)) + _dot(lhs_bf16, mid.astype(BF16))
            + _dot(lhs_bf16, lo.astype(BF16)))


def _dot_exact_lhs(x, rhs_bf16):
    hi, mid, lo = _split3(x)
    return (_dot(hi.astype(BF16), rhs_bf16) + _dot(mid.astype(BF16), rhs_bf16)
            + _dot(lo.astype(BF16), rhs_bf16))


def _expand_heads(v, e3):
    lane = lax.broadcasted_iota(jnp.int32, v.shape, 1)
    v = jnp.where(lane < HEADS, v, 0.0)
    hi, mid, lo = _split3(v)
    packed = hi + pltpu.roll(mid, HEADS, axis=1) + pltpu.roll(lo, 2 * HEADS, axis=1)
    return _dot(packed.astype(BF16), e3)


def _lru_gates(u, w_gate_ref, ba, bx, lam):
    ub = u.astype(BF16)
    r_parts, i_parts = [], []
    for q in range(D_LRU // LANES):
        gq = _dot(ub[:, q * LANES:(q + 1) * LANES], w_gate_ref[q])
        r_parts.append(gq[:, :LANES])
        i_parts.append(gq[:, LANES:])
    r = _sigmoid(jnp.concatenate(r_parts, axis=1) + ba)
    i = _sigmoid(jnp.concatenate(i_parts, axis=1) + bx)
    log_a = (-LRU_C) * r * _softplus(-lam)
    a = jnp.exp(log_a)
    v = -jnp.tanh(log_a) * (a * a + 1.0)
    mult = jnp.where(v > 0.0, v * lax.rsqrt(v), 0.0)
    return a, mult, i


def _gated_rmsnorm(y, z, norm_w):
    g = y * _silu(z)
    parts = []
    for gi in range(GROUPS):
        gg = g[:, gi * GROUP_W:(gi + 1) * GROUP_W]
        ms = jnp.mean(gg * gg, axis=-1, keepdims=True)
        parts.append(gg * lax.rsqrt(ms + RMS_EPS))
    return jnp.concatenate(parts, axis=1) * norm_w


def _layer_norm(v, g, b):
    mu = jnp.mean(v, axis=-1, keepdims=True)
    d = v - mu
    var = jnp.mean(d * d, axis=-1, keepdims=True)
    return d * lax.rsqrt(var + LN_EPS) * g + b


def _cond_kernel(c_ref, w_ref, b_ref, o_ref):
    o_ref[...] = _dot(c_ref[...].astype(BF16), w_ref[...]) + b_ref[...]


def _time_permute(v, n_chunks, to_permuted):
    sub = lax.broadcasted_iota(jnp.int32, (SUBLANES, 1), 0)
    n_half = SEG // SUBLANES
    out = [None] * (n_chunks * SEG)
    for c in range(n_chunks):
        for h in range(n_half):
            natural = [c * SEG + n_half * s + h for s in range(SUBLANES)]
            permuted = [c * SEG + SUBLANES * h + i for i in range(SUBLANES)]
            src, dst = (natural, permuted) if to_permuted else (permuted, natural)
            rows = [v[g * SUBLANES:(g + 1) * SUBLANES] for g in src]
            d = SUBLANES // 2
            while d >= 1:
                keep = jnp.bitwise_and(sub, d) == 0
                for a in range(SUBLANES):
                    if a & d == 0:
                        lo, hi = rows[a], rows[a + d]
                        rows[a] = jnp.where(keep, lo, pltpu.roll(hi, d, axis=0))
                        rows[a + d] = jnp.where(keep, pltpu.roll(lo, SUBLANES - d, axis=0), hi)
                d //= 2
            for i, g in enumerate(dst):
                out[g] = rows[i]
    return jnp.concatenate(out, axis=0)


def _fill_conv_buf(buf, raw, n_chunks, col0):
    for c in range(n_chunks):
        buf[c, HALO:HALO + CHUNK, col0:col0 + raw.shape[1]] = raw[c * CHUNK:(c + 1) * CHUNK]


def _permuted_conv(buf, tail, w_ref, b_ref, n_chunks):
    sub = lax.broadcasted_iota(jnp.int32, (SUBLANES, 1), 0)
    outs = []
    for c in range(n_chunks):
        for i in range(CONV_W - 1):
            r0 = CHUNK + i * SUBLANES
            cur = buf[c, r0:r0 + SUBLANES, :]
            if c == 0:
                prev = tail[i * SUBLANES:(i + 1) * SUBLANES, :]
            else:
                prev = buf[c - 1, r0:r0 + SUBLANES, :]
            buf[c, i * SUBLANES:(i + 1) * SUBLANES, :] = pltpu.roll(
                jnp.where(sub == SUBLANES - 1, prev, cur), 1, axis=0)
        acc = b_ref[...] + w_ref[0:1, :] * buf[c, 0:CHUNK, :]
        for k in range(1, CONV_W):
            acc = acc + w_ref[k:k + 1, :] * buf[c, k * SUBLANES:k * SUBLANES + CHUNK, :]
        outs.append(acc)
    tail[...] = buf[n_chunks - 1, CHUNK:CHUNK + HALO, :]
    return jnp.concatenate(outs, axis=0) if n_chunks > 1 else outs[0]


def _permuted_scan(a, b, h0, n_chunks):
    hs = []
    h_in = h0
    for c in range(n_chunks):
        ps, hz = [], []
        for q in range(SEG):
            r0 = c * CHUNK + q * SUBLANES
            aq, bq = a[r0:r0 + SUBLANES], b[r0:r0 + SUBLANES]
            if q == 0:
                p, h = aq, bq
            else:
                p, h = aq * p, aq * h + bq
            ps.append(p)
            hz.append(h)
        carries = [h_in]
        for s in range(SUBLANES):
            carries.append(h[s:s + 1] + p[s:s + 1] * carries[-1])
        h_in = carries[SUBLANES]
        carry = jnp.concatenate(carries[:SUBLANES], axis=0)
        hs.extend(hz[q] + ps[q] * carry for q in range(SEG))
    return jnp.concatenate(hs, axis=0), h_in


def _prompt_kernel(xa_ref, xb_ref, moda_ref, modb_ref,
                   w_main, w_dt, w_dtT, w_mg, lcw, lcb, w_gate, ba, bx, lam,
                   scw, scb, dtb, dtbT, alog, alogT, dexp, normw, wlp, wsp, wo, lng, lnb, e3,
                   y_ref, lh_ref, lc_ref, sh_ref, sc_ref,
                   lbuf, sbuf, ltail, stail, hl, hT, hb_s, dt_s, dtT_s, *, n_t):
    tc = xa_ref.shape[1]
    n_chunks = tc // CHUNK
    n = pl.program_id(0)
    j = lax.rem(jnp.maximum(n - 1, 0), n_t)

    @pl.when(n == 0)
    def _():
        lbuf[...] = jnp.zeros_like(lbuf)
        sbuf[...] = jnp.zeros_like(sbuf)
        hb_s[...] = jnp.zeros_like(hb_s)
        dt_s[...] = jnp.zeros_like(dt_s)
        dtT_s[...] = jnp.zeros_like(dtT_s)

    @pl.when(j == 0)
    def _():
        ltail[...] = jnp.zeros_like(ltail)
        stail[...] = jnp.zeros_like(stail)
        hl[...] = jnp.zeros_like(hl)
        hT[...] = jnp.zeros_like(hT)

    hb = hb_s[...]
    dt = dt_s[...]
    dtT = dtT_s[...]

    moda = moda_ref[0]
    ha = xa_ref[0] * (1.0 + moda[:, D_MODEL:2 * D_MODEL]) + moda[:, :D_MODEL]
    hab = _time_permute(ha, n_chunks, True).astype(BF16)
    x = xb_ref[0]
    gate = modb_ref[0][:, 2 * D_MODEL:]
    xbc_slab = CONV_DIM // 3

    def xbc_next(k):
        return _dot(hab, _w(w_main, OFF_XBC + k * xbc_slab, OFF_XBC + (k + 1) * xbc_slab))

    lru_x_next = _dot(hab, _w(w_main, OFF_LRU_X, OFF_LRU_X + D_LRU))
    u = _permuted_conv(lbuf, ltail, lcw, lcb, n_chunks)
    _fill_conv_buf(lbuf, lru_x_next, n_chunks, 0)

    xbc_next0 = xbc_next(0)
    a, mult, ig = _lru_gates(u, w_gate, ba[...], bx[...], lam[...])
    iu = ig * u
    row = lax.broadcasted_iota(jnp.int32, (tc, 1), 0)
    first = jnp.logical_and(row == 0, j == 0)
    lru_z = _dot(hb, _w(w_main, OFF_LRU_Z, OFF_LRU_Z + D_LRU))
    hs, h_end = _permuted_scan(a, jnp.where(first, iu, mult * iu), hl[...], n_chunks)
    hl[...] = h_end
    y_lru = hs * _silu(lru_z)
    xbc_next1 = xbc_next(1)
    xbc = _silu(_permuted_conv(sbuf, stail, scw, scb, n_chunks))
    xbc_next2 = xbc_next(2)
    for k, slab in enumerate((xbc_next0, xbc_next1, xbc_next2)):
        _fill_conv_buf(sbuf, slab, n_chunks, k * xbc_slab)
    dt_s[...] = _softplus(_dot(hab, w_dt[...]) + dtb[...])
    dtT_s[...] = _softplus(_dot_nt(w_dtT[...], hab) + dtbT[...])
    hb_s[...] = hab
    p_lru = _dot(y_lru.astype(BF16), _w(wlp))

    xs = xbc[:, :D_SSD]
    bm = xbc[:, D_SSD:D_SSD + GROUPS * STATE]
    cm = xbc[:, D_SSD + GROUPS * STATE:]
    a_row = -LOG2_E * jnp.exp(alog[...])
    a_col = -LOG2_E * jnp.exp(alogT[...])

    ri = lax.broadcasted_iota(jnp.int32, (CHUNK, CHUNK), 0)
    ci = lax.broadcasted_iota(jnp.int32, (CHUNK, CHUNK), 1)
    t_row = jnp.bitwise_and(ri, SUBLANES - 1) * SEG + jnp.right_shift(ri, 3)
    t_col = jnp.bitwise_and(ci, SUBLANES - 1) * SEG + jnp.right_shift(ci, 3)
    causal = t_row >= t_col
    tril = causal.astype(BF16)
    triu = (t_row <= t_col).astype(BF16)
    lane = lax.broadcasted_iota(jnp.int32, (CHUNK, LANES), 1)
    lo_half = lane < HEAD_DIM

    y_chunks = []
    late_proj = []
    for c in range(n_chunks):
        r0 = c * CHUNK
        if c == 0:
            late_proj.append(_dot(hb, _w(w_main, OFF_SSD_Z, OFF_SSD_Z + D_SSD)))
        if c == n_chunks - 1:
            late_proj.append(_dot(hb, _w(w_mg)))
        xs_c = xs[r0:r0 + CHUNK]
        xs_cb = xs_c.astype(BF16)
        bm_c = bm[r0:r0 + CHUNK].astype(BF16)
        cm_c = cm[r0:r0 + CHUNK].astype(BF16)
        dt_c = dt[r0:r0 + CHUNK]
        dtT_c = dtT[:, r0:r0 + CHUNK]
        acs = _dot_exact_rhs(tril, dt_c * a_row)
        acsT = _dot_exact_lhs(dtT_c * a_col, triu)
        e_exp = _expand_heads(jnp.exp2(acs), e3[...])
        w_exp = _expand_heads(jnp.exp2(acs[CHUNK - 1:CHUNK, :] - acs) * dt_c, e3[...])

        y_parts = []
        for g in range(GROUPS):
            cg = cm_c[:, g * STATE:(g + 1) * STATE]
            bg = bm_c[:, g * STATE:(g + 1) * STATE]
            cb = _dot_nt(cg, bg)
            for q in range(HEADS_PER_GROUP // 2):
                pair = g * (HEADS_PER_GROUP // 2) + q
                xp = xs_cb[:, pair * LANES:(pair + 1) * LANES]
                zero = jnp.zeros_like(xp)
                ms = []
                for half in range(2):
                    h = 2 * pair + half
                    seg = acs[:, h:h + 1] - acsT[h:h + 1, :]
                    m = jnp.exp2(jnp.where(causal, seg, -jnp.inf)) * cb * dtT_c[h:h + 1, :]
                    ms.append(m.astype(BF16))
                x2 = jnp.concatenate([jnp.where(lo_half, xp, zero), jnp.where(lo_half, zero, xp)], axis=0)
                y_parts.append(_dot(jnp.concatenate(ms, axis=1), x2))
        y_diag = jnp.concatenate(y_parts, axis=1)

        h_prev = hT[...]
        h_prev_b = h_prev.astype(BF16)
        xsw = (xs_c * w_exp).astype(BF16)
        off_parts, st_parts = [], []
        for g in range(GROUPS):
            sl = slice(g * GROUP_W, (g + 1) * GROUP_W)
            off_parts.append(_dot(cm_c[:, g * STATE:(g + 1) * STATE], h_prev_b[:, sl]))
            st_parts.append(_dot_tn(bm_c[:, g * STATE:(g + 1) * STATE], xsw[:, sl]))
        y_off = jnp.concatenate(off_parts, axis=1) * e_exp
        hT[...] = e_exp[CHUNK - 1:CHUNK, :] * h_prev + jnp.concatenate(st_parts, axis=1)
        y_chunks.append(y_diag + y_off + dexp[...] * xs_c)
    y = jnp.concatenate(y_chunks, axis=0) if len(y_chunks) > 1 else y_chunks[0]
    ssd_z, merge_logits = late_proj

    y_ssd = _gated_rmsnorm(y, ssd_z, normw[...])
    p_ssd = _dot(y_ssd.astype(BF16), _w(wsp))

    gm = _sigmoid(merge_logits)
    merged = gm[:, :D_MODEL] * p_lru + gm[:, D_MODEL:] * p_ssd
    out = _dot(merged.astype(BF16), _w(wo))
    y_ref[0] = _layer_norm(ALPHA * x + gate * _time_permute(out, n_chunks, False), lng[...], lnb[...])

    @pl.when(jnp.logical_and(n > 0, j == n_t - 1))
    def _():
        for i in range(CONV_W - 1):
            r = (i + 1) * SUBLANES - 1
            lc_ref[0, 0, i:i + 1, :] = ltail[r:r + 1, :]
            sc_ref[0, 0, i:i + 1, :] = stail[r:r + 1, :]
        lh_ref[0] = hl[...]
        hfin = hT[...]
        for q in range(D_SSD // LANES):
            sh_ref[0, q * LANES:(q + 1) * LANES, :] = hfin[:, q * LANES:(q + 1) * LANES].T


def _sample_in_kernel(x_ref, mod_ref, lh0_ref, lst_ref, sst_ref, w_main, w_dt, w_mg, lcw, lcb,
                      w_gate, ba, bx, lam, scw, scb, dtb, alog, wlp, e3,
                      lh_ref, lco_ref, sco_ref, plru_ref, xbc_ref, xdt_ref, da_ref, z_ref, gm_ref):
    x = x_ref[...]
    mod = mod_ref[...]
    shift = mod[:, :D_MODEL]
    scale = mod[:, D_MODEL:2 * D_MODEL]
    hb = (x * (1.0 + scale) + shift).astype(BF16)

    lru_x = _dot(hb, _w(w_main, OFF_LRU_X, OFF_LRU_X + D_LRU))
    xbc_raw = _dot(hb, _w(w_main, OFF_XBC, OFF_XBC + CONV_DIM))
    u = lcb[...] + lcw[CONV_W - 1:CONV_W, :] * lru_x
    xbc = scb[...] + scw[CONV_W - 1:CONV_W, :] * xbc_raw
    for k in range(CONV_W - 1):
        u = u + lcw[k:k + 1, :] * lst_ref[k]
        xbc = xbc + scw[k:k + 1, :] * sst_ref[k]
    for k in range(CONV_W - 2):
        lco_ref[k] = lst_ref[k + 1]
        sco_ref[k] = sst_ref[k + 1]
    lco_ref[CONV_W - 2] = lru_x
    sco_ref[CONV_W - 2] = xbc_raw
    xbc = _silu(xbc)
    xbc_ref[...] = xbc

    a, mult, ig = _lru_gates(u, w_gate, ba[...], bx[...], lam[...])
    h_new = a * lh0_ref[...] + mult * ig * u
    lh_ref[...] = h_new
    lru_z = _dot(hb, _w(w_main, OFF_LRU_Z, OFF_LRU_Z + D_LRU))
    plru_ref[...] = _dot((h_new * _silu(lru_z)).astype(BF16), _w(wlp))

    dt = _softplus(_dot(hb, w_dt[...]) + dtb[...])
    da_ref[...] = jnp.exp(dt * (-jnp.exp(alog[...])))
    xdt_ref[...] = xbc[:, :D_SSD] * _expand_heads(dt, e3[...])
    z_ref[...] = _dot(hb, _w(w_main, OFF_SSD_Z, OFF_SSD_Z + D_SSD))
    gm_ref[...] = _sigmoid(_dot(hb, _w(w_mg)))


def _sample_state_kernel(da_ref, h0_ref, xdt_ref, b_ref, c_ref, ho_ref, y_ref):
    i = pl.program_id(0)
    bb = h0_ref.shape[0]
    n_terms = 6
    pad_rows = LANES - n_terms * bb

    def stack(terms, width):
        return jnp.concatenate(list(terms) + [jnp.zeros((pad_rows, width), F32)], axis=0)

    xh, xm, xl = _split3(xdt_ref[...])
    bh, bm, bl = _split3(b_ref[...])
    x6t = stack((xh, xh, xm, xm, xh, xl), D_SSD).T.astype(BF16)
    b6 = stack((bh, bm, bh, bm, bl, bh), GROUPS * STATE)
    ch, cmid, cl = _split3(c_ref[...])
    c3 = jnp.concatenate([ch, cmid, cl], axis=0).astype(BF16)
    krow = lax.broadcasted_iota(jnp.int32, (LANES, 1), 0)
    yrow = lax.broadcasted_iota(jnp.int32, (3 * bb, 1), 0)
    for b in range(bb):
        own = jnp.logical_and(jnp.bitwise_and(krow, bb - 1) == b, krow < n_terms * bb)
        rhs = jnp.where(own, b6, 0.0).astype(BF16)
        y_own = jnp.bitwise_and(yrow, bb - 1) == b
        y_parts = []
        for g in range(GROUPS):
            st = _dot(x6t[g * GROUP_W:(g + 1) * GROUP_W, :], rhs[:, g * STATE:(g + 1) * STATE])
            hn_parts = []
            for hh in range(HEADS_PER_GROUP):
                h = g * HEADS_PER_GROUP + hh
                rows = slice(h * HEAD_DIM, (h + 1) * HEAD_DIM)
                hn = da_ref[i * bb + b, h] * h0_ref[b, rows, :] + st[hh * HEAD_DIM:(hh + 1) * HEAD_DIM, :]
                ho_ref[b, rows, :] = hn
                hn_parts.append(hn.astype(BF16))
            hn_g = jnp.concatenate(hn_parts, axis=0)
            r = _dot_nt(c3[:, g * STATE:(g + 1) * STATE], hn_g)
            y_parts.append(jnp.sum(jnp.where(y_own, r, 0.0), axis=0, keepdims=True))
        y_ref[b:b + 1, :] = jnp.concatenate(y_parts, axis=1)


def _sample_out_kernel(x_ref, mod_ref, yraw_ref, xbc_ref, z_ref, gm_ref, plru_ref,
                       dexp, normw, wsp, wo, lng, lnb, o_ref):
    x = x_ref[...]
    gate = mod_ref[:, 2 * D_MODEL:]
    y = yraw_ref[...] + dexp[...] * xbc_ref[:, :D_SSD]
    y_ssd = _gated_rmsnorm(y, z_ref[...], normw[...])
    p_ssd = _dot(y_ssd.astype(BF16), _w(wsp))
    gm = gm_ref[...]
    merged = gm[:, :D_MODEL] * plru_ref[...] + gm[:, D_MODEL:] * p_ssd
    out = _dot(merged.astype(BF16), _w(wo))
    o_ref[...] = _layer_norm(ALPHA * x + gate * out, lng[...], lnb[...])


def _resident(shape):
    zeros = (0,) * len(shape)
    return pl.BlockSpec(shape, lambda n: zeros, pipeline_mode=pl.Buffered(1))


def _whole(shape):
    return jax.ShapeDtypeStruct(shape, F32)


def kernel(x_prompt, x_sample, state_lru_h, state_lru_conv, state_ssd_h, state_ssd_conv, c_prompt, c_sample, w_cond, b_cond, w_in, lru_conv_w, lru_conv_b, lru_wa, lru_ba, lru_wx, lru_bx, lru_lambda, ssd_conv_w, ssd_conv_b, ssd_dt_bias, ssd_a_log, ssd_d, ssd_norm_w, w_lru_proj, w_ssd_proj, w_out, ln_g, ln_b):
    nbp, seq, _ = x_prompt.shape
    nbs = x_sample.shape[0]
    tc = PROMPT_TC
    assert seq % tc == 0 and tc % CHUNK == 0 and nbs % SAMPLE_BB == 0 and SAMPLE_BB == SUBLANES

    w_in0 = w_in[0]
    w_main = w_in0[:, :N_MAIN].astype(BF16)
    w_dt_cols = w_in0[:, N_MAIN:N_MAIN + HEADS]
    w_dt = jnp.pad(w_dt_cols, ((0, 0), (0, LANES - HEADS))).astype(BF16)
    w_dtT = w_dt_cols.T.astype(BF16)
    w_mg = w_in0[:, N_MAIN + HEADS:].astype(BF16)
    nq = D_LRU // LANES
    wa = lru_wa[0].reshape(nq, 2, LRU_BLOCK_W, LRU_BLOCK_W)
    wx = lru_wx[0].reshape(nq, 2, LRU_BLOCK_W, LRU_BLOCK_W)
    zb = jnp.zeros((nq, LRU_BLOCK_W, LRU_BLOCK_W), F32)

    def blockdiag(w):
        top = jnp.concatenate([w[:, 0], zb], axis=2)
        bot = jnp.concatenate([zb, w[:, 1]], axis=2)
        return jnp.concatenate([top, bot], axis=1)

    w_gate = jnp.concatenate([blockdiag(wa), blockdiag(wx)], axis=2).astype(BF16)
    row = lambda v: v.reshape(1, -1)
    ba, bx, lam = row(lru_ba[0]), row(lru_bx[0]), row(lru_lambda[0])
    lcw, lcb = lru_conv_w[0], row(lru_conv_b[0])
    scw, scb = ssd_conv_w[0], row(ssd_conv_b[0])
    dtb = jnp.pad(row(ssd_dt_bias[0]), ((0, 0), (0, LANES - HEADS)))
    alog = jnp.pad(row(ssd_a_log[0]), ((0, 0), (0, LANES - HEADS)))
    dtbT = ssd_dt_bias[0].reshape(HEADS, 1)
    alogT = ssd_a_log[0].reshape(HEADS, 1)
    dexp = row(jnp.repeat(ssd_d[0], HEAD_DIM))
    normw = row(ssd_norm_w[0])
    wlp = w_lru_proj[0].astype(BF16)
    wsp = w_ssd_proj[0].astype(BF16)
    wo = w_out[0].astype(BF16)
    lng, lnb = row(ln_g[0]), row(ln_b[0])
    k_idx = jnp.arange(LANES)[:, None]
    c_idx = jnp.arange(D_SSD)[None, :]
    e3 = jnp.logical_and(k_idx % HEADS == c_idx // HEAD_DIM, k_idx < 3 * HEADS).astype(BF16)

    c_all = jnp.concatenate([c_prompt, c_sample], axis=0)
    mod = pl.pallas_call(
        _cond_kernel,
        out_shape=_whole((nbp + nbs, 3 * D_MODEL)),
        name="cond",
    )(c_all, w_cond[0].astype(BF16), row(b_cond[0]))
    mod_p = mod[:nbp].reshape(nbp, 1, 3 * D_MODEL)
    mod_s = mod[nbp:]

    weights = (w_main, w_dt, w_dtT, w_mg, lcw, lcb, w_gate, ba, bx, lam, scw, scb, dtb, dtbT,
               alog, alogT, dexp, normw, wlp, wsp, wo, lng, lnb, e3)
    n_chunks = tc // CHUNK
    n_t = seq // tc
    n_blocks = nbp * n_t

    def blk_a(n):
        m = jnp.minimum(n, n_blocks - 1)
        return m // n_t, m % n_t

    def blk_b(n):
        m = jnp.maximum(n - 1, 0)
        return m // n_t, m % n_t

    y_p, lh_p, lc_p, sh_p, sc_p = pl.pallas_call(
        functools.partial(_prompt_kernel, n_t=n_t),
        grid=(n_blocks + 1,),
        in_specs=[pl.BlockSpec((1, tc, D_MODEL), lambda n: (*blk_a(n), 0)),
                  pl.BlockSpec((1, tc, D_MODEL), lambda n: (*blk_b(n), 0)),
                  pl.BlockSpec((1, 1, 3 * D_MODEL), lambda n: (blk_a(n)[0], 0, 0)),
                  pl.BlockSpec((1, 1, 3 * D_MODEL), lambda n: (blk_b(n)[0], 0, 0))]
                 + [_resident(w.shape) for w in weights],
        out_specs=[pl.BlockSpec((1, tc, D_MODEL), lambda n: (*blk_b(n), 0)),
                   pl.BlockSpec((1, 1, D_LRU), lambda n: (blk_b(n)[0], 0, 0)),
                   pl.BlockSpec((1, 1, CONV_W - 1, D_LRU), lambda n: (0, blk_b(n)[0], 0, 0)),
                   pl.BlockSpec((1, D_SSD, STATE), lambda n: (blk_b(n)[0], 0, 0)),
                   pl.BlockSpec((1, 1, CONV_W - 1, CONV_DIM), lambda n: (0, blk_b(n)[0], 0, 0))],
        out_shape=[_whole((nbp, seq, D_MODEL)), _whole((nbp, 1, D_LRU)),
                   _whole((1, nbp, CONV_W - 1, D_LRU)), _whole((nbp, D_SSD, STATE)),
                   _whole((1, nbp, CONV_W - 1, CONV_DIM))],
        scratch_shapes=[pltpu.VMEM((n_chunks, HALO + CHUNK, D_LRU), F32),
                        pltpu.VMEM((n_chunks, HALO + CHUNK, CONV_DIM), F32),
                        pltpu.VMEM((HALO, D_LRU), F32),
                        pltpu.VMEM((HALO, CONV_DIM), F32),
                        pltpu.VMEM((1, D_LRU), F32),
                        pltpu.VMEM((STATE, D_SSD), F32),
                        pltpu.VMEM((tc, D_MODEL), BF16),
                        pltpu.VMEM((tc, LANES), F32),
                        pltpu.VMEM((HEADS, tc), F32)],
        compiler_params=pltpu.CompilerParams(
            dimension_semantics=("arbitrary",), vmem_limit_bytes=VMEM_LIMIT),
        name="prompt",
    )(x_prompt, x_prompt, mod_p, mod_p, *weights)

    xs2 = x_sample.reshape(nbs, D_MODEL)
    lh_s, lco, sco, p_lru, xbc_s, xdt, da, z_s, gm_s = pl.pallas_call(
        _sample_in_kernel,
        out_shape=[_whole((nbs, D_LRU)), _whole((CONV_W - 1, nbs, D_LRU)),
                   _whole((CONV_W - 1, nbs, CONV_DIM)), _whole((nbs, D_MODEL)),
                   _whole((nbs, CONV_DIM)), _whole((nbs, D_SSD)), _whole((nbs, LANES)),
                   _whole((nbs, D_SSD)), _whole((nbs, 2 * D_MODEL))],
        compiler_params=pltpu.CompilerParams(vmem_limit_bytes=VMEM_LIMIT),
        name="sample_in",
    )(xs2, mod_s, state_lru_h[0], jnp.transpose(state_lru_conv[0], (1, 0, 2)),
      jnp.transpose(state_ssd_conv[0], (1, 0, 2)), w_main, w_dt, w_mg,
      lcw, lcb, w_gate, ba, bx, lam, scw, scb, dtb, alog, wlp, e3)

    bb = SAMPLE_BB
    n_bc = GROUPS * STATE
    sh_s, y_raw = pl.pallas_call(
        _sample_state_kernel,
        grid=(nbs // bb,),
        in_specs=[pl.BlockSpec(memory_space=pltpu.SMEM),
                  pl.BlockSpec((bb, D_SSD, STATE), lambda i: (i, 0, 0)),
                  pl.BlockSpec((bb, D_SSD), lambda i: (i, 0)),
                  pl.BlockSpec((bb, n_bc), lambda i: (i, D_SSD // n_bc)),
                  pl.BlockSpec((bb, n_bc), lambda i: (i, D_SSD // n_bc + 1))],
        out_specs=[pl.BlockSpec((bb, D_SSD, STATE), lambda i: (i, 0, 0)),
                   pl.BlockSpec((bb, D_SSD), lambda i: (i, 0))],
        out_shape=[_whole((nbs, D_SSD, STATE)), _whole((nbs, D_SSD))],
        compiler_params=pltpu.CompilerParams(
            dimension_semantics=("arbitrary",), vmem_limit_bytes=VMEM_LIMIT),
        name="sample_state",
    )(da[:, :HEADS], state_ssd_h[0].reshape(nbs, D_SSD, STATE), xdt, xbc_s, xbc_s)

    y_s = pl.pallas_call(
        _sample_out_kernel,
        out_shape=_whole((nbs, D_MODEL)),
        compiler_params=pltpu.CompilerParams(vmem_limit_bytes=VMEM_LIMIT),
        name="sample_out",
    )(xs2, mod_s, y_raw, xbc_s, z_s, gm_s, p_lru, dexp, normw, wsp, wo, lng, lnb)

    return (y_p,
            y_s.reshape(nbs, 1, D_MODEL),
            lh_p.reshape(1, nbp, D_LRU),
            lc_p,
            sh_p.reshape(1, nbp, HEADS, HEAD_DIM, STATE),
            sc_p,
            lh_s.reshape(1, nbs, D_LRU),
            jnp.transpose(lco, (1, 0, 2)).reshape(1, nbs, CONV_W - 1, D_LRU),
            sh_s.reshape(1, nbs, HEADS, HEAD_DIM, STATE),
            jnp.transpose(sco, (1, 0, 2)).reshape(1, nbs, CONV_W - 1, CONV_DIM))
```

```python
import jax
import jax.numpy as jnp
from jax import lax
from jax.experimental import pallas as pl
from jax.experimental.pallas import tpu as pltpu

F32 = jnp.float32
BF16 = jnp.bfloat16

D_MODEL = 1024
D_LRU = 1024
LRU_BLOCK_W = 64
LRU_C = 8.0
D_SSD = 2048
HEAD_DIM = 64
HEADS = 32
GROUPS = 4
HEADS_PER_GROUP = HEADS // GROUPS
STATE = 128
GROUP_W = D_SSD // GROUPS
CHUNK = 128
CONV_DIM = D_SSD + 2 * GROUPS * STATE
CONV_W = 4
LN_EPS = 1e-5
RMS_EPS = 1e-5
ALPHA = 2.0 ** 0.25
LOG2_E = 1.4426950408889634

LANES = 128
SUBLANES = 8
SEG = CHUNK // SUBLANES
HALO = (CONV_W - 1) * SUBLANES
N_MAIN = 2 * D_LRU + D_SSD + CONV_DIM
OFF_LRU_X, OFF_LRU_Z, OFF_SSD_Z, OFF_XBC = 0, D_LRU, 2 * D_LRU, 2 * D_LRU + D_SSD

PROMPT_TC = 256
SAMPLE_BB = 8
CAST_COLS = 512
VMEM_LIMIT = 60 * 1024 * 1024


def _sigmoid(x):
    return 0.5 + 0.5 * jnp.tanh(0.5 * x)


def _silu(x):
    h = 0.5 * x
    return h + h * jnp.tanh(h)


def _softplus(x):
    return jnp.maximum(x, 0.0) + jnp.log1p(jnp.exp(-jnp.abs(x)))


def _dot(a, b):
    return jnp.dot(a, b, preferred_element_type=F32)


def _dot_nt(a, b):
    return lax.dot_general(a, b, (((1,), (1,)), ((), ())), preferred_element_type=F32)


def _dot_tn(a, b):
    return lax.dot_general(a, b, (((0,), (0,)), ((), ())), preferred_element_type=F32)


def _w(ref, c0=None, c1=None):
    return ref[...] if c0 is None else ref[:, c0:c1]


def _split3(x):
    hi = x.astype(BF16).astype(F32)
    r = x - hi
    mid = r.astype(BF16).astype(F32)
    lo = r - mid
    return hi, mid, lo


def _dot_exact_rhs(lhs_bf16, x):
    hi, mid, lo = _split3(x)
    return (_dot(lhs_bf16, hi.astype(BF16)) + _dot(lhs_bf16, mid.astype(BF16))
            + _dot(lhs_bf16, lo.astype(BF16)))


def _dot_exact_lhs(x, rhs_bf16):
    hi, mid, lo = _split3(x)
    return (_dot(hi.astype(BF16), rhs_bf16) + _dot(mid.astype(BF16), rhs_bf16)
            + _dot(lo.astype(BF16), rhs_bf16))


def _expand_heads(v, e3):
    lane = lax.broadcasted_iota(jnp.int32, v.shape, 1)
    v = jnp.where(lane < HEADS, v, 0.0)
    hi, mid, lo = _split3(v)
    packed = hi + pltpu.roll(mid, HEADS, axis=1) + pltpu.roll(lo, 2 * HEADS, axis=1)
    return _dot(packed.astype(BF16), e3)


def _lru_gates(u, w_gate_ref, ba, bx, lam):
    ub = u.astype(BF16)
    r_parts, i_parts = [], []
    for q in range(D_LRU // LANES):
        gq = _dot(ub[:, q * LANES:(q + 1) * LANES], w_gate_ref[q])
        r_parts.append(gq[:, :LANES])
        i_parts.append(gq[:, LANES:])
    r = _sigmoid(jnp.concatenate(r_parts, axis=1) + ba)
    i = _sigmoid(jnp.concatenate(i_parts, axis=1) + bx)
    log_a = (-LRU_C) * r * _softplus(-lam)
    a = jnp.exp(log_a)
    v = -jnp.tanh(log_a) * (a * a + 1.0)
    mult = jnp.where(v > 0.0, v * lax.rsqrt(v), 0.0)
    return a, mult, i


def _gated_rmsnorm(y, z, norm_w):
    g = y * _silu(z)
    parts = []
    for gi in range(GROUPS):
        gg = g[:, gi * GROUP_W:(gi + 1) * GROUP_W]
        ms = jnp.mean(gg * gg, axis=-1, keepdims=True)
        parts.append(gg * lax.rsqrt(ms + RMS_EPS))
    return jnp.concatenate(parts, axis=1) * norm_w


def _layer_norm(v, g, b):
    mu = jnp.mean(v, axis=-1, keepdims=True)
    d = v - mu
    var = jnp.mean(d * d, axis=-1, keepdims=True)
    return d * lax.rsqrt(var + LN_EPS) * g + b


def _cond_kernel(c_ref, w_ref, b_ref, o_ref):
    o_ref[...] = _dot(c_ref[...].astype(BF16), w_ref[...].astype(BF16)) + b_ref[...]


def _w_in_cast_kernel(a_ref, b_ref, main_ref, mg_ref):
    i = pl.program_id(0)
    n_main = N_MAIN // CAST_COLS

    @pl.when(i < n_main)
    def _():
        main_ref[...] = a_ref[0].astype(BF16)

    @pl.when(i >= n_main)
    def _():
        mg_ref[...] = jnp.concatenate([a_ref[0, :, HEADS:], b_ref[0, :, :HEADS]], axis=1).astype(BF16)


def _time_permute(v, n_chunks, to_permuted):
    sub = lax.broadcasted_iota(jnp.int32, (SUBLANES, 1), 0)
    n_half = SEG // SUBLANES
    out = [None] * (n_chunks * SEG)
    for c in range(n_chunks):
        for h in range(n_half):
            natural = [c * SEG + n_half * s + h for s in range(SUBLANES)]
            permuted = [c * SEG + SUBLANES * h + i for i in range(SUBLANES)]
            src, dst = (natural, permuted) if to_permuted else (permuted, natural)
            rows = [v[g * SUBLANES:(g + 1) * SUBLANES] for g in src]
            d = SUBLANES // 2
            while d >= 1:
                keep = jnp.bitwise_and(sub, d) == 0
                for a in range(SUBLANES):
                    if a & d == 0:
                        lo, hi = rows[a], rows[a + d]
                        rows[a] = jnp.where(keep, lo, pltpu.roll(hi, d, axis=0))
                        rows[a + d] = jnp.where(keep, pltpu.roll(lo, SUBLANES - d, axis=0), hi)
                d //= 2
            for i, g in enumerate(dst):
                out[g] = rows[i]
    return jnp.concatenate(out, axis=0)


def _permuted_conv(raw, buf, tail, w_ref, b_ref, n_chunks):
    sub = lax.broadcasted_iota(jnp.int32, (SUBLANES, 1), 0)
    outs = []
    for c in range(n_chunks):
        buf[c, HALO:HALO + CHUNK, :] = raw[c * CHUNK:(c + 1) * CHUNK]
        for i in range(CONV_W - 1):
            r0 = CHUNK + i * SUBLANES
            cur = buf[c, r0:r0 + SUBLANES, :]
            if c == 0:
                prev = tail[i * SUBLANES:(i + 1) * SUBLANES, :]
            else:
                prev = buf[c - 1, r0:r0 + SUBLANES, :]
            buf[c, i * SUBLANES:(i + 1) * SUBLANES, :] = pltpu.roll(
                jnp.where(sub == SUBLANES - 1, prev, cur), 1, axis=0)
        acc = b_ref[...] + w_ref[0:1, :] * buf[c, 0:CHUNK, :]
        for k in range(1, CONV_W):
            acc = acc + w_ref[k:k + 1, :] * buf[c, k * SUBLANES:k * SUBLANES + CHUNK, :]
        outs.append(acc)
    tail[...] = buf[n_chunks - 1, CHUNK:CHUNK + HALO, :]
    return jnp.concatenate(outs, axis=0) if n_chunks > 1 else outs[0]


def _permuted_scan(a, b, h0, n_chunks):
    hs = []
    h_in = h0
    for c in range(n_chunks):
        ps, hz = [], []
        for q in range(SEG):
            r0 = c * CHUNK + q * SUBLANES
            aq, bq = a[r0:r0 + SUBLANES], b[r0:r0 + SUBLANES]
            if q == 0:
                p, h = aq, bq
            else:
                p, h = aq * p, aq * h + bq
            ps.append(p)
            hz.append(h)
        carries = [h_in]
        for s in range(SUBLANES):
            carries.append(h[s:s + 1] + p[s:s + 1] * carries[-1])
        h_in = carries[SUBLANES]
        carry = jnp.concatenate(carries[:SUBLANES], axis=0)
        hs.extend(hz[q] + ps[q] * carry for q in range(SEG))
    return jnp.concatenate(hs, axis=0), h_in


def _prompt_kernel(x_ref, mod_ref, w_main, w_dt, w_dtT, w_mg, lcw, lcb, w_gate, ba, bx, lam,
                   scw, scb, dtb, dtbT, alog, alogT, dexp, normw, wlp, wsp, wo, lng, lnb, e3,
                   y_ref, lh_ref, lc_ref, sh_ref, sc_ref,
                   lbuf, sbuf, ltail, stail, hl, hT):
    tc = x_ref.shape[1]
    n_chunks = tc // CHUNK
    j = pl.program_id(1)
    last = pl.num_programs(1) - 1

    @pl.when(j == 0)
    def _():
        ltail[...] = jnp.zeros_like(ltail)
        stail[...] = jnp.zeros_like(stail)
        hl[...] = jnp.zeros_like(hl)
        hT[...] = jnp.zeros_like(hT)

    x = x_ref[0]
    mod = mod_ref[0]
    shift = mod[:, :D_MODEL]
    scale = mod[:, D_MODEL:2 * D_MODEL]
    gate = mod[:, 2 * D_MODEL:]
    hb = _time_permute(x * (1.0 + scale) + shift, n_chunks, True).astype(BF16)

    lru_x = _dot(hb, _w(w_main, OFF_LRU_X, OFF_LRU_X + D_LRU))
    u = _permuted_conv(lru_x, lbuf, ltail, lcw, lcb, n_chunks)
    xbc_raw = _dot(hb, _w(w_main, OFF_XBC, OFF_XBC + CONV_DIM))

    a, mult, ig = _lru_gates(u, w_gate, ba[...], bx[...], lam[...])
    iu = ig * u
    row = lax.broadcasted_iota(jnp.int32, (tc, 1), 0)
    first = jnp.logical_and(row == 0, j == 0)
    lru_z = _dot(hb, _w(w_main, OFF_LRU_Z, OFF_LRU_Z + D_LRU))
    hs, h_end = _permuted_scan(a, jnp.where(first, iu, mult * iu), hl[...], n_chunks)
    hl[...] = h_end
    y_lru = hs * _silu(lru_z)
    xbc = _silu(_permuted_conv(xbc_raw, sbuf, stail, scw, scb, n_chunks))
    p_lru = _dot(y_lru.astype(BF16), _w(wlp))

    xs = xbc[:, :D_SSD]
    bm = xbc[:, D_SSD:D_SSD + GROUPS * STATE]
    cm = xbc[:, D_SSD + GROUPS * STATE:]
    dt = _softplus(_dot(hb, w_dt[...]) + dtb[...])
    dtT = _softplus(_dot_nt(w_dtT[...], hb) + dtbT[...])
    a_row = -LOG2_E * jnp.exp(alog[...])
    a_col = -LOG2_E * jnp.exp(alogT[...])

    ri = lax.broadcasted_iota(jnp.int32, (CHUNK, CHUNK), 0)
    ci = lax.broadcasted_iota(jnp.int32, (CHUNK, CHUNK), 1)
    t_row = jnp.bitwise_and(ri, SUBLANES - 1) * SEG + jnp.right_shift(ri, 3)
    t_col = jnp.bitwise_and(ci, SUBLANES - 1) * SEG + jnp.right_shift(ci, 3)
    causal = t_row >= t_col
    tril = causal.astype(BF16)
    triu = (t_row <= t_col).astype(BF16)
    lane = lax.broadcasted_iota(jnp.int32, (CHUNK, LANES), 1)
    lo_half = lane < HEAD_DIM

    y_chunks = []
    late_proj = []
    for c in range(n_chunks):
        r0 = c * CHUNK
        if c == 0:
            late_proj.append(_dot(hb, _w(w_main, OFF_SSD_Z, OFF_SSD_Z + D_SSD)))
        if c == n_chunks - 1:
            late_proj.append(_dot(hb, _w(w_mg)))
        xs_c = xs[r0:r0 + CHUNK]
        xs_cb = xs_c.astype(BF16)
        bm_c = bm[r0:r0 + CHUNK].astype(BF16)
        cm_c = cm[r0:r0 + CHUNK].astype(BF16)
        dt_c = dt[r0:r0 + CHUNK]
        dtT_c = dtT[:, r0:r0 + CHUNK]
        acs = _dot_exact_rhs(tril, dt_c * a_row)
        acsT = _dot_exact_lhs(dtT_c * a_col, triu)
        e_exp = _expand_heads(jnp.exp2(acs), e3[...])
        w_exp = _expand_heads(jnp.exp2(acs[CHUNK - 1:CHUNK, :] - acs) * dt_c, e3[...])

        y_parts = []
        for g in range(GROUPS):
            cg = cm_c[:, g * STATE:(g + 1) * STATE]
            bg = bm_c[:, g * STATE:(g + 1) * STATE]
            cb = _dot_nt(cg, bg)
            for q in range(HEADS_PER_GROUP // 2):
                pair = g * (HEADS_PER_GROUP // 2) + q
                xp = xs_cb[:, pair * LANES:(pair + 1) * LANES]
                zero = jnp.zeros_like(xp)
                ms = []
                for half in range(2):
                    h = 2 * pair + half
                    seg = acs[:, h:h + 1] - acsT[h:h + 1, :]
                    m = jnp.exp2(jnp.where(causal, seg, -jnp.inf)) * cb * dtT_c[h:h + 1, :]
                    ms.append(m.astype(BF16))
                x2 = jnp.concatenate([jnp.where(lo_half, xp, zero), jnp.where(lo_half, zero, xp)], axis=0)
                y_parts.append(_dot(jnp.concatenate(ms, axis=1), x2))
        y_diag = jnp.concatenate(y_parts, axis=1)

        h_prev = hT[...]
        h_prev_b = h_prev.astype(BF16)
        xsw = (xs_c * w_exp).astype(BF16)
        off_parts, st_parts = [], []
        for g in range(GROUPS):
            sl = slice(g * GROUP_W, (g + 1) * GROUP_W)
            off_parts.append(_dot(cm_c[:, g * STATE:(g + 1) * STATE], h_prev_b[:, sl]))
            st_parts.append(_dot_tn(bm_c[:, g * STATE:(g + 1) * STATE], xsw[:, sl]))
        y_off = jnp.concatenate(off_parts, axis=1) * e_exp
        hT[...] = e_exp[CHUNK - 1:CHUNK, :] * h_prev + jnp.concatenate(st_parts, axis=1)
        y_chunks.append(y_diag + y_off + dexp[...] * xs_c)
    y = jnp.concatenate(y_chunks, axis=0) if len(y_chunks) > 1 else y_chunks[0]
    ssd_z, merge_logits = late_proj

    y_ssd = _gated_rmsnorm(y, ssd_z, normw[...])
    p_ssd = _dot(y_ssd.astype(BF16), _w(wsp))

    gm = _sigmoid(merge_logits)
    merged = gm[:, :D_MODEL] * p_lru + gm[:, D_MODEL:] * p_ssd
    out = _dot(merged.astype(BF16), _w(wo))
    y_ref[0] = _layer_norm(ALPHA * x + gate * _time_permute(out, n_chunks, False), lng[...], lnb[...])

    @pl.when(j == last)
    def _():
        for i in range(CONV_W - 1):
            r = (i + 1) * SUBLANES - 1
            lc_ref[0, 0, i:i + 1, :] = ltail[r:r + 1, :]
            sc_ref[0, 0, i:i + 1, :] = stail[r:r + 1, :]
        lh_ref[0] = hl[...]
        hfin = hT[...]
        for q in range(D_SSD // LANES):
            sh_ref[0, q * LANES:(q + 1) * LANES, :] = hfin[:, q * LANES:(q + 1) * LANES].T


def _sample_in_kernel(x_ref, mod_ref, lh0_ref, lst_ref, sst_ref, w_main, w_dt, w_mg, lcw, lcb,
                      w_gate, ba, bx, lam, scw, scb, dtb, alog, wlp, e3,
                      lh_ref, lco_ref, sco_ref, plru_ref, xbc_ref, xdt_ref, da_ref, z_ref, gm_ref):
    x = x_ref[...]
    mod = mod_ref[...]
    shift = mod[:, :D_MODEL]
    scale = mod[:, D_MODEL:2 * D_MODEL]
    hb = (x * (1.0 + scale) + shift).astype(BF16)

    lru_x = _dot(hb, _w(w_main, OFF_LRU_X, OFF_LRU_X + D_LRU))
    xbc_raw = _dot(hb, _w(w_main, OFF_XBC, OFF_XBC + CONV_DIM))
    u = lcb[...] + lcw[CONV_W - 1:CONV_W, :] * lru_x
    xbc = scb[...] + scw[CONV_W - 1:CONV_W, :] * xbc_raw
    for k in range(CONV_W - 1):
        u = u + lcw[k:k + 1, :] * lst_ref[k]
        xbc = xbc + scw[k:k + 1, :] * sst_ref[k]
    for k in range(CONV_W - 2):
        lco_ref[k] = lst_ref[k + 1]
        sco_ref[k] = sst_ref[k + 1]
    lco_ref[CONV_W - 2] = lru_x
    sco_ref[CONV_W - 2] = xbc_raw
    xbc = _silu(xbc)
    xbc_ref[...] = xbc

    a, mult, ig = _lru_gates(u, w_gate, ba[...], bx[...], lam[...])
    h_new = a * lh0_ref[...] + mult * ig * u
    lh_ref[...] = h_new
    lru_z = _dot(hb, _w(w_main, OFF_LRU_Z, OFF_LRU_Z + D_LRU))
    plru_ref[...] = _dot((h_new * _silu(lru_z)).astype(BF16), _w(wlp))

    dt = _softplus(_dot(hb, w_dt[...]) + dtb[...])
    da_ref[...] = jnp.exp(dt * (-jnp.exp(alog[...])))
    xdt_ref[...] = xbc[:, :D_SSD] * _expand_heads(dt, e3[...])
    z_ref[...] = _dot(hb, _w(w_main, OFF_SSD_Z, OFF_SSD_Z + D_SSD))
    gm_ref[...] = _sigmoid(_dot(hb, _w(w_mg)))


def _sample_state_kernel(da_ref, h0_ref, xdt_ref, b_ref, c_ref, ho_ref, y_ref):
    i = pl.program_id(0)
    bb = h0_ref.shape[0]
    n_terms = 6
    pad_rows = LANES - n_terms * bb

    def stack(terms, width):
        return jnp.concatenate(list(terms) + [jnp.zeros((pad_rows, width), F32)], axis=0)

    xh, xm, xl = _split3(xdt_ref[...])
    bh, bm, bl = _split3(b_ref[...])
    x6t = stack((xh, xh, xm, xm, xh, xl), D_SSD).T.astype(BF16)
    b6 = stack((bh, bm, bh, bm, bl, bh), GROUPS * STATE)
    ch, cmid, cl = _split3(c_ref[...])
    c3 = jnp.concatenate([ch, cmid, cl], axis=0).astype(BF16)
    krow = lax.broadcasted_iota(jnp.int32, (LANES, 1), 0)
    yrow = lax.broadcasted_iota(jnp.int32, (3 * bb, 1), 0)
    for b in range(bb):
        own = jnp.logical_and(jnp.bitwise_and(krow, bb - 1) == b, krow < n_terms * bb)
        rhs = jnp.where(own, b6, 0.0).astype(BF16)
        y_own = jnp.bitwise_and(yrow, bb - 1) == b
        y_parts = []
        for g in range(GROUPS):
            st = _dot(x6t[g * GROUP_W:(g + 1) * GROUP_W, :], rhs[:, g * STATE:(g + 1) * STATE])
            hn_parts = []
            for hh in range(HEADS_PER_GROUP):
                h = g * HEADS_PER_GROUP + hh
                rows = slice(h * HEAD_DIM, (h + 1) * HEAD_DIM)
                hn = da_ref[i * bb + b, h] * h0_ref[b, rows, :] + st[hh * HEAD_DIM:(hh + 1) * HEAD_DIM, :]
                ho_ref[b, rows, :] = hn
                hn_parts.append(hn.astype(BF16))
            hn_g = jnp.concatenate(hn_parts, axis=0)
            r = _dot_nt(c3[:, g * STATE:(g + 1) * STATE], hn_g)
            y_parts.append(jnp.sum(jnp.where(y_own, r, 0.0), axis=0, keepdims=True))
        y_ref[b:b + 1, :] = jnp.concatenate(y_parts, axis=1)


def _sample_out_kernel(x_ref, mod_ref, yraw_ref, xbc_ref, z_ref, gm_ref, plru_ref,
                       dexp, normw, wsp, wo, lng, lnb, o_ref):
    x = x_ref[...]
    gate = mod_ref[:, 2 * D_MODEL:]
    y = yraw_ref[...] + dexp[...] * xbc_ref[:, :D_SSD]
    y_ssd = _gated_rmsnorm(y, z_ref[...], normw[...])
    p_ssd = _dot(y_ssd.astype(BF16), _w(wsp))
    gm = gm_ref[...]
    merged = gm[:, :D_MODEL] * plru_ref[...] + gm[:, D_MODEL:] * p_ssd
    out = _dot(merged.astype(BF16), _w(wo))
    o_ref[...] = _layer_norm(ALPHA * x + gate * out, lng[...], lnb[...])


def _resident(shape, grid_rank):
    zeros = (0,) * len(shape)
    if grid_rank == 1:
        imap = lambda i: zeros
    else:
        imap = lambda b, j: zeros
    return pl.BlockSpec(shape, imap, pipeline_mode=pl.Buffered(1))


def _whole(shape):
    return jax.ShapeDtypeStruct(shape, F32)


def kernel(x_prompt, x_sample, state_lru_h, state_lru_conv, state_ssd_h, state_ssd_conv, c_prompt, c_sample, w_cond, b_cond, w_in, lru_conv_w, lru_conv_b, lru_wa, lru_ba, lru_wx, lru_bx, lru_lambda, ssd_conv_w, ssd_conv_b, ssd_dt_bias, ssd_a_log, ssd_d, ssd_norm_w, w_lru_proj, w_ssd_proj, w_out, ln_g, ln_b):
    nbp, seq, _ = x_prompt.shape
    nbs = x_sample.shape[0]
    tc = PROMPT_TC
    assert seq % tc == 0 and tc % CHUNK == 0 and nbs % SAMPLE_BB == 0 and SAMPLE_BB == SUBLANES

    w_in0 = w_in[0]
    n_main_blk = N_MAIN // CAST_COLS
    n_mg_blk = 2 * D_MODEL // CAST_COLS
    w_main, w_mg = pl.pallas_call(
        _w_in_cast_kernel,
        grid=(n_main_blk + n_mg_blk,),
        in_specs=[pl.BlockSpec((1, D_MODEL, CAST_COLS), lambda i: (0, 0, i)),
                  pl.BlockSpec((1, D_MODEL, CAST_COLS), lambda i: (0, 0, jnp.maximum(i, n_main_blk) + 1))],
        out_specs=[pl.BlockSpec((D_MODEL, CAST_COLS), lambda i: (0, jnp.minimum(i, n_main_blk - 1))),
                   pl.BlockSpec((D_MODEL, CAST_COLS), lambda i: (0, jnp.maximum(i - n_main_blk, 0)))],
        out_shape=[jax.ShapeDtypeStruct((D_MODEL, N_MAIN), BF16),
                   jax.ShapeDtypeStruct((D_MODEL, 2 * D_MODEL), BF16)],
        compiler_params=pltpu.CompilerParams(dimension_semantics=("arbitrary",)),
        name="w_in_cast",
    )(w_in, w_in)
    w_dt_cols = w_in0[:, N_MAIN:N_MAIN + HEADS]
    w_dt = jnp.pad(w_dt_cols, ((0, 0), (0, LANES - HEADS))).astype(BF16)
    w_dtT = w_dt_cols.T.astype(BF16)
    nq = D_LRU // LANES
    wa = lru_wa[0].reshape(nq, 2, LRU_BLOCK_W, LRU_BLOCK_W)
    wx = lru_wx[0].reshape(nq, 2, LRU_BLOCK_W, LRU_BLOCK_W)
    zb = jnp.zeros((nq, LRU_BLOCK_W, LRU_BLOCK_W), F32)

    def blockdiag(w):
        top = jnp.concatenate([w[:, 0], zb], axis=2)
        bot = jnp.concatenate([zb, w[:, 1]], axis=2)
        return jnp.concatenate([top, bot], axis=1)

    w_gate = jnp.concatenate([blockdiag(wa), blockdiag(wx)], axis=2).astype(BF16)
    row = lambda v: v.reshape(1, -1)
    ba, bx, lam = row(lru_ba[0]), row(lru_bx[0]), row(lru_lambda[0])
    lcw, lcb = lru_conv_w[0], row(lru_conv_b[0])
    scw, scb = ssd_conv_w[0], row(ssd_conv_b[0])
    dtb = jnp.pad(row(ssd_dt_bias[0]), ((0, 0), (0, LANES - HEADS)))
    alog = jnp.pad(row(ssd_a_log[0]), ((0, 0), (0, LANES - HEADS)))
    dtbT = ssd_dt_bias[0].reshape(HEADS, 1)
    alogT = ssd_a_log[0].reshape(HEADS, 1)
    dexp = row(jnp.repeat(ssd_d[0], HEAD_DIM))
    normw = row(ssd_norm_w[0])
    wlp = w_lru_proj[0].astype(BF16)
    wsp = w_ssd_proj[0].astype(BF16)
    wo = w_out[0].astype(BF16)
    lng, lnb = row(ln_g[0]), row(ln_b[0])
    k_idx = jnp.arange(LANES)[:, None]
    c_idx = jnp.arange(D_SSD)[None, :]
    e3 = jnp.logical_and(k_idx % HEADS == c_idx // HEAD_DIM, k_idx < 3 * HEADS).astype(BF16)

    c_all = jnp.concatenate([c_prompt, c_sample], axis=0)
    mod = pl.pallas_call(
        _cond_kernel,
        out_shape=_whole((nbp + nbs, 3 * D_MODEL)),
        name="cond",
    )(c_all, w_cond[0], row(b_cond[0]))
    mod_p = mod[:nbp].reshape(nbp, 1, 3 * D_MODEL)
    mod_s = mod[nbp:]

    weights = (w_main, w_dt, w_dtT, w_mg, lcw, lcb, w_gate, ba, bx, lam, scw, scb, dtb, dtbT,
               alog, alogT, dexp, normw, wlp, wsp, wo, lng, lnb, e3)
    n_chunks = tc // CHUNK
    y_p, lh_p, lc_p, sh_p, sc_p = pl.pallas_call(
        _prompt_kernel,
        grid=(nbp, seq // tc),
        in_specs=[pl.BlockSpec((1, tc, D_MODEL), lambda b, j: (b, j, 0)),
                  pl.BlockSpec((1, 1, 3 * D_MODEL), lambda b, j: (b, 0, 0))]
                 + [_resident(w.shape, 2) for w in weights],
        out_specs=[pl.BlockSpec((1, tc, D_MODEL), lambda b, j: (b, j, 0)),
                   pl.BlockSpec((1, 1, D_LRU), lambda b, j: (b, 0, 0)),
                   pl.BlockSpec((1, 1, CONV_W - 1, D_LRU), lambda b, j: (0, b, 0, 0)),
                   pl.BlockSpec((1, D_SSD, STATE), lambda b, j: (b, 0, 0)),
                   pl.BlockSpec((1, 1, CONV_W - 1, CONV_DIM), lambda b, j: (0, b, 0, 0))],
        out_shape=[_whole((nbp, seq, D_MODEL)), _whole((nbp, 1, D_LRU)),
                   _whole((1, nbp, CONV_W - 1, D_LRU)), _whole((nbp, D_SSD, STATE)),
                   _whole((1, nbp, CONV_W - 1, CONV_DIM))],
        scratch_shapes=[pltpu.VMEM((n_chunks, HALO + CHUNK, D_LRU), F32),
                        pltpu.VMEM((n_chunks, HALO + CHUNK, CONV_DIM), F32),
                        pltpu.VMEM((HALO, D_LRU), F32),
                        pltpu.VMEM((HALO, CONV_DIM), F32),
                        pltpu.VMEM((1, D_LRU), F32),
                        pltpu.VMEM((STATE, D_SSD), F32)],
        compiler_params=pltpu.CompilerParams(
            dimension_semantics=("arbitrary", "arbitrary"), vmem_limit_bytes=VMEM_LIMIT),
        name="prompt",
    )(x_prompt, mod_p, *weights)

    xs2 = x_sample.reshape(nbs, D_MODEL)
    lh_s, lco, sco, p_lru, xbc_s, xdt, da, z_s, gm_s = pl.pallas_call(
        _sample_in_kernel,
        out_shape=[_whole((nbs, D_LRU)), _whole((CONV_W - 1, nbs, D_LRU)),
                   _whole((CONV_W - 1, nbs, CONV_DIM)), _whole((nbs, D_MODEL)),
                   _whole((nbs, CONV_DIM)), _whole((nbs, D_SSD)), _whole((nbs, LANES)),
                   _whole((nbs, D_SSD)), _whole((nbs, 2 * D_MODEL))],
        compiler_params=pltpu.CompilerParams(vmem_limit_bytes=VMEM_LIMIT),
        name="sample_in",
    )(xs2, mod_s, state_lru_h[0], jnp.transpose(state_lru_conv[0], (1, 0, 2)),
      jnp.transpose(state_ssd_conv[0], (1, 0, 2)), w_main, w_dt, w_mg,
      lcw, lcb, w_gate, ba, bx, lam, scw, scb, dtb, alog, wlp, e3)

    bb = SAMPLE_BB
    n_bc = GROUPS * STATE
    sh_s, y_raw = pl.pallas_call(
        _sample_state_kernel,
        grid=(nbs // bb,),
        in_specs=[pl.BlockSpec(memory_space=pltpu.SMEM),
                  pl.BlockSpec((bb, D_SSD, STATE), lambda i: (i, 0, 0)),
                  pl.BlockSpec((bb, D_SSD), lambda i: (i, 0)),
                  pl.BlockSpec((bb, n_bc), lambda i: (i, D_SSD // n_bc)),
                  pl.BlockSpec((bb, n_bc), lambda i: (i, D_SSD // n_bc + 1))],
        out_specs=[pl.BlockSpec((bb, D_SSD, STATE), lambda i: (i, 0, 0)),
                   pl.BlockSpec((bb, D_SSD), lambda i: (i, 0))],
        out_shape=[_whole((nbs, D_SSD, STATE)), _whole((nbs, D_SSD))],
        compiler_params=pltpu.CompilerParams(
            dimension_semantics=("arbitrary",), vmem_limit_bytes=VMEM_LIMIT),
        name="sample_state",
    )(da[:, :HEADS], state_ssd_h[0].reshape(nbs, D_SSD, STATE), xdt, xbc_s, xbc_s)

    y_s = pl.pallas_call(
        _sample_out_kernel,
        out_shape=_whole((nbs, D_MODEL)),
        compiler_params=pltpu.CompilerParams(vmem_limit_bytes=VMEM_LIMIT),
        name="sample_out",
    )(xs2, mod_s, y_raw, xbc_s, z_s, gm_s, p_lru, dexp, normw, wsp, wo, lng, lnb)

    return (y_p,
            y_s.reshape(nbs, 1, D_MODEL),
            lh_p.reshape(1, nbp, D_LRU),
            lc_p,
            sh_p.reshape(1, nbp, HEADS, HEAD_DIM, STATE),
            sc_p,
            lh_s.reshape(1, nbs, D_LRU),
            jnp.transpose(lco, (1, 0, 2)).reshape(1, nbs, CONV_W - 1, D_LRU),
            sh_s.reshape(1, nbs, HEADS, HEAD_DIM, STATE),
            jnp.transpose(sco, (1, 0, 2)).reshape(1, nbs, CONV_W - 1, CONV_DIM))
```

```python
import jax
import jax.numpy as jnp
from jax import lax
from jax.experimental import pallas as pl
from jax.experimental.pallas import tpu as pltpu

F32 = jnp.float32
BF16 = jnp.bfloat16

D_MODEL = 1024
D_LRU = 1024
LRU_BLOCK_W = 64
LRU_C = 8.0
D_SSD = 2048
HEAD_DIM = 64
HEADS = 32
GROUPS = 4
HEADS_PER_GROUP = HEADS // GROUPS
STATE = 128
GROUP_W = D_SSD // GROUPS
CHUNK = 128
CONV_DIM = D_SSD + 2 * GROUPS * STATE
CONV_W = 4
LN_EPS = 1e-5
RMS_EPS = 1e-5
ALPHA = 2.0 ** 0.25
LOG2_E = 1.4426950408889634

LANES = 128
SUBLANES = 8
SEG = CHUNK // SUBLANES
HALO = (CONV_W - 1) * SUBLANES
N_MAIN = 2 * D_LRU + D_SSD + CONV_DIM
OFF_LRU_X, OFF_LRU_Z, OFF_SSD_Z, OFF_XBC = 0, D_LRU, 2 * D_LRU, 2 * D_LRU + D_SSD

PROMPT_TC = 512
SAMPLE_BB = 8
VMEM_LIMIT = 60 * 1024 * 1024


def _sigmoid(x):
    return 0.5 + 0.5 * jnp.tanh(0.5 * x)


def _silu(x):
    h = 0.5 * x
    return h + h * jnp.tanh(h)


def _softplus(x):
    return jnp.maximum(x, 0.0) + jnp.log1p(jnp.exp(-jnp.abs(x)))


def _dot(a, b):
    return jnp.dot(a, b, preferred_element_type=F32)


def _dot_nt(a, b):
    return lax.dot_general(a, b, (((1,), (1,)), ((), ())), preferred_element_type=F32)


def _dot_tn(a, b):
    return lax.dot_general(a, b, (((0,), (0,)), ((), ())), preferred_element_type=F32)


def _w(ref, c0=None, c1=None):
    return ref[...] if c0 is None else ref[:, c0:c1]


def _split3(x):
    hi = x.astype(BF16).astype(F32)
    r = x - hi
    mid = r.astype(BF16).astype(F32)
    lo = r - mid
    return hi, mid, lo


def _dot_exact_rhs(lhs_bf16, x):
    hi, mid, lo = _split3(x)
    return (_dot(lhs_bf16, hi.astype(BF16)) + _dot(lhs_bf16, mid.astype(BF16))
            + _dot(lhs_bf16, lo.astype(BF16)))


def _dot_exact_lhs(x, rhs_bf16):
    hi, mid, lo = _split3(x)
    return (_dot(hi.astype(BF16), rhs_bf16) + _dot(mid.astype(BF16), rhs_bf16)
            + _dot(lo.astype(BF16), rhs_bf16))


def _expand_heads(v, e3):
    lane = lax.broadcasted_iota(jnp.int32, v.shape, 1)
    v = jnp.where(lane < HEADS, v, 0.0)
    hi, mid, lo = _split3(v)
    packed = hi + pltpu.roll(mid, HEADS, axis=1) + pltpu.roll(lo, 2 * HEADS, axis=1)
    return _dot(packed.astype(BF16), e3)


def _lru_gates(u, w_gate_ref, ba, bx, lam):
    ub = u.astype(BF16)
    r_parts, i_parts = [], []
    for q in range(D_LRU // LANES):
        gq = _dot(ub[:, q * LANES:(q + 1) * LANES], w_gate_ref[q])
        r_parts.append(gq[:, :LANES])
        i_parts.append(gq[:, LANES:])
    r = _sigmoid(jnp.concatenate(r_parts, axis=1) + ba)
    i = _sigmoid(jnp.concatenate(i_parts, axis=1) + bx)
    log_a = (-LRU_C) * r * _softplus(-lam)
    a = jnp.exp(log_a)
    v = -jnp.tanh(log_a) * (a * a + 1.0)
    mult = jnp.where(v > 0.0, v * lax.rsqrt(v), 0.0)
    return a, mult, i


def _gated_rmsnorm(y, z, norm_w):
    g = y * _silu(z)
    parts = []
    for gi in range(GROUPS):
        gg = g[:, gi * GROUP_W:(gi + 1) * GROUP_W]
        ms = jnp.mean(gg * gg, axis=-1, keepdims=True)
        parts.append(gg * lax.rsqrt(ms + RMS_EPS))
    return jnp.concatenate(parts, axis=1) * norm_w


def _layer_norm(v, g, b):
    mu = jnp.mean(v, axis=-1, keepdims=True)
    d = v - mu
    var = jnp.mean(d * d, axis=-1, keepdims=True)
    return d * lax.rsqrt(var + LN_EPS) * g + b


def _cond_kernel(c_ref, w_ref, b_ref, o_ref):
    o_ref[...] = _dot(c_ref[...].astype(BF16), w_ref[...].astype(BF16)) + b_ref[...]


def _time_permute(v, n_chunks, to_permuted):
    sub = lax.broadcasted_iota(jnp.int32, (SUBLANES, 1), 0)
    n_half = SEG // SUBLANES
    out = [None] * (n_chunks * SEG)
    for c in range(n_chunks):
        for h in range(n_half):
            natural = [c * SEG + n_half * s + h for s in range(SUBLANES)]
            permuted = [c * SEG + SUBLANES * h + i for i in range(SUBLANES)]
            src, dst = (natural, permuted) if to_permuted else (permuted, natural)
            rows = [v[g * SUBLANES:(g + 1) * SUBLANES] for g in src]
            d = SUBLANES // 2
            while d >= 1:
                keep = jnp.bitwise_and(sub, d) == 0
                for a in range(SUBLANES):
                    if a & d == 0:
                        lo, hi = rows[a], rows[a + d]
                        rows[a] = jnp.where(keep, lo, pltpu.roll(hi, d, axis=0))
                        rows[a + d] = jnp.where(keep, pltpu.roll(lo, SUBLANES - d, axis=0), hi)
                d //= 2
            for i, g in enumerate(dst):
                out[g] = rows[i]
    return jnp.concatenate(out, axis=0)


def _permuted_conv(raw, buf, tail, w_ref, b_ref, n_chunks):
    sub = lax.broadcasted_iota(jnp.int32, (SUBLANES, 1), 0)
    outs = []
    for c in range(n_chunks):
        buf[c, HALO:HALO + CHUNK, :] = raw[c * CHUNK:(c + 1) * CHUNK]
        for i in range(CONV_W - 1):
            r0 = CHUNK + i * SUBLANES
            cur = buf[c, r0:r0 + SUBLANES, :]
            if c == 0:
                prev = tail[i * SUBLANES:(i + 1) * SUBLANES, :]
            else:
                prev = buf[c - 1, r0:r0 + SUBLANES, :]
            buf[c, i * SUBLANES:(i + 1) * SUBLANES, :] = pltpu.roll(
                jnp.where(sub == SUBLANES - 1, prev, cur), 1, axis=0)
        acc = b_ref[...] + w_ref[0:1, :] * buf[c, 0:CHUNK, :]
        for k in range(1, CONV_W):
            acc = acc + w_ref[k:k + 1, :] * buf[c, k * SUBLANES:k * SUBLANES + CHUNK, :]
        outs.append(acc)
    tail[...] = buf[n_chunks - 1, CHUNK:CHUNK + HALO, :]
    return jnp.concatenate(outs, axis=0) if n_chunks > 1 else outs[0]


def _permuted_scan(a, b, h0, n_chunks):
    hs = []
    h_in = h0
    for c in range(n_chunks):
        ps, hz = [], []
        for q in range(SEG):
            r0 = c * CHUNK + q * SUBLANES
            aq, bq = a[r0:r0 + SUBLANES], b[r0:r0 + SUBLANES]
            if q == 0:
                p, h = aq, bq
            else:
                p, h = aq * p, aq * h + bq
            ps.append(p)
            hz.append(h)
        carries = [h_in]
        for s in range(SUBLANES):
            carries.append(h[s:s + 1] + p[s:s + 1] * carries[-1])
        h_in = carries[SUBLANES]
        carry = jnp.concatenate(carries[:SUBLANES], axis=0)
        hs.extend(hz[q] + ps[q] * carry for q in range(SEG))
    return jnp.concatenate(hs, axis=0), h_in


def _prompt_kernel(x_ref, mod_ref, w_main, w_dt, w_dtT, w_mg, lcw, lcb, w_gate, ba, bx, lam,
                   scw, scb, dtb, dtbT, alog, alogT, dexp, normw, wlp, wsp, wo, lng, lnb, e3,
                   y_ref, lh_ref, lc_ref, sh_ref, sc_ref,
                   lbuf, sbuf, ltail, stail, hl, hT):
    tc = x_ref.shape[1]
    n_chunks = tc // CHUNK
    j = pl.program_id(1)
    last = pl.num_programs(1) - 1

    @pl.when(j == 0)
    def _():
        ltail[...] = jnp.zeros_like(ltail)
        stail[...] = jnp.zeros_like(stail)
        hl[...] = jnp.zeros_like(hl)
        hT[...] = jnp.zeros_like(hT)

    x = x_ref[0]
    mod = mod_ref[0]
    shift = mod[:, :D_MODEL]
    scale = mod[:, D_MODEL:2 * D_MODEL]
    gate = mod[:, 2 * D_MODEL:]
    hb = _time_permute(x * (1.0 + scale) + shift, n_chunks, True).astype(BF16)

    lru_x = _dot(hb, _w(w_main, OFF_LRU_X, OFF_LRU_X + D_LRU))
    u = _permuted_conv(lru_x, lbuf, ltail, lcw, lcb, n_chunks)
    xbc_raw = _dot(hb, _w(w_main, OFF_XBC, OFF_XBC + CONV_DIM))

    a, mult, ig = _lru_gates(u, w_gate, ba[...], bx[...], lam[...])
    iu = ig * u
    row = lax.broadcasted_iota(jnp.int32, (tc, 1), 0)
    first = jnp.logical_and(row == 0, j == 0)
    lru_z = _dot(hb, _w(w_main, OFF_LRU_Z, OFF_LRU_Z + D_LRU))
    hs, h_end = _permuted_scan(a, jnp.where(first, iu, mult * iu), hl[...], n_chunks)
    hl[...] = h_end
    y_lru = hs * _silu(lru_z)
    xbc = _silu(_permuted_conv(xbc_raw, sbuf, stail, scw, scb, n_chunks))
    p_lru = _dot(y_lru.astype(BF16), _w(wlp))

    xs = xbc[:, :D_SSD]
    bm = xbc[:, D_SSD:D_SSD + GROUPS * STATE]
    cm = xbc[:, D_SSD + GROUPS * STATE:]
    dt = _softplus(_dot(hb, w_dt[...]) + dtb[...])
    dtT = _softplus(_dot_nt(w_dtT[...], hb) + dtbT[...])
    a_row = -LOG2_E * jnp.exp(alog[...])
    a_col = -LOG2_E * jnp.exp(alogT[...])

    ri = lax.broadcasted_iota(jnp.int32, (CHUNK, CHUNK), 0)
    ci = lax.broadcasted_iota(jnp.int32, (CHUNK, CHUNK), 1)
    t_row = jnp.bitwise_and(ri, SUBLANES - 1) * SEG + jnp.right_shift(ri, 3)
    t_col = jnp.bitwise_and(ci, SUBLANES - 1) * SEG + jnp.right_shift(ci, 3)
    causal = t_row >= t_col
    tril = causal.astype(BF16)
    triu = (t_row <= t_col).astype(BF16)
    lane = lax.broadcasted_iota(jnp.int32, (CHUNK, LANES), 1)
    lo_half = lane < HEAD_DIM

    y_chunks = []
    late_proj = []
    for c in range(n_chunks):
        r0 = c * CHUNK
        if c == 0:
            late_proj.append(_dot(hb, _w(w_main, OFF_SSD_Z, OFF_SSD_Z + D_SSD)))
        if c == n_chunks - 1:
            late_proj.append(_dot(hb, _w(w_mg)))
        xs_c = xs[r0:r0 + CHUNK]
        xs_cb = xs_c.astype(BF16)
        bm_c = bm[r0:r0 + CHUNK].astype(BF16)
        cm_c = cm[r0:r0 + CHUNK].astype(BF16)
        dt_c = dt[r0:r0 + CHUNK]
        dtT_c = dtT[:, r0:r0 + CHUNK]
        acs = _dot_exact_rhs(tril, dt_c * a_row)
        acsT = _dot_exact_lhs(dtT_c * a_col, triu)
        e_exp = _expand_heads(jnp.exp2(acs), e3[...])
        w_exp = _expand_heads(jnp.exp2(acs[CHUNK - 1:CHUNK, :] - acs) * dt_c, e3[...])

        y_parts = []
        for g in range(GROUPS):
            cg = cm_c[:, g * STATE:(g + 1) * STATE]
            bg = bm_c[:, g * STATE:(g + 1) * STATE]
            cb = _dot_nt(cg, bg)
            for q in range(HEADS_PER_GROUP // 2):
                pair = g * (HEADS_PER_GROUP // 2) + q
                xp = xs_cb[:, pair * LANES:(pair + 1) * LANES]
                zero = jnp.zeros_like(xp)
                ms = []
                for half in range(2):
                    h = 2 * pair + half
                    seg = acs[:, h:h + 1] - acsT[h:h + 1, :]
                    m = jnp.exp2(jnp.where(causal, seg, -jnp.inf)) * cb * dtT_c[h:h + 1, :]
                    ms.append(m.astype(BF16))
                x2 = jnp.concatenate([jnp.where(lo_half, xp, zero), jnp.where(lo_half, zero, xp)], axis=0)
                y_parts.append(_dot(jnp.concatenate(ms, axis=1), x2))
        y_diag = jnp.concatenate(y_parts, axis=1)

        h_prev = hT[...]
        h_prev_b = h_prev.astype(BF16)
        xsw = (xs_c * w_exp).astype(BF16)
        off_parts, st_parts = [], []
        for g in range(GROUPS):
            sl = slice(g * GROUP_W, (g + 1) * GROUP_W)
            off_parts.append(_dot(cm_c[:, g * STATE:(g + 1) * STATE], h_prev_b[:, sl]))
            st_parts.append(_dot_tn(bm_c[:, g * STATE:(g + 1) * STATE], xsw[:, sl]))
        y_off = jnp.concatenate(off_parts, axis=1) * e_exp
        hT[...] = e_exp[CHUNK - 1:CHUNK, :] * h_prev + jnp.concatenate(st_parts, axis=1)
        y_chunks.append(y_diag + y_off + dexp[...] * xs_c)
    y = jnp.concatenate(y_chunks, axis=0) if len(y_chunks) > 1 else y_chunks[0]
    ssd_z, merge_logits = late_proj

    y_ssd = _gated_rmsnorm(y, ssd_z, normw[...])
    p_ssd = _dot(y_ssd.astype(BF16), _w(wsp))

    gm = _sigmoid(merge_logits)
    merged = gm[:, :D_MODEL] * p_lru + gm[:, D_MODEL:] * p_ssd
    out = _dot(merged.astype(BF16), _w(wo))
    y_ref[0] = _layer_norm(ALPHA * x + gate * _time_permute(out, n_chunks, False), lng[...], lnb[...])

    @pl.when(j == last)
    def _():
        for i in range(CONV_W - 1):
            r = (i + 1) * SUBLANES - 1
            lc_ref[0, 0, i:i + 1, :] = ltail[r:r + 1, :]
            sc_ref[0, 0, i:i + 1, :] = stail[r:r + 1, :]
        lh_ref[0] = hl[...]
        hfin = hT[...]
        for q in range(D_SSD // LANES):
            sh_ref[0, q * LANES:(q + 1) * LANES, :] = hfin[:, q * LANES:(q + 1) * LANES].T


def _sample_in_kernel(x_ref, mod_ref, lh0_ref, lst_ref, sst_ref, w_main, w_dt, w_mg, lcw, lcb,
                      w_gate, ba, bx, lam, scw, scb, dtb, alog, wlp, e3,
                      lh_ref, lco_ref, sco_ref, plru_ref, xbc_ref, xdt_ref, da_ref, z_ref, gm_ref):
    x = x_ref[...]
    mod = mod_ref[...]
    shift = mod[:, :D_MODEL]
    scale = mod[:, D_MODEL:2 * D_MODEL]
    hb = (x * (1.0 + scale) + shift).astype(BF16)

    lru_x = _dot(hb, _w(w_main, OFF_LRU_X, OFF_LRU_X + D_LRU))
    xbc_raw = _dot(hb, _w(w_main, OFF_XBC, OFF_XBC + CONV_DIM))
    u = lcb[...] + lcw[CONV_W - 1:CONV_W, :] * lru_x
    xbc = scb[...] + scw[CONV_W - 1:CONV_W, :] * xbc_raw
    for k in range(CONV_W - 1):
        u = u + lcw[k:k + 1, :] * lst_ref[k]
        xbc = xbc + scw[k:k + 1, :] * sst_ref[k]
    for k in range(CONV_W - 2):
        lco_ref[k] = lst_ref[k + 1]
        sco_ref[k] = sst_ref[k + 1]
    lco_ref[CONV_W - 2] = lru_x
    sco_ref[CONV_W - 2] = xbc_raw
    xbc = _silu(xbc)
    xbc_ref[...] = xbc

    a, mult, ig = _lru_gates(u, w_gate, ba[...], bx[...], lam[...])
    h_new = a * lh0_ref[...] + mult * ig * u
    lh_ref[...] = h_new
    lru_z = _dot(hb, _w(w_main, OFF_LRU_Z, OFF_LRU_Z + D_LRU))
    plru_ref[...] = _dot((h_new * _silu(lru_z)).astype(BF16), _w(wlp))

    dt = _softplus(_dot(hb, w_dt[...]) + dtb[...])
    da_ref[...] = jnp.exp(dt * (-jnp.exp(alog[...])))
    xdt_ref[...] = xbc[:, :D_SSD] * _expand_heads(dt, e3[...])
    z_ref[...] = _dot(hb, _w(w_main, OFF_SSD_Z, OFF_SSD_Z + D_SSD))
    gm_ref[...] = _sigmoid(_dot(hb, _w(w_mg)))


def _sample_state_kernel(da_ref, h0_ref, xdt_ref, b_ref, c_ref, ho_ref, y_ref):
    i = pl.program_id(0)
    bb = h0_ref.shape[0]
    n_terms = 6
    pad_rows = LANES - n_terms * bb

    def stack(terms, width):
        return jnp.concatenate(list(terms) + [jnp.zeros((pad_rows, width), F32)], axis=0)

    xh, xm, xl = _split3(xdt_ref[...])
    bh, bm, bl = _split3(b_ref[...])
    x6t = stack((xh, xh, xm, xm, xh, xl), D_SSD).T.astype(BF16)
    b6 = stack((bh, bm, bh, bm, bl, bh), GROUPS * STATE)
    ch, cmid, cl = _split3(c_ref[...])
    c3 = jnp.concatenate([ch, cmid, cl], axis=0).astype(BF16)
    krow = lax.broadcasted_iota(jnp.int32, (LANES, 1), 0)
    yrow = lax.broadcasted_iota(jnp.int32, (3 * bb, 1), 0)
    for b in range(bb):
        own = jnp.logical_and(jnp.bitwise_and(krow, bb - 1) == b, krow < n_terms * bb)
        rhs = jnp.where(own, b6, 0.0).astype(BF16)
        y_own = jnp.bitwise_and(yrow, bb - 1) == b
        y_parts = []
        for g in range(GROUPS):
            st = _dot(x6t[g * GROUP_W:(g + 1) * GROUP_W, :], rhs[:, g * STATE:(g + 1) * STATE])
            hn_parts = []
            for hh in range(HEADS_PER_GROUP):
                h = g * HEADS_PER_GROUP + hh
                rows = slice(h * HEAD_DIM, (h + 1) * HEAD_DIM)
                hn = da_ref[i * bb + b, h] * h0_ref[b, rows, :] + st[hh * HEAD_DIM:(hh + 1) * HEAD_DIM, :]
                ho_ref[b, rows, :] = hn
                hn_parts.append(hn.astype(BF16))
            hn_g = jnp.concatenate(hn_parts, axis=0)
            r = _dot_nt(c3[:, g * STATE:(g + 1) * STATE], hn_g)
            y_parts.append(jnp.sum(jnp.where(y_own, r, 0.0), axis=0, keepdims=True))
        y_ref[b:b + 1, :] = jnp.concatenate(y_parts, axis=1)


def _sample_out_kernel(x_ref, mod_ref, yraw_ref, xbc_ref, z_ref, gm_ref, plru_ref,
                       dexp, normw, wsp, wo, lng, lnb, o_ref):
    x = x_ref[...]
    gate = mod_ref[:, 2 * D_MODEL:]
    y = yraw_ref[...] + dexp[...] * xbc_ref[:, :D_SSD]
    y_ssd = _gated_rmsnorm(y, z_ref[...], normw[...])
    p_ssd = _dot(y_ssd.astype(BF16), _w(wsp))
    gm = gm_ref[...]
    merged = gm[:, :D_MODEL] * plru_ref[...] + gm[:, D_MODEL:] * p_ssd
    out = _dot(merged.astype(BF16), _w(wo))
    o_ref[...] = _layer_norm(ALPHA * x + gate * out, lng[...], lnb[...])


def _resident(shape, grid_rank):
    zeros = (0,) * len(shape)
    if grid_rank == 1:
        imap = lambda i: zeros
    else:
        imap = lambda b, j: zeros
    return pl.BlockSpec(shape, imap, pipeline_mode=pl.Buffered(1))


def _whole(shape):
    return jax.ShapeDtypeStruct(shape, F32)


def kernel(x_prompt, x_sample, state_lru_h, state_lru_conv, state_ssd_h, state_ssd_conv, c_prompt, c_sample, w_cond, b_cond, w_in, lru_conv_w, lru_conv_b, lru_wa, lru_ba, lru_wx, lru_bx, lru_lambda, ssd_conv_w, ssd_conv_b, ssd_dt_bias, ssd_a_log, ssd_d, ssd_norm_w, w_lru_proj, w_ssd_proj, w_out, ln_g, ln_b):
    nbp, seq, _ = x_prompt.shape
    nbs = x_sample.shape[0]
    tc = PROMPT_TC
    assert seq % tc == 0 and tc % CHUNK == 0 and nbs % SAMPLE_BB == 0 and SAMPLE_BB == SUBLANES

    w_in0 = w_in[0]
    w_main = w_in0[:, :N_MAIN].astype(BF16)
    w_mg = w_in0[:, N_MAIN + HEADS:].astype(BF16)
    w_dt_cols = w_in0[:, N_MAIN:N_MAIN + HEADS]
    w_dt = jnp.pad(w_dt_cols, ((0, 0), (0, LANES - HEADS))).astype(BF16)
    w_dtT = w_dt_cols.T.astype(BF16)
    nq = D_LRU // LANES
    wa = lru_wa[0].reshape(nq, 2, LRU_BLOCK_W, LRU_BLOCK_W)
    wx = lru_wx[0].reshape(nq, 2, LRU_BLOCK_W, LRU_BLOCK_W)
    zb = jnp.zeros((nq, LRU_BLOCK_W, LRU_BLOCK_W), F32)

    def blockdiag(w):
        top = jnp.concatenate([w[:, 0], zb], axis=2)
        bot = jnp.concatenate([zb, w[:, 1]], axis=2)
        return jnp.concatenate([top, bot], axis=1)

    w_gate = jnp.concatenate([blockdiag(wa), blockdiag(wx)], axis=2).astype(BF16)
    row = lambda v: v.reshape(1, -1)
    ba, bx, lam = row(lru_ba[0]), row(lru_bx[0]), row(lru_lambda[0])
    lcw, lcb = lru_conv_w[0], row(lru_conv_b[0])
    scw, scb = ssd_conv_w[0], row(ssd_conv_b[0])
    dtb = jnp.pad(row(ssd_dt_bias[0]), ((0, 0), (0, LANES - HEADS)))
    alog = jnp.pad(row(ssd_a_log[0]), ((0, 0), (0, LANES - HEADS)))
    dtbT = ssd_dt_bias[0].reshape(HEADS, 1)
    alogT = ssd_a_log[0].reshape(HEADS, 1)
    dexp = row(jnp.repeat(ssd_d[0], HEAD_DIM))
    normw = row(ssd_norm_w[0])
    wlp = w_lru_proj[0].astype(BF16)
    wsp = w_ssd_proj[0].astype(BF16)
    wo = w_out[0].astype(BF16)
    lng, lnb = row(ln_g[0]), row(ln_b[0])
    k_idx = jnp.arange(LANES)[:, None]
    c_idx = jnp.arange(D_SSD)[None, :]
    e3 = jnp.logical_and(k_idx % HEADS == c_idx // HEAD_DIM, k_idx < 3 * HEADS).astype(BF16)

    c_all = jnp.concatenate([c_prompt, c_sample], axis=0)
    mod = pl.pallas_call(
        _cond_kernel,
        out_shape=_whole((nbp + nbs, 3 * D_MODEL)),
        name="cond",
    )(c_all, w_cond[0], row(b_cond[0]))
    mod_p = mod[:nbp].reshape(nbp, 1, 3 * D_MODEL)
    mod_s = mod[nbp:]

    weights = (w_main, w_dt, w_dtT, w_mg, lcw, lcb, w_gate, ba, bx, lam, scw, scb, dtb, dtbT,
               alog, alogT, dexp, normw, wlp, wsp, wo, lng, lnb, e3)
    n_chunks = tc // CHUNK
    y_p, lh_p, lc_p, sh_p, sc_p = pl.pallas_call(
        _prompt_kernel,
        grid=(nbp, seq // tc),
        in_specs=[pl.BlockSpec((1, tc, D_MODEL), lambda b, j: (b, j, 0)),
                  pl.BlockSpec((1, 1, 3 * D_MODEL), lambda b, j: (b, 0, 0))]
                 + [_resident(w.shape, 2) for w in weights],
        out_specs=[pl.BlockSpec((1, tc, D_MODEL), lambda b, j: (b, j, 0)),
                   pl.BlockSpec((1, 1, D_LRU), lambda b, j: (b, 0, 0)),
                   pl.BlockSpec((1, 1, CONV_W - 1, D_LRU), lambda b, j: (0, b, 0, 0)),
                   pl.BlockSpec((1, D_SSD, STATE), lambda b, j: (b, 0, 0)),
                   pl.BlockSpec((1, 1, CONV_W - 1, CONV_DIM), lambda b, j: (0, b, 0, 0))],
        out_shape=[_whole((nbp, seq, D_MODEL)), _whole((nbp, 1, D_LRU)),
                   _whole((1, nbp, CONV_W - 1, D_LRU)), _whole((nbp, D_SSD, STATE)),
                   _whole((1, nbp, CONV_W - 1, CONV_DIM))],
        scratch_shapes=[pltpu.VMEM((n_chunks, HALO + CHUNK, D_LRU), F32),
                        pltpu.VMEM((n_chunks, HALO + CHUNK, CONV_DIM), F32),
                        pltpu.VMEM((HALO, D_LRU), F32),
                        pltpu.VMEM((HALO, CONV_DIM), F32),
                        pltpu.VMEM((1, D_LRU), F32),
                        pltpu.VMEM((STATE, D_SSD), F32)],
        compiler_params=pltpu.CompilerParams(
            dimension_semantics=("arbitrary", "arbitrary"), vmem_limit_bytes=VMEM_LIMIT),
        name="prompt",
    )(x_prompt, mod_p, *weights)

    xs2 = x_sample.reshape(nbs, D_MODEL)
    lh_s, lco, sco, p_lru, xbc_s, xdt, da, z_s, gm_s = pl.pallas_call(
        _sample_in_kernel,
        out_shape=[_whole((nbs, D_LRU)), _whole((CONV_W - 1, nbs, D_LRU)),
                   _whole((CONV_W - 1, nbs, CONV_DIM)), _whole((nbs, D_MODEL)),
                   _whole((nbs, CONV_DIM)), _whole((nbs, D_SSD)), _whole((nbs, LANES)),
                   _whole((nbs, D_SSD)), _whole((nbs, 2 * D_MODEL))],
        compiler_params=pltpu.CompilerParams(vmem_limit_bytes=VMEM_LIMIT),
        name="sample_in",
    )(xs2, mod_s, state_lru_h[0], jnp.transpose(state_lru_conv[0], (1, 0, 2)),
      jnp.transpose(state_ssd_conv[0], (1, 0, 2)), w_main, w_dt, w_mg,
      lcw, lcb, w_gate, ba, bx, lam, scw, scb, dtb, alog, wlp, e3)

    bb = SAMPLE_BB
    n_bc = GROUPS * STATE
    sh_s, y_raw = pl.pallas_call(
        _sample_state_kernel,
        grid=(nbs // bb,),
        in_specs=[pl.BlockSpec(memory_space=pltpu.SMEM),
                  pl.BlockSpec((bb, D_SSD, STATE), lambda i: (i, 0, 0)),
                  pl.BlockSpec((bb, D_SSD), lambda i: (i, 0)),
                  pl.BlockSpec((bb, n_bc), lambda i: (i, D_SSD // n_bc)),
                  pl.BlockSpec((bb, n_bc), lambda i: (i, D_SSD // n_bc + 1))],
        out_specs=[pl.BlockSpec((bb, D_SSD, STATE), lambda i: (i, 0, 0)),
                   pl.BlockSpec((bb, D_SSD), lambda i: (i, 0))],
        out_shape=[_whole((nbs, D_SSD, STATE)), _whole((nbs, D_SSD))],
        compiler_params=pltpu.CompilerParams(
            dimension_semantics=("arbitrary",), vmem_limit_bytes=VMEM_LIMIT),
        name="sample_state",
    )(da[:, :HEADS], state_ssd_h[0].reshape(nbs, D_SSD, STATE), xdt, xbc_s, xbc_s)

    y_s = pl.pallas_call(
        _sample_out_kernel,
        out_shape=_whole((nbs, D_MODEL)),
        compiler_params=pltpu.CompilerParams(vmem_limit_bytes=VMEM_LIMIT),
        name="sample_out",
    )(xs2, mod_s, y_raw, xbc_s, z_s, gm_s, p_lru, dexp, normw, wsp, wo, lng, lnb)

    return (y_p,
            y_s.reshape(nbs, 1, D_MODEL),
            lh_p.reshape(1, nbp, D_LRU),
            lc_p,
            sh_p.reshape(1, nbp, HEADS, HEAD_DIM, STATE),
            sc_p,
            lh_s.reshape(1, nbs, D_LRU),
            jnp.transpose(lco, (1, 0, 2)).reshape(1, nbs, CONV_W - 1, D_LRU),
            sh_s.reshape(1, nbs, HEADS, HEAD_DIM, STATE),
            jnp.transpose(sco, (1, 0, 2)).reshape(1, nbs, CONV_W - 1, CONV_DIM))
```

```python
import jax
import jax.numpy as jnp
from jax import lax
from jax.experimental import pallas as pl
from jax.experimental.pallas import tpu as pltpu

F32 = jnp.float32
BF16 = jnp.bfloat16

D_MODEL = 1024
D_LRU = 1024
LRU_BLOCK_W = 64
LRU_C = 8.0
D_SSD = 2048
HEAD_DIM = 64
HEADS = 32
GROUPS = 4
HEADS_PER_GROUP = HEADS // GROUPS
STATE = 128
GROUP_W = D_SSD // GROUPS
CHUNK = 128
CONV_DIM = D_SSD + 2 * GROUPS * STATE
CONV_W = 4
LN_EPS = 1e-5
RMS_EPS = 1e-5
ALPHA = 2.0 ** 0.25
LOG2_E = 1.4426950408889634

LANES = 128
SUBLANES = 8
SEG = CHUNK // SUBLANES
HALO = (CONV_W - 1) * SUBLANES
N_MAIN = 2 * D_LRU + D_SSD + CONV_DIM
OFF_LRU_X, OFF_LRU_Z, OFF_SSD_Z, OFF_XBC = 0, D_LRU, 2 * D_LRU, 2 * D_LRU + D_SSD

PROMPT_TC = 512
SAMPLE_BB = 8
CONV_SLAB = 512
STAGE_ROWS, STAGE_COLS = 1024, 256
VMEM_LIMIT = 60 * 1024 * 1024


def _sigmoid(x):
    return 0.5 + 0.5 * jnp.tanh(0.5 * x)


def _silu(x):
    h = 0.5 * x
    return h + h * jnp.tanh(h)


def _softplus(x):
    return jnp.maximum(x, 0.0) + jnp.log1p(jnp.exp(-jnp.abs(x)))


def _dot(a, b):
    return jnp.dot(a, b, preferred_element_type=F32)


def _dot_nt(a, b):
    return lax.dot_general(a, b, (((1,), (1,)), ((), ())), preferred_element_type=F32)


def _dot_tn(a, b):
    return lax.dot_general(a, b, (((0,), (0,)), ((), ())), preferred_element_type=F32)


def _w(ref, c0=None, c1=None):
    return ref[...] if c0 is None else ref[:, c0:c1]


def _split3(x):
    hi = x.astype(BF16).astype(F32)
    r = x - hi
    mid = r.astype(BF16).astype(F32)
    lo = r - mid
    return hi, mid, lo


def _dot_exact_rhs(lhs_bf16, x):
    hi, mid, lo = _split3(x)
    return (_dot(lhs_bf16, hi.astype(BF16)) + _dot(lhs_bf16, mid.astype(BF16))
            + _dot(lhs_bf16, lo.astype(BF16)))


def _dot_exact_lhs(x, rhs_bf16):
    hi, mid, lo = _split3(x)
    return (_dot(hi.astype(BF16), rhs_bf16) + _dot(mid.astype(BF16), rhs_bf16)
            + _dot(lo.astype(BF16), rhs_bf16))


def _expand_heads(v, e3):
    lane = lax.broadcasted_iota(jnp.int32, v.shape, 1)
    v = jnp.where(lane < HEADS, v, 0.0)
    hi, mid, lo = _split3(v)
    packed = hi + pltpu.roll(mid, HEADS, axis=1) + pltpu.roll(lo, 2 * HEADS, axis=1)
    return _dot(packed.astype(BF16), e3)


def _lru_gates(u, w_gate_ref, ba, bx, lam):
    ub = u.astype(BF16)
    r_parts, i_parts = [], []
    for q in range(D_LRU // LANES):
        gq = _dot(ub[:, q * LANES:(q + 1) * LANES], w_gate_ref[q])
        r_parts.append(gq[:, :LANES])
        i_parts.append(gq[:, LANES:])
    r = _sigmoid(jnp.concatenate(r_parts, axis=1) + ba)
    i = _sigmoid(jnp.concatenate(i_parts, axis=1) + bx)
    log_a = (-LRU_C) * r * _softplus(-lam)
    a = jnp.exp(log_a)
    v = -jnp.tanh(log_a) * (a * a + 1.0)
    mult = jnp.where(v > 0.0, v * lax.rsqrt(v), 0.0)
    return a, mult, i


def _gated_rmsnorm(y, z, norm_w):
    g = y * _silu(z)
    parts = []
    for gi in range(GROUPS):
        gg = g[:, gi * GROUP_W:(gi + 1) * GROUP_W]
        ms = jnp.mean(gg * gg, axis=-1, keepdims=True)
        parts.append(gg * lax.rsqrt(ms + RMS_EPS))
    return jnp.concatenate(parts, axis=1) * norm_w


def _layer_norm(v, g, b):
    mu = jnp.mean(v, axis=-1, keepdims=True)
    d = v - mu
    var = jnp.mean(d * d, axis=-1, keepdims=True)
    return d * lax.rsqrt(var + LN_EPS) * g + b


def _cond_kernel(c_ref, w_ref, b_ref, o_ref):
    o_ref[...] = _dot(c_ref[...].astype(BF16), w_ref[...].astype(BF16)) + b_ref[...]


def _time_permute(v, n_chunks, to_permuted):
    sub = lax.broadcasted_iota(jnp.int32, (SUBLANES, 1), 0)
    n_half = SEG // SUBLANES
    out = [None] * (n_chunks * SEG)
    for c in range(n_chunks):
        for h in range(n_half):
            natural = [c * SEG + n_half * s + h for s in range(SUBLANES)]
            permuted = [c * SEG + SUBLANES * h + i for i in range(SUBLANES)]
            src, dst = (natural, permuted) if to_permuted else (permuted, natural)
            rows = [v[g * SUBLANES:(g + 1) * SUBLANES] for g in src]
            d = SUBLANES // 2
            while d >= 1:
                keep = jnp.bitwise_and(sub, d) == 0
                for a in range(SUBLANES):
                    if a & d == 0:
                        lo, hi = rows[a], rows[a + d]
                        rows[a] = jnp.where(keep, lo, pltpu.roll(hi, d, axis=0))
                        rows[a + d] = jnp.where(keep, pltpu.roll(lo, SUBLANES - d, axis=0), hi)
                d //= 2
            for i, g in enumerate(dst):
                out[g] = rows[i]
    return jnp.concatenate(out, axis=0)


def _permuted_conv_slab(raw, buf, tail, w_ref, b_ref, n_chunks, c0):
    cols = slice(c0, c0 + raw.shape[1])
    sub = lax.broadcasted_iota(jnp.int32, (SUBLANES, 1), 0)
    outs = []
    for c in range(n_chunks):
        buf[c, HALO:HALO + CHUNK, cols] = raw[c * CHUNK:(c + 1) * CHUNK]
        for i in range(CONV_W - 1):
            r0 = CHUNK + i * SUBLANES
            cur = buf[c, r0:r0 + SUBLANES, cols]
            if c == 0:
                prev = tail[i * SUBLANES:(i + 1) * SUBLANES, cols]
            else:
                prev = buf[c - 1, r0:r0 + SUBLANES, cols]
            buf[c, i * SUBLANES:(i + 1) * SUBLANES, cols] = pltpu.roll(
                jnp.where(sub == SUBLANES - 1, prev, cur), 1, axis=0)
        acc = b_ref[:, cols] + w_ref[0:1, cols] * buf[c, 0:CHUNK, cols]
        for k in range(1, CONV_W):
            acc = acc + w_ref[k:k + 1, cols] * buf[c, k * SUBLANES:k * SUBLANES + CHUNK, cols]
        outs.append(acc)
    tail[:, cols] = buf[n_chunks - 1, CHUNK:CHUNK + HALO, cols]
    return jnp.concatenate(outs, axis=0) if n_chunks > 1 else outs[0]


def _project_and_conv(hb, w_ref, off, width, buf, tail, cw_ref, cb_ref, n_chunks, post=None):
    n_slabs = width // CONV_SLAB
    raws, outs = [], []
    for k in range(n_slabs + 1):
        if k < n_slabs:
            raws.append(_dot(hb, _w(w_ref, off + k * CONV_SLAB, off + (k + 1) * CONV_SLAB)))
        if k >= 1:
            o = _permuted_conv_slab(raws[k - 1], buf, tail, cw_ref, cb_ref, n_chunks, (k - 1) * CONV_SLAB)
            outs.append(o if post is None else post(o))
    return jnp.concatenate(outs, axis=1)


def _permuted_scan(a, b, h0, n_chunks):
    hs = []
    h_in = h0
    for c in range(n_chunks):
        ps, hz = [], []
        for q in range(SEG):
            r0 = c * CHUNK + q * SUBLANES
            aq, bq = a[r0:r0 + SUBLANES], b[r0:r0 + SUBLANES]
            if q == 0:
                p, h = aq, bq
            else:
                p, h = aq * p, aq * h + bq
            ps.append(p)
            hz.append(h)
        carries = [h_in]
        for s in range(SUBLANES):
            carries.append(h[s:s + 1] + p[s:s + 1] * carries[-1])
        h_in = carries[SUBLANES]
        carry = jnp.concatenate(carries[:SUBLANES], axis=0)
        hs.extend(hz[q] + ps[q] * carry for q in range(SEG))
    return jnp.concatenate(hs, axis=0), h_in


def _load_weights(pairs, stage, sem):
    def dma(i):
        return pltpu.make_async_copy(pairs[i][0], stage.at[i % 2, pl.ds(0, STAGE_ROWS)], sem.at[i % 2])

    dma(0).start()
    for i in range(len(pairs)):
        if i + 1 < len(pairs):
            dma(i + 1).start()
        dma(i).wait()
        pairs[i][1][...] = stage[i % 2, 0:STAGE_ROWS, :]


def _weight_slabs(src, dst):
    rows, cols = dst.shape
    return [(src.at[pl.ds(r, STAGE_ROWS), pl.ds(c, STAGE_COLS)],
             dst.at[pl.ds(r, STAGE_ROWS), pl.ds(c, STAGE_COLS)])
            for r in range(0, rows, STAGE_ROWS) for c in range(0, cols, STAGE_COLS)]


def _prompt_kernel(x_ref, mod_ref, w_main_hbm, w_mg_hbm, wlp_hbm, wsp_hbm, wo_hbm,
                   w_dt, w_dtT, lcw, lcb, w_gate, ba, bx, lam,
                   scw, scb, dtb, dtbT, alog, alogT, dexp, normw, lng, lnb, e3,
                   y_ref, lh_ref, lc_ref, sh_ref, sc_ref,
                   w_main, w_mg, wlp, wsp, wo, stage, sem,
                   lbuf, sbuf, ltail, stail, hl, hT):
    tc = x_ref.shape[1]
    n_chunks = tc // CHUNK
    j = pl.program_id(1)
    last = pl.num_programs(1) - 1

    @pl.when(jnp.logical_and(pl.program_id(0) == 0, j == 0))
    def _():
        _load_weights(_weight_slabs(w_main_hbm, w_main) + _weight_slabs(w_mg_hbm, w_mg)
                      + _weight_slabs(wlp_hbm, wlp) + _weight_slabs(wsp_hbm, wsp)
                      + _weight_slabs(wo_hbm, wo), stage, sem)

    @pl.when(j == 0)
    def _():
        ltail[...] = jnp.zeros_like(ltail)
        stail[...] = jnp.zeros_like(stail)
        hl[...] = jnp.zeros_like(hl)
        hT[...] = jnp.zeros_like(hT)

    x = x_ref[0]
    mod = mod_ref[0]
    shift = mod[:, :D_MODEL]
    scale = mod[:, D_MODEL:2 * D_MODEL]
    gate = mod[:, 2 * D_MODEL:]
    hb = _time_permute(x * (1.0 + scale) + shift, n_chunks, True).astype(BF16)

    u = _project_and_conv(hb, w_main, OFF_LRU_X, D_LRU, lbuf, ltail, lcw, lcb, n_chunks)

    a, mult, ig = _lru_gates(u, w_gate, ba[...], bx[...], lam[...])
    iu = ig * u
    row = lax.broadcasted_iota(jnp.int32, (tc, 1), 0)
    first = jnp.logical_and(row == 0, j == 0)
    lru_z = _dot(hb, _w(w_main, OFF_LRU_Z, OFF_LRU_Z + D_LRU))
    hs, h_end = _permuted_scan(a, jnp.where(first, iu, mult * iu), hl[...], n_chunks)
    hl[...] = h_end
    y_lru = hs * _silu(lru_z)
    xbc = _project_and_conv(hb, w_main, OFF_XBC, CONV_DIM, sbuf, stail, scw, scb, n_chunks, post=_silu)
    p_lru = _dot(y_lru.astype(BF16), _w(wlp))

    xs = xbc[:, :D_SSD]
    bm = xbc[:, D_SSD:D_SSD + GROUPS * STATE]
    cm = xbc[:, D_SSD + GROUPS * STATE:]
    dt = _softplus(_dot(hb, w_dt[...]) + dtb[...])
    dtT = _softplus(_dot_nt(w_dtT[...], hb) + dtbT[...])
    a_row = -LOG2_E * jnp.exp(alog[...])
    a_col = -LOG2_E * jnp.exp(alogT[...])

    ri = lax.broadcasted_iota(jnp.int32, (CHUNK, CHUNK), 0)
    ci = lax.broadcasted_iota(jnp.int32, (CHUNK, CHUNK), 1)
    t_row = jnp.bitwise_and(ri, SUBLANES - 1) * SEG + jnp.right_shift(ri, 3)
    t_col = jnp.bitwise_and(ci, SUBLANES - 1) * SEG + jnp.right_shift(ci, 3)
    causal = t_row >= t_col
    tril = causal.astype(BF16)
    triu = (t_row <= t_col).astype(BF16)
    lane = lax.broadcasted_iota(jnp.int32, (CHUNK, LANES), 1)
    lo_half = lane < HEAD_DIM

    y_chunks = []
    late_proj = []
    for c in range(n_chunks):
        r0 = c * CHUNK
        if c == 0:
            late_proj.append(_dot(hb, _w(w_main, OFF_SSD_Z, OFF_SSD_Z + D_SSD)))
        if c == n_chunks - 1:
            late_proj.append(_dot(hb, _w(w_mg)))
        xs_c = xs[r0:r0 + CHUNK]
        xs_cb = xs_c.astype(BF16)
        bm_c = bm[r0:r0 + CHUNK].astype(BF16)
        cm_c = cm[r0:r0 + CHUNK].astype(BF16)
        dt_c = dt[r0:r0 + CHUNK]
        dtT_c = dtT[:, r0:r0 + CHUNK]
        acs = _dot_exact_rhs(tril, dt_c * a_row)
        acsT = _dot_exact_lhs(dtT_c * a_col, triu)
        e_exp = _expand_heads(jnp.exp2(acs), e3[...])
        w_exp = _expand_heads(jnp.exp2(acs[CHUNK - 1:CHUNK, :] - acs) * dt_c, e3[...])

        y_parts = []
        for g in range(GROUPS):
            cg = cm_c[:, g * STATE:(g + 1) * STATE]
            bg = bm_c[:, g * STATE:(g + 1) * STATE]
            cb = _dot_nt(cg, bg)
            for q in range(HEADS_PER_GROUP // 2):
                pair = g * (HEADS_PER_GROUP // 2) + q
                xp = xs_cb[:, pair * LANES:(pair + 1) * LANES]
                zero = jnp.zeros_like(xp)
                ms = []
                for half in range(2):
                    h = 2 * pair + half
                    seg = acs[:, h:h + 1] - acsT[h:h + 1, :]
                    m = jnp.exp2(jnp.where(causal, seg, -jnp.inf)) * cb * dtT_c[h:h + 1, :]
                    ms.append(m.astype(BF16))
                x2 = jnp.concatenate([jnp.where(lo_half, xp, zero), jnp.where(lo_half, zero, xp)], axis=0)
                y_parts.append(_dot(jnp.concatenate(ms, axis=1), x2))
        y_diag = jnp.concatenate(y_parts, axis=1)

        h_prev = hT[...]
        h_prev_b = h_prev.astype(BF16)
        xsw = (xs_c * w_exp).astype(BF16)
        off_parts, st_parts = [], []
        for g in range(GROUPS):
            sl = slice(g * GROUP_W, (g + 1) * GROUP_W)
            off_parts.append(_dot(cm_c[:, g * STATE:(g + 1) * STATE], h_prev_b[:, sl]))
            st_parts.append(_dot_tn(bm_c[:, g * STATE:(g + 1) * STATE], xsw[:, sl]))
        y_off = jnp.concatenate(off_parts, axis=1) * e_exp
        hT[...] = e_exp[CHUNK - 1:CHUNK, :] * h_prev + jnp.concatenate(st_parts, axis=1)
        y_chunks.append(y_diag + y_off + dexp[...] * xs_c)
    y = jnp.concatenate(y_chunks, axis=0) if len(y_chunks) > 1 else y_chunks[0]
    ssd_z, merge_logits = late_proj

    y_ssd = _gated_rmsnorm(y, ssd_z, normw[...])
    p_ssd = _dot(y_ssd.astype(BF16), _w(wsp))

    gm = _sigmoid(merge_logits)
    merged = gm[:, :D_MODEL] * p_lru + gm[:, D_MODEL:] * p_ssd
    out = _dot(merged.astype(BF16), _w(wo))
    y_ref[0] = _layer_norm(ALPHA * x + gate * _time_permute(out, n_chunks, False), lng[...], lnb[...])

    @pl.when(j == last)
    def _():
        for i in range(CONV_W - 1):
            r = (i + 1) * SUBLANES - 1
            lc_ref[0, 0, i:i + 1, :] = ltail[r:r + 1, :]
            sc_ref[0, 0, i:i + 1, :] = stail[r:r + 1, :]
        lh_ref[0] = hl[...]
        hfin = hT[...]
        for q in range(D_SSD // LANES):
            sh_ref[0, q * LANES:(q + 1) * LANES, :] = hfin[:, q * LANES:(q + 1) * LANES].T


def _sample_in_kernel(x_ref, mod_ref, lh0_ref, lst_ref, sst_ref, w_main, w_dt, w_mg, lcw, lcb,
                      w_gate, ba, bx, lam, scw, scb, dtb, alog, wlp, e3,
                      lh_ref, lco_ref, sco_ref, plru_ref, xbc_ref, xdt_ref, da_ref, z_ref, gm_ref):
    x = x_ref[...]
    mod = mod_ref[...]
    shift = mod[:, :D_MODEL]
    scale = mod[:, D_MODEL:2 * D_MODEL]
    hb = (x * (1.0 + scale) + shift).astype(BF16)

    lru_x = _dot(hb, _w(w_main, OFF_LRU_X, OFF_LRU_X + D_LRU))
    xbc_raw = _dot(hb, _w(w_main, OFF_XBC, OFF_XBC + CONV_DIM))
    u = lcb[...] + lcw[CONV_W - 1:CONV_W, :] * lru_x
    xbc = scb[...] + scw[CONV_W - 1:CONV_W, :] * xbc_raw
    for k in range(CONV_W - 1):
        u = u + lcw[k:k + 1, :] * lst_ref[k]
        xbc = xbc + scw[k:k + 1, :] * sst_ref[k]
    for k in range(CONV_W - 2):
        lco_ref[k] = lst_ref[k + 1]
        sco_ref[k] = sst_ref[k + 1]
    lco_ref[CONV_W - 2] = lru_x
    sco_ref[CONV_W - 2] = xbc_raw
    xbc = _silu(xbc)
    xbc_ref[...] = xbc

    a, mult, ig = _lru_gates(u, w_gate, ba[...], bx[...], lam[...])
    h_new = a * lh0_ref[...] + mult * ig * u
    lh_ref[...] = h_new
    lru_z = _dot(hb, _w(w_main, OFF_LRU_Z, OFF_LRU_Z + D_LRU))
    plru_ref[...] = _dot((h_new * _silu(lru_z)).astype(BF16), _w(wlp))

    dt = _softplus(_dot(hb, w_dt[...]) + dtb[...])
    da_ref[...] = jnp.exp(dt * (-jnp.exp(alog[...])))
    xdt_ref[...] = xbc[:, :D_SSD] * _expand_heads(dt, e3[...])
    z_ref[...] = _dot(hb, _w(w_main, OFF_SSD_Z, OFF_SSD_Z + D_SSD))
    gm_ref[...] = _sigmoid(_dot(hb, _w(w_mg)))


def _sample_state_kernel(da_ref, h0_ref, xdt_ref, b_ref, c_ref, ho_ref, y_ref):
    i = pl.program_id(0)
    bb = h0_ref.shape[0]
    n_terms = 6
    pad_rows = LANES - n_terms * bb

    def stack(terms, width):
        return jnp.concatenate(list(terms) + [jnp.zeros((pad_rows, width), F32)], axis=0)

    xh, xm, xl = _split3(xdt_ref[...])
    bh, bm, bl = _split3(b_ref[...])
    x6t = stack((xh, xh, xm, xm, xh, xl), D_SSD).T.astype(BF16)
    b6 = stack((bh, bm, bh, bm, bl, bh), GROUPS * STATE)
    ch, cmid, cl = _split3(c_ref[...])
    c3 = jnp.concatenate([ch, cmid, cl], axis=0).astype(BF16)
    krow = lax.broadcasted_iota(jnp.int32, (LANES, 1), 0)
    yrow = lax.broadcasted_iota(jnp.int32, (3 * bb, 1), 0)
    for b in range(bb):
        own = jnp.logical_and(jnp.bitwise_and(krow, bb - 1) == b, krow < n_terms * bb)
        rhs = jnp.where(own, b6, 0.0).astype(BF16)
        y_own = jnp.bitwise_and(yrow, bb - 1) == b
        y_parts = []
        for g in range(GROUPS):
            st = _dot(x6t[g * GROUP_W:(g + 1) * GROUP_W, :], rhs[:, g * STATE:(g + 1) * STATE])
            hn_parts = []
            for hh in range(HEADS_PER_GROUP):
                h = g * HEADS_PER_GROUP + hh
                rows = slice(h * HEAD_DIM, (h + 1) * HEAD_DIM)
                hn = da_ref[i * bb + b, h] * h0_ref[b, rows, :] + st[hh * HEAD_DIM:(hh + 1) * HEAD_DIM, :]
                ho_ref[b, rows, :] = hn
                hn_parts.append(hn.astype(BF16))
            hn_g = jnp.concatenate(hn_parts, axis=0)
            r = _dot_nt(c3[:, g * STATE:(g + 1) * STATE], hn_g)
            y_parts.append(jnp.sum(jnp.where(y_own, r, 0.0), axis=0, keepdims=True))
        y_ref[b:b + 1, :] = jnp.concatenate(y_parts, axis=1)


def _sample_out_kernel(x_ref, mod_ref, yraw_ref, xbc_ref, z_ref, gm_ref, plru_ref,
                       dexp, normw, wsp, wo, lng, lnb, o_ref):
    x = x_ref[...]
    gate = mod_ref[:, 2 * D_MODEL:]
    y = yraw_ref[...] + dexp[...] * xbc_ref[:, :D_SSD]
    y_ssd = _gated_rmsnorm(y, z_ref[...], normw[...])
    p_ssd = _dot(y_ssd.astype(BF16), _w(wsp))
    gm = gm_ref[...]
    merged = gm[:, :D_MODEL] * plru_ref[...] + gm[:, D_MODEL:] * p_ssd
    out = _dot(merged.astype(BF16), _w(wo))
    o_ref[...] = _layer_norm(ALPHA * x + gate * out, lng[...], lnb[...])


def _resident(shape, grid_rank):
    zeros = (0,) * len(shape)
    if grid_rank == 1:
        imap = lambda i: zeros
    else:
        imap = lambda b, j: zeros
    return pl.BlockSpec(shape, imap, pipeline_mode=pl.Buffered(1))


def _whole(shape):
    return jax.ShapeDtypeStruct(shape, F32)


def kernel(x_prompt, x_sample, state_lru_h, state_lru_conv, state_ssd_h, state_ssd_conv, c_prompt, c_sample, w_cond, b_cond, w_in, lru_conv_w, lru_conv_b, lru_wa, lru_ba, lru_wx, lru_bx, lru_lambda, ssd_conv_w, ssd_conv_b, ssd_dt_bias, ssd_a_log, ssd_d, ssd_norm_w, w_lru_proj, w_ssd_proj, w_out, ln_g, ln_b):
    nbp, seq, _ = x_prompt.shape
    nbs = x_sample.shape[0]
    tc = PROMPT_TC
    assert seq % tc == 0 and tc % CHUNK == 0 and nbs % SAMPLE_BB == 0 and SAMPLE_BB == SUBLANES

    w_in0 = w_in[0]
    w_main = w_in0[:, :N_MAIN].astype(BF16)
    w_mg = w_in0[:, N_MAIN + HEADS:].astype(BF16)
    w_dt_cols = w_in0[:, N_MAIN:N_MAIN + HEADS]
    w_dt = jnp.pad(w_dt_cols, ((0, 0), (0, LANES - HEADS))).astype(BF16)
    w_dtT = w_dt_cols.T.astype(BF16)
    nq = D_LRU // LANES
    wa = lru_wa[0].reshape(nq, 2, LRU_BLOCK_W, LRU_BLOCK_W)
    wx = lru_wx[0].reshape(nq, 2, LRU_BLOCK_W, LRU_BLOCK_W)
    zb = jnp.zeros((nq, LRU_BLOCK_W, LRU_BLOCK_W), F32)

    def blockdiag(w):
        top = jnp.concatenate([w[:, 0], zb], axis=2)
        bot = jnp.concatenate([zb, w[:, 1]], axis=2)
        return jnp.concatenate([top, bot], axis=1)

    w_gate = jnp.concatenate([blockdiag(wa), blockdiag(wx)], axis=2).astype(BF16)
    row = lambda v: v.reshape(1, -1)
    ba, bx, lam = row(lru_ba[0]), row(lru_bx[0]), row(lru_lambda[0])
    lcw, lcb = lru_conv_w[0], row(lru_conv_b[0])
    scw, scb = ssd_conv_w[0], row(ssd_conv_b[0])
    dtb = jnp.pad(row(ssd_dt_bias[0]), ((0, 0), (0, LANES - HEADS)))
    alog = jnp.pad(row(ssd_a_log[0]), ((0, 0), (0, LANES - HEADS)))
    dtbT = ssd_dt_bias[0].reshape(HEADS, 1)
    alogT = ssd_a_log[0].reshape(HEADS, 1)
    dexp = row(jnp.repeat(ssd_d[0], HEAD_DIM))
    normw = row(ssd_norm_w[0])
    wlp = w_lru_proj[0].astype(BF16)
    wsp = w_ssd_proj[0].astype(BF16)
    wo = w_out[0].astype(BF16)
    lng, lnb = row(ln_g[0]), row(ln_b[0])
    k_idx = jnp.arange(LANES)[:, None]
    c_idx = jnp.arange(D_SSD)[None, :]
    e3 = jnp.logical_and(k_idx % HEADS == c_idx // HEAD_DIM, k_idx < 3 * HEADS).astype(BF16)

    c_all = jnp.concatenate([c_prompt, c_sample], axis=0)
    mod = pl.pallas_call(
        _cond_kernel,
        out_shape=_whole((nbp + nbs, 3 * D_MODEL)),
        name="cond",
    )(c_all, w_cond[0], row(b_cond[0]))
    mod_p = mod[:nbp].reshape(nbp, 1, 3 * D_MODEL)
    mod_s = mod[nbp:]

    big = (w_main, w_mg, wlp, wsp, wo)
    weights = (w_dt, w_dtT, lcw, lcb, w_gate, ba, bx, lam, scw, scb, dtb, dtbT,
               alog, alogT, dexp, normw, lng, lnb, e3)
    n_chunks = tc // CHUNK
    y_p, lh_p, lc_p, sh_p, sc_p = pl.pallas_call(
        _prompt_kernel,
        grid=(nbp, seq // tc),
        in_specs=[pl.BlockSpec((1, tc, D_MODEL), lambda b, j: (b, j, 0)),
                  pl.BlockSpec((1, 1, 3 * D_MODEL), lambda b, j: (b, 0, 0))]
                 + [pl.BlockSpec(memory_space=pl.ANY)] * len(big)
                 + [_resident(w.shape, 2) for w in weights],
        out_specs=[pl.BlockSpec((1, tc, D_MODEL), lambda b, j: (b, j, 0)),
                   pl.BlockSpec((1, 1, D_LRU), lambda b, j: (b, 0, 0)),
                   pl.BlockSpec((1, 1, CONV_W - 1, D_LRU), lambda b, j: (0, b, 0, 0)),
                   pl.BlockSpec((1, D_SSD, STATE), lambda b, j: (b, 0, 0)),
                   pl.BlockSpec((1, 1, CONV_W - 1, CONV_DIM), lambda b, j: (0, b, 0, 0))],
        out_shape=[_whole((nbp, seq, D_MODEL)), _whole((nbp, 1, D_LRU)),
                   _whole((1, nbp, CONV_W - 1, D_LRU)), _whole((nbp, D_SSD, STATE)),
                   _whole((1, nbp, CONV_W - 1, CONV_DIM))],
        scratch_shapes=[pltpu.VMEM(w.shape, BF16) for w in big] + [
                        pltpu.VMEM((2, STAGE_ROWS + SUBLANES, STAGE_COLS), BF16),
                        pltpu.SemaphoreType.DMA((2,)),
                        pltpu.VMEM((n_chunks, HALO + CHUNK, D_LRU), F32),
                        pltpu.VMEM((n_chunks, HALO + CHUNK, CONV_DIM), F32),
                        pltpu.VMEM((HALO, D_LRU), F32),
                        pltpu.VMEM((HALO, CONV_DIM), F32),
                        pltpu.VMEM((1, D_LRU), F32),
                        pltpu.VMEM((STATE, D_SSD), F32)],
        compiler_params=pltpu.CompilerParams(
            dimension_semantics=("arbitrary", "arbitrary"), vmem_limit_bytes=VMEM_LIMIT),
        name="prompt",
    )(x_prompt, mod_p, *big, *weights)

    xs2 = x_sample.reshape(nbs, D_MODEL)
    lh_s, lco, sco, p_lru, xbc_s, xdt, da, z_s, gm_s = pl.pallas_call(
        _sample_in_kernel,
        out_shape=[_whole((nbs, D_LRU)), _whole((CONV_W - 1, nbs, D_LRU)),
                   _whole((CONV_W - 1, nbs, CONV_DIM)), _whole((nbs, D_MODEL)),
                   _whole((nbs, CONV_DIM)), _whole((nbs, D_SSD)), _whole((nbs, LANES)),
                   _whole((nbs, D_SSD)), _whole((nbs, 2 * D_MODEL))],
        compiler_params=pltpu.CompilerParams(vmem_limit_bytes=VMEM_LIMIT),
        name="sample_in",
    )(xs2, mod_s, state_lru_h[0], jnp.transpose(state_lru_conv[0], (1, 0, 2)),
      jnp.transpose(state_ssd_conv[0], (1, 0, 2)), w_main, w_dt, w_mg,
      lcw, lcb, w_gate, ba, bx, lam, scw, scb, dtb, alog, wlp, e3)

    bb = SAMPLE_BB
    n_bc = GROUPS * STATE
    sh_s, y_raw = pl.pallas_call(
        _sample_state_kernel,
        grid=(nbs // bb,),
        in_specs=[pl.BlockSpec(memory_space=pltpu.SMEM),
                  pl.BlockSpec((bb, D_SSD, STATE), lambda i: (i, 0, 0)),
                  pl.BlockSpec((bb, D_SSD), lambda i: (i, 0)),
                  pl.BlockSpec((bb, n_bc), lambda i: (i, D_SSD // n_bc)),
                  pl.BlockSpec((bb, n_bc), lambda i: (i, D_SSD // n_bc + 1))],
        out_specs=[pl.BlockSpec((bb, D_SSD, STATE), lambda i: (i, 0, 0)),
                   pl.BlockSpec((bb, D_SSD), lambda i: (i, 0))],
        out_shape=[_whole((nbs, D_SSD, STATE)), _whole((nbs, D_SSD))],
        compiler_params=pltpu.CompilerParams(
            dimension_semantics=("arbitrary",), vmem_limit_bytes=VMEM_LIMIT),
        name="sample_state",
    )(da[:, :HEADS], state_ssd_h[0].reshape(nbs, D_SSD, STATE), xdt, xbc_s, xbc_s)

    y_s = pl.pallas_call(
        _sample_out_kernel,
        out_shape=_whole((nbs, D_MODEL)),
        compiler_params=pltpu.CompilerParams(vmem_limit_bytes=VMEM_LIMIT),
        name="sample_out",
    )(xs2, mod_s, y_raw, xbc_s, z_s, gm_s, p_lru, dexp, normw, wsp, wo, lng, lnb)

    return (y_p,
            y_s.reshape(nbs, 1, D_MODEL),
            lh_p.reshape(1, nbp, D_LRU),
            lc_p,
            sh_p.reshape(1, nbp, HEADS, HEAD_DIM, STATE),
            sc_p,
            lh_s.reshape(1, nbs, D_LRU),
            jnp.transpose(lco, (1, 0, 2)).reshape(1, nbs, CONV_W - 1, D_LRU),
            sh_s.reshape(1, nbs, HEADS, HEAD_DIM, STATE),
            jnp.transpose(sco, (1, 0, 2)).reshape(1, nbs, CONV_W - 1, CONV_DIM))
```

```python
import jax
import jax.numpy as jnp
from jax import lax
from jax.experimental import pallas as pl
from jax.experimental.pallas import tpu as pltpu

F32 = jnp.float32
BF16 = jnp.bfloat16

D_MODEL = 1024
D_LRU = 1024
LRU_BLOCK_W = 64
LRU_C = 8.0
D_SSD = 2048
HEAD_DIM = 64
HEADS = 32
GROUPS = 4
HEADS_PER_GROUP = HEADS // GROUPS
STATE = 128
GROUP_W = D_SSD // GROUPS
CHUNK = 128
CONV_DIM = D_SSD + 2 * GROUPS * STATE
CONV_W = 4
LN_EPS = 1e-5
RMS_EPS = 1e-5
ALPHA = 2.0 ** 0.25
LOG2_E = 1.4426950408889634

LANES = 128
SUBLANES = 8
SEG = CHUNK // SUBLANES
HALO = (CONV_W - 1) * SUBLANES
N_MAIN = 2 * D_LRU + D_SSD + CONV_DIM
OFF_LRU_X, OFF_LRU_Z, OFF_SSD_Z, OFF_XBC = 0, D_LRU, 2 * D_LRU, 2 * D_LRU + D_SSD

PROMPT_TC = 512
SAMPLE_BB = 8
CONV_SLAB = 512
VMEM_LIMIT = 60 * 1024 * 1024


def _sigmoid(x):
    return 0.5 + 0.5 * jnp.tanh(0.5 * x)


def _silu(x):
    h = 0.5 * x
    return h + h * jnp.tanh(h)


def _softplus(x):
    return jnp.maximum(x, 0.0) + jnp.log1p(jnp.exp(-jnp.abs(x)))


def _dot(a, b):
    return jnp.dot(a, b, preferred_element_type=F32)


def _dot_nt(a, b):
    return lax.dot_general(a, b, (((1,), (1,)), ((), ())), preferred_element_type=F32)


def _dot_tn(a, b):
    return lax.dot_general(a, b, (((0,), (0,)), ((), ())), preferred_element_type=F32)


def _w(ref, c0=None, c1=None):
    return ref[...] if c0 is None else ref[:, c0:c1]


def _split3(x):
    hi = x.astype(BF16).astype(F32)
    r = x - hi
    mid = r.astype(BF16).astype(F32)
    lo = r - mid
    return hi, mid, lo


def _dot_exact_rhs(lhs_bf16, x):
    hi, mid, lo = _split3(x)
    return (_dot(lhs_bf16, hi.astype(BF16)) + _dot(lhs_bf16, mid.astype(BF16))
            + _dot(lhs_bf16, lo.astype(BF16)))


def _dot_exact_lhs(x, rhs_bf16):
    hi, mid, lo = _split3(x)
    return (_dot(hi.astype(BF16), rhs_bf16) + _dot(mid.astype(BF16), rhs_bf16)
            + _dot(lo.astype(BF16), rhs_bf16))


def _expand_heads(v, e3):
    lane = lax.broadcasted_iota(jnp.int32, v.shape, 1)
    v = jnp.where(lane < HEADS, v, 0.0)
    hi, mid, lo = _split3(v)
    packed = hi + pltpu.roll(mid, HEADS, axis=1) + pltpu.roll(lo, 2 * HEADS, axis=1)
    return _dot(packed.astype(BF16), e3)


def _lru_gates(u, w_gate_ref, ba, bx, lam):
    ub = u.astype(BF16)
    r_parts, i_parts = [], []
    for q in range(D_LRU // LANES):
        gq = _dot(ub[:, q * LANES:(q + 1) * LANES], w_gate_ref[q])
        r_parts.append(gq[:, :LANES])
        i_parts.append(gq[:, LANES:])
    r = _sigmoid(jnp.concatenate(r_parts, axis=1) + ba)
    i = _sigmoid(jnp.concatenate(i_parts, axis=1) + bx)
    log_a = (-LRU_C) * r * _softplus(-lam)
    a = jnp.exp(log_a)
    v = -jnp.tanh(log_a) * (a * a + 1.0)
    mult = jnp.where(v > 0.0, v * lax.rsqrt(v), 0.0)
    return a, mult, i


def _gated_rmsnorm(y, z, norm_w):
    g = y * _silu(z)
    parts = []
    for gi in range(GROUPS):
        gg = g[:, gi * GROUP_W:(gi + 1) * GROUP_W]
        ms = jnp.mean(gg * gg, axis=-1, keepdims=True)
        parts.append(gg * lax.rsqrt(ms + RMS_EPS))
    return jnp.concatenate(parts, axis=1) * norm_w


def _layer_norm(v, g, b):
    mu = jnp.mean(v, axis=-1, keepdims=True)
    d = v - mu
    var = jnp.mean(d * d, axis=-1, keepdims=True)
    return d * lax.rsqrt(var + LN_EPS) * g + b


def _cond_kernel(c_ref, w_ref, b_ref, o_ref):
    o_ref[...] = _dot(c_ref[...].astype(BF16), w_ref[...].astype(BF16)) + b_ref[...]


def _time_permute(v, n_chunks, to_permuted):
    sub = lax.broadcasted_iota(jnp.int32, (SUBLANES, 1), 0)
    n_half = SEG // SUBLANES
    out = [None] * (n_chunks * SEG)
    for c in range(n_chunks):
        for h in range(n_half):
            natural = [c * SEG + n_half * s + h for s in range(SUBLANES)]
            permuted = [c * SEG + SUBLANES * h + i for i in range(SUBLANES)]
            src, dst = (natural, permuted) if to_permuted else (permuted, natural)
            rows = [v[g * SUBLANES:(g + 1) * SUBLANES] for g in src]
            d = SUBLANES // 2
            while d >= 1:
                keep = jnp.bitwise_and(sub, d) == 0
                for a in range(SUBLANES):
                    if a & d == 0:
                        lo, hi = rows[a], rows[a + d]
                        rows[a] = jnp.where(keep, lo, pltpu.roll(hi, d, axis=0))
                        rows[a + d] = jnp.where(keep, pltpu.roll(lo, SUBLANES - d, axis=0), hi)
                d //= 2
            for i, g in enumerate(dst):
                out[g] = rows[i]
    return jnp.concatenate(out, axis=0)


def _permuted_conv_slab(raw, buf, tail, w_ref, b_ref, n_chunks, c0):
    cols = slice(c0, c0 + raw.shape[1])
    sub = lax.broadcasted_iota(jnp.int32, (SUBLANES, 1), 0)
    outs = []
    for c in range(n_chunks):
        buf[c, HALO:HALO + CHUNK, cols] = raw[c * CHUNK:(c + 1) * CHUNK]
        for i in range(CONV_W - 1):
            r0 = CHUNK + i * SUBLANES
            cur = buf[c, r0:r0 + SUBLANES, cols]
            if c == 0:
                prev = tail[i * SUBLANES:(i + 1) * SUBLANES, cols]
            else:
                prev = buf[c - 1, r0:r0 + SUBLANES, cols]
            buf[c, i * SUBLANES:(i + 1) * SUBLANES, cols] = pltpu.roll(
                jnp.where(sub == SUBLANES - 1, prev, cur), 1, axis=0)
        acc = b_ref[:, cols] + w_ref[0:1, cols] * buf[c, 0:CHUNK, cols]
        for k in range(1, CONV_W):
            acc = acc + w_ref[k:k + 1, cols] * buf[c, k * SUBLANES:k * SUBLANES + CHUNK, cols]
        outs.append(acc)
    tail[:, cols] = buf[n_chunks - 1, CHUNK:CHUNK + HALO, cols]
    return jnp.concatenate(outs, axis=0) if n_chunks > 1 else outs[0]


def _project_and_conv(hb, w_ref, off, width, buf, tail, cw_ref, cb_ref, n_chunks, post=None):
    n_slabs = width // CONV_SLAB
    raws, outs = [], []
    for k in range(n_slabs + 1):
        if k < n_slabs:
            raws.append(_dot(hb, _w(w_ref, off + k * CONV_SLAB, off + (k + 1) * CONV_SLAB)))
        if k >= 1:
            o = _permuted_conv_slab(raws[k - 1], buf, tail, cw_ref, cb_ref, n_chunks, (k - 1) * CONV_SLAB)
            outs.append(o if post is None else post(o))
    return jnp.concatenate(outs, axis=1)


def _permuted_scan(a, b, h0, n_chunks):
    hs = []
    h_in = h0
    for c in range(n_chunks):
        ps, hz = [], []
        for q in range(SEG):
            r0 = c * CHUNK + q * SUBLANES
            aq, bq = a[r0:r0 + SUBLANES], b[r0:r0 + SUBLANES]
            if q == 0:
                p, h = aq, bq
            else:
                p, h = aq * p, aq * h + bq
            ps.append(p)
            hz.append(h)
        carries = [h_in]
        for s in range(SUBLANES):
            carries.append(h[s:s + 1] + p[s:s + 1] * carries[-1])
        h_in = carries[SUBLANES]
        carry = jnp.concatenate(carries[:SUBLANES], axis=0)
        hs.extend(hz[q] + ps[q] * carry for q in range(SEG))
    return jnp.concatenate(hs, axis=0), h_in


def _prompt_kernel(x_ref, mod_ref, w_main, w_dt, w_dtT, w_mg, lcw, lcb, w_gate, ba, bx, lam,
                   scw, scb, dtb, dtbT, alog, alogT, dexp, normw, wlp, wsp, wo, lng, lnb, e3,
                   y_ref, lh_ref, lc_ref, sh_ref, sc_ref,
                   lbuf, sbuf, ltail, stail, hl, hT):
    tc = x_ref.shape[1]
    n_chunks = tc // CHUNK
    j = pl.program_id(1)
    last = pl.num_programs(1) - 1

    @pl.when(j == 0)
    def _():
        ltail[...] = jnp.zeros_like(ltail)
        stail[...] = jnp.zeros_like(stail)
        hl[...] = jnp.zeros_like(hl)
        hT[...] = jnp.zeros_like(hT)

    x = x_ref[0]
    mod = mod_ref[0]
    shift = mod[:, :D_MODEL]
    scale = mod[:, D_MODEL:2 * D_MODEL]
    gate = mod[:, 2 * D_MODEL:]
    hb = _time_permute(x * (1.0 + scale) + shift, n_chunks, True).astype(BF16)

    u = _project_and_conv(hb, w_main, OFF_LRU_X, D_LRU, lbuf, ltail, lcw, lcb, n_chunks)

    a, mult, ig = _lru_gates(u, w_gate, ba[...], bx[...], lam[...])
    iu = ig * u
    row = lax.broadcasted_iota(jnp.int32, (tc, 1), 0)
    first = jnp.logical_and(row == 0, j == 0)
    lru_z = _dot(hb, _w(w_main, OFF_LRU_Z, OFF_LRU_Z + D_LRU))
    hs, h_end = _permuted_scan(a, jnp.where(first, iu, mult * iu), hl[...], n_chunks)
    hl[...] = h_end
    y_lru = hs * _silu(lru_z)
    xbc = _project_and_conv(hb, w_main, OFF_XBC, CONV_DIM, sbuf, stail, scw, scb, n_chunks, post=_silu)
    p_lru = _dot(y_lru.astype(BF16), _w(wlp))

    xs = xbc[:, :D_SSD]
    bm = xbc[:, D_SSD:D_SSD + GROUPS * STATE]
    cm = xbc[:, D_SSD + GROUPS * STATE:]
    dt = _softplus(_dot(hb, w_dt[...]) + dtb[...])
    dtT = _softplus(_dot_nt(w_dtT[...], hb) + dtbT[...])
    a_row = -LOG2_E * jnp.exp(alog[...])
    a_col = -LOG2_E * jnp.exp(alogT[...])

    ri = lax.broadcasted_iota(jnp.int32, (CHUNK, CHUNK), 0)
    ci = lax.broadcasted_iota(jnp.int32, (CHUNK, CHUNK), 1)
    t_row = jnp.bitwise_and(ri, SUBLANES - 1) * SEG + jnp.right_shift(ri, 3)
    t_col = jnp.bitwise_and(ci, SUBLANES - 1) * SEG + jnp.right_shift(ci, 3)
    causal = t_row >= t_col
    tril = causal.astype(BF16)
    triu = (t_row <= t_col).astype(BF16)
    lane = lax.broadcasted_iota(jnp.int32, (CHUNK, LANES), 1)
    lo_half = lane < HEAD_DIM

    y_chunks = []
    late_proj = []
    for c in range(n_chunks):
        r0 = c * CHUNK
        if c == 0:
            late_proj.append(_dot(hb, _w(w_main, OFF_SSD_Z, OFF_SSD_Z + D_SSD)))
        if c == n_chunks - 1:
            late_proj.append(_dot(hb, _w(w_mg)))
        xs_c = xs[r0:r0 + CHUNK]
        xs_cb = xs_c.astype(BF16)
        bm_c = bm[r0:r0 + CHUNK].astype(BF16)
        cm_c = cm[r0:r0 + CHUNK].astype(BF16)
        dt_c = dt[r0:r0 + CHUNK]
        dtT_c = dtT[:, r0:r0 + CHUNK]
        acs = _dot_exact_rhs(tril, dt_c * a_row)
        acsT = _dot_exact_lhs(dtT_c * a_col, triu)
        e_exp = _expand_heads(jnp.exp2(acs), e3[...])
        w_exp = _expand_heads(jnp.exp2(acs[CHUNK - 1:CHUNK, :] - acs) * dt_c, e3[...])

        y_parts = []
        for g in range(GROUPS):
            cg = cm_c[:, g * STATE:(g + 1) * STATE]
            bg = bm_c[:, g * STATE:(g + 1) * STATE]
            cb = _dot_nt(cg, bg)
            for q in range(HEADS_PER_GROUP // 2):
                pair = g * (HEADS_PER_GROUP // 2) + q
                xp = xs_cb[:, pair * LANES:(pair + 1) * LANES]
                zero = jnp.zeros_like(xp)
                ms = []
                for half in range(2):
                    h = 2 * pair + half
                    seg = acs[:, h:h + 1] - acsT[h:h + 1, :]
                    m = jnp.exp2(jnp.where(causal, seg, -jnp.inf)) * cb * dtT_c[h:h + 1, :]
                    ms.append(m.astype(BF16))
                x2 = jnp.concatenate([jnp.where(lo_half, xp, zero), jnp.where(lo_half, zero, xp)], axis=0)
                y_parts.append(_dot(jnp.concatenate(ms, axis=1), x2))
        y_diag = jnp.concatenate(y_parts, axis=1)

        h_prev = hT[...]
        h_prev_b = h_prev.astype(BF16)
        xsw = (xs_c * w_exp).astype(BF16)
        off_parts, st_parts = [], []
        for g in range(GROUPS):
            sl = slice(g * GROUP_W, (g + 1) * GROUP_W)
            off_parts.append(_dot(cm_c[:, g * STATE:(g + 1) * STATE], h_prev_b[:, sl]))
            st_parts.append(_dot_tn(bm_c[:, g * STATE:(g + 1) * STATE], xsw[:, sl]))
        y_off = jnp.concatenate(off_parts, axis=1) * e_exp
        hT[...] = e_exp[CHUNK - 1:CHUNK, :] * h_prev + jnp.concatenate(st_parts, axis=1)
        y_chunks.append(y_diag + y_off + dexp[...] * xs_c)
    ssd_z, merge_logits = late_proj

    halves = [(0, tc)] if n_chunks == 1 else [(0, tc // 2), (tc // 2, tc)]
    y_ssd_b, p_ssd, merged_b, outs = {}, {}, {}, {}

    def norm_half(i):
        r0, r1 = halves[i]
        yh = jnp.concatenate(y_chunks[r0 // CHUNK:r1 // CHUNK], axis=0)
        y_ssd_b[i] = _gated_rmsnorm(yh, ssd_z[r0:r1], normw[...]).astype(BF16)

    def merge_half(i):
        r0, r1 = halves[i]
        gm = _sigmoid(merge_logits[r0:r1])
        merged_b[i] = (gm[:, :D_MODEL] * p_lru[r0:r1] + gm[:, D_MODEL:] * p_ssd[i]).astype(BF16)

    def finish_half(i):
        r0, r1 = halves[i]
        o = _time_permute(outs[i], (r1 - r0) // CHUNK, False)
        y_ref[0, r0:r1, :] = _layer_norm(ALPHA * x[r0:r1] + gate * o, lng[...], lnb[...])

    norm_half(0)
    for i in range(len(halves)):
        p_ssd[i] = _dot(y_ssd_b[i], _w(wsp))
        if i + 1 < len(halves):
            norm_half(i + 1)
        merge_half(i)
        outs[i] = _dot(merged_b[i], _w(wo))
        if i >= 1:
            finish_half(i - 1)
    finish_half(len(halves) - 1)

    @pl.when(j == last)
    def _():
        for i in range(CONV_W - 1):
            r = (i + 1) * SUBLANES - 1
            lc_ref[0, 0, i:i + 1, :] = ltail[r:r + 1, :]
            sc_ref[0, 0, i:i + 1, :] = stail[r:r + 1, :]
        lh_ref[0] = hl[...]
        hfin = hT[...]
        for q in range(D_SSD // LANES):
            sh_ref[0, q * LANES:(q + 1) * LANES, :] = hfin[:, q * LANES:(q + 1) * LANES].T


def _sample_in_kernel(x_ref, mod_ref, lh0_ref, lst_ref, sst_ref, w_main, w_dt, w_mg, lcw, lcb,
                      w_gate, ba, bx, lam, scw, scb, dtb, alog, wlp, e3,
                      lh_ref, lco_ref, sco_ref, plru_ref, xbc_ref, xdt_ref, da_ref, z_ref, gm_ref):
    x = x_ref[...]
    mod = mod_ref[...]
    shift = mod[:, :D_MODEL]
    scale = mod[:, D_MODEL:2 * D_MODEL]
    hb = (x * (1.0 + scale) + shift).astype(BF16)

    lru_x = _dot(hb, _w(w_main, OFF_LRU_X, OFF_LRU_X + D_LRU))
    xbc_raw = _dot(hb, _w(w_main, OFF_XBC, OFF_XBC + CONV_DIM))
    u = lcb[...] + lcw[CONV_W - 1:CONV_W, :] * lru_x
    xbc = scb[...] + scw[CONV_W - 1:CONV_W, :] * xbc_raw
    for k in range(CONV_W - 1):
        u = u + lcw[k:k + 1, :] * lst_ref[k]
        xbc = xbc + scw[k:k + 1, :] * sst_ref[k]
    for k in range(CONV_W - 2):
        lco_ref[k] = lst_ref[k + 1]
        sco_ref[k] = sst_ref[k + 1]
    lco_ref[CONV_W - 2] = lru_x
    sco_ref[CONV_W - 2] = xbc_raw
    xbc = _silu(xbc)
    xbc_ref[...] = xbc

    a, mult, ig = _lru_gates(u, w_gate, ba[...], bx[...], lam[...])
    h_new = a * lh0_ref[...] + mult * ig * u
    lh_ref[...] = h_new
    lru_z = _dot(hb, _w(w_main, OFF_LRU_Z, OFF_LRU_Z + D_LRU))
    plru_ref[...] = _dot((h_new * _silu(lru_z)).astype(BF16), _w(wlp))

    dt = _softplus(_dot(hb, w_dt[...]) + dtb[...])
    da_ref[...] = jnp.exp(dt * (-jnp.exp(alog[...])))
    xdt_ref[...] = xbc[:, :D_SSD] * _expand_heads(dt, e3[...])
    z_ref[...] = _dot(hb, _w(w_main, OFF_SSD_Z, OFF_SSD_Z + D_SSD))
    gm_ref[...] = _sigmoid(_dot(hb, _w(w_mg)))


def _sample_state_kernel(da_ref, h0_ref, xdt_ref, b_ref, c_ref, ho_ref, y_ref):
    i = pl.program_id(0)
    bb = h0_ref.shape[0]
    n_terms = 6
    pad_rows = LANES - n_terms * bb

    def stack(terms, width):
        return jnp.concatenate(list(terms) + [jnp.zeros((pad_rows, width), F32)], axis=0)

    xh, xm, xl = _split3(xdt_ref[...])
    bh, bm, bl = _split3(b_ref[...])
    x6t = stack((xh, xh, xm, xm, xh, xl), D_SSD).T.astype(BF16)
    b6 = stack((bh, bm, bh, bm, bl, bh), GROUPS * STATE)
    ch, cmid, cl = _split3(c_ref[...])
    c3 = jnp.concatenate([ch, cmid, cl], axis=0).astype(BF16)
    krow = lax.broadcasted_iota(jnp.int32, (LANES, 1), 0)
    yrow = lax.broadcasted_iota(jnp.int32, (3 * bb, 1), 0)
    for b in range(bb):
        own = jnp.logical_and(jnp.bitwise_and(krow, bb - 1) == b, krow < n_terms * bb)
        rhs = jnp.where(own, b6, 0.0).astype(BF16)
        y_own = jnp.bitwise_and(yrow, bb - 1) == b
        y_parts = []
        for g in range(GROUPS):
            st = _dot(x6t[g * GROUP_W:(g + 1) * GROUP_W, :], rhs[:, g * STATE:(g + 1) * STATE])
            hn_parts = []
            for hh in range(HEADS_PER_GROUP):
                h = g * HEADS_PER_GROUP + hh
                rows = slice(h * HEAD_DIM, (h + 1) * HEAD_DIM)
                hn = da_ref[i * bb + b, h] * h0_ref[b, rows, :] + st[hh * HEAD_DIM:(hh + 1) * HEAD_DIM, :]
                ho_ref[b, rows, :] = hn
                hn_parts.append(hn.astype(BF16))
            hn_g = jnp.concatenate(hn_parts, axis=0)
            r = _dot_nt(c3[:, g * STATE:(g + 1) * STATE], hn_g)
            y_parts.append(jnp.sum(jnp.where(y_own, r, 0.0), axis=0, keepdims=True))
        y_ref[b:b + 1, :] = jnp.concatenate(y_parts, axis=1)


def _sample_out_kernel(x_ref, mod_ref, yraw_ref, xbc_ref, z_ref, gm_ref, plru_ref,
                       dexp, normw, wsp, wo, lng, lnb, o_ref):
    x = x_ref[...]
    gate = mod_ref[:, 2 * D_MODEL:]
    y = yraw_ref[...] + dexp[...] * xbc_ref[:, :D_SSD]
    y_ssd = _gated_rmsnorm(y, z_ref[...], normw[...])
    p_ssd = _dot(y_ssd.astype(BF16), _w(wsp))
    gm = gm_ref[...]
    merged = gm[:, :D_MODEL] * plru_ref[...] + gm[:, D_MODEL:] * p_ssd
    out = _dot(merged.astype(BF16), _w(wo))
    o_ref[...] = _layer_norm(ALPHA * x + gate * out, lng[...], lnb[...])


def _resident(shape, grid_rank):
    zeros = (0,) * len(shape)
    if grid_rank == 1:
        imap = lambda i: zeros
    else:
        imap = lambda b, j: zeros
    return pl.BlockSpec(shape, imap, pipeline_mode=pl.Buffered(1))


def _whole(shape):
    return jax.ShapeDtypeStruct(shape, F32)


def kernel(x_prompt, x_sample, state_lru_h, state_lru_conv, state_ssd_h, state_ssd_conv, c_prompt, c_sample, w_cond, b_cond, w_in, lru_conv_w, lru_conv_b, lru_wa, lru_ba, lru_wx, lru_bx, lru_lambda, ssd_conv_w, ssd_conv_b, ssd_dt_bias, ssd_a_log, ssd_d, ssd_norm_w, w_lru_proj, w_ssd_proj, w_out, ln_g, ln_b):
    nbp, seq, _ = x_prompt.shape
    nbs = x_sample.shape[0]
    tc = PROMPT_TC
    assert seq % tc == 0 and tc % CHUNK == 0 and nbs % SAMPLE_BB == 0 and SAMPLE_BB == SUBLANES

    w_in0 = w_in[0]
    w_main = w_in0[:, :N_MAIN].astype(BF16)
    w_mg = w_in0[:, N_MAIN + HEADS:].astype(BF16)
    w_dt_cols = w_in0[:, N_MAIN:N_MAIN + HEADS]
    w_dt = jnp.pad(w_dt_cols, ((0, 0), (0, LANES - HEADS))).astype(BF16)
    w_dtT = w_dt_cols.T.astype(BF16)
    nq = D_LRU // LANES
    wa = lru_wa[0].reshape(nq, 2, LRU_BLOCK_W, LRU_BLOCK_W)
    wx = lru_wx[0].reshape(nq, 2, LRU_BLOCK_W, LRU_BLOCK_W)
    zb = jnp.zeros((nq, LRU_BLOCK_W, LRU_BLOCK_W), F32)

    def blockdiag(w):
        top = jnp.concatenate([w[:, 0], zb], axis=2)
        bot = jnp.concatenate([zb, w[:, 1]], axis=2)
        return jnp.concatenate([top, bot], axis=1)

    w_gate = jnp.concatenate([blockdiag(wa), blockdiag(wx)], axis=2).astype(BF16)
    row = lambda v: v.reshape(1, -1)
    ba, bx, lam = row(lru_ba[0]), row(lru_bx[0]), row(lru_lambda[0])
    lcw, lcb = lru_conv_w[0], row(lru_conv_b[0])
    scw, scb = ssd_conv_w[0], row(ssd_conv_b[0])
    dtb = jnp.pad(row(ssd_dt_bias[0]), ((0, 0), (0, LANES - HEADS)))
    alog = jnp.pad(row(ssd_a_log[0]), ((0, 0), (0, LANES - HEADS)))
    dtbT = ssd_dt_bias[0].reshape(HEADS, 1)
    alogT = ssd_a_log[0].reshape(HEADS, 1)
    dexp = row(jnp.repeat(ssd_d[0], HEAD_DIM))
    normw = row(ssd_norm_w[0])
    wlp = w_lru_proj[0].astype(BF16)
    wsp = w_ssd_proj[0].astype(BF16)
    wo = w_out[0].astype(BF16)
    lng, lnb = row(ln_g[0]), row(ln_b[0])
    k_idx = jnp.arange(LANES)[:, None]
    c_idx = jnp.arange(D_SSD)[None, :]
    e3 = jnp.logical_and(k_idx % HEADS == c_idx // HEAD_DIM, k_idx < 3 * HEADS).astype(BF16)

    c_all = jnp.concatenate([c_prompt, c_sample], axis=0)
    mod = pl.pallas_call(
        _cond_kernel,
        out_shape=_whole((nbp + nbs, 3 * D_MODEL)),
        name="cond",
    )(c_all, w_cond[0], row(b_cond[0]))
    mod_p = mod[:nbp].reshape(nbp, 1, 3 * D_MODEL)
    mod_s = mod[nbp:]

    weights = (w_main, w_dt, w_dtT, w_mg, lcw, lcb, w_gate, ba, bx, lam, scw, scb, dtb, dtbT,
               alog, alogT, dexp, normw, wlp, wsp, wo, lng, lnb, e3)
    n_chunks = tc // CHUNK
    y_p, lh_p, lc_p, sh_p, sc_p = pl.pallas_call(
        _prompt_kernel,
        grid=(nbp, seq // tc),
        in_specs=[pl.BlockSpec((1, tc, D_MODEL), lambda b, j: (b, j, 0)),
                  pl.BlockSpec((1, 1, 3 * D_MODEL), lambda b, j: (b, 0, 0))]
                 + [_resident(w.shape, 2) for w in weights],
        out_specs=[pl.BlockSpec((1, tc, D_MODEL), lambda b, j: (b, j, 0)),
                   pl.BlockSpec((1, 1, D_LRU), lambda b, j: (b, 0, 0)),
                   pl.BlockSpec((1, 1, CONV_W - 1, D_LRU), lambda b, j: (0, b, 0, 0)),
                   pl.BlockSpec((1, D_SSD, STATE), lambda b, j: (b, 0, 0)),
                   pl.BlockSpec((1, 1, CONV_W - 1, CONV_DIM), lambda b, j: (0, b, 0, 0))],
        out_shape=[_whole((nbp, seq, D_MODEL)), _whole((nbp, 1, D_LRU)),
                   _whole((1, nbp, CONV_W - 1, D_LRU)), _whole((nbp, D_SSD, STATE)),
                   _whole((1, nbp, CONV_W - 1, CONV_DIM))],
        scratch_shapes=[pltpu.VMEM((n_chunks, HALO + CHUNK, D_LRU), F32),
                        pltpu.VMEM((n_chunks, HALO + CHUNK, CONV_DIM), F32),
                        pltpu.VMEM((HALO, D_LRU), F32),
                        pltpu.VMEM((HALO, CONV_DIM), F32),
                        pltpu.VMEM((1, D_LRU), F32),
                        pltpu.VMEM((STATE, D_SSD), F32)],
        compiler_params=pltpu.CompilerParams(
            dimension_semantics=("arbitrary", "arbitrary"), vmem_limit_bytes=VMEM_LIMIT),
        name="prompt",
    )(x_prompt, mod_p, *weights)

    xs2 = x_sample.reshape(nbs, D_MODEL)
    lh_s, lco, sco, p_lru, xbc_s, xdt, da, z_s, gm_s = pl.pallas_call(
        _sample_in_kernel,
        out_shape=[_whole((nbs, D_LRU)), _whole((CONV_W - 1, nbs, D_LRU)),
                   _whole((CONV_W - 1, nbs, CONV_DIM)), _whole((nbs, D_MODEL)),
                   _whole((nbs, CONV_DIM)), _whole((nbs, D_SSD)), _whole((nbs, LANES)),
                   _whole((nbs, D_SSD)), _whole((nbs, 2 * D_MODEL))],
        compiler_params=pltpu.CompilerParams(vmem_limit_bytes=VMEM_LIMIT),
        name="sample_in",
    )(xs2, mod_s, state_lru_h[0], jnp.transpose(state_lru_conv[0], (1, 0, 2)),
      jnp.transpose(state_ssd_conv[0], (1, 0, 2)), w_main, w_dt, w_mg,
      lcw, lcb, w_gate, ba, bx, lam, scw, scb, dtb, alog, wlp, e3)

    bb = SAMPLE_BB
    n_bc = GROUPS * STATE
    sh_s, y_raw = pl.pallas_call(
        _sample_state_kernel,
        grid=(nbs // bb,),
        in_specs=[pl.BlockSpec(memory_space=pltpu.SMEM),
                  pl.BlockSpec((bb, D_SSD, STATE), lambda i: (i, 0, 0)),
                  pl.BlockSpec((bb, D_SSD), lambda i: (i, 0)),
                  pl.BlockSpec((bb, n_bc), lambda i: (i, D_SSD // n_bc)),
                  pl.BlockSpec((bb, n_bc), lambda i: (i, D_SSD // n_bc + 1))],
        out_specs=[pl.BlockSpec((bb, D_SSD, STATE), lambda i: (i, 0, 0)),
                   pl.BlockSpec((bb, D_SSD), lambda i: (i, 0))],
        out_shape=[_whole((nbs, D_SSD, STATE)), _whole((nbs, D_SSD))],
        compiler_params=pltpu.CompilerParams(
            dimension_semantics=("arbitrary",), vmem_limit_bytes=VMEM_LIMIT),
        name="sample_state",
    )(da[:, :HEADS], state_ssd_h[0].reshape(nbs, D_SSD, STATE), xdt, xbc_s, xbc_s)

    y_s = pl.pallas_call(
        _sample_out_kernel,
        out_shape=_whole((nbs, D_MODEL)),
        compiler_params=pltpu.CompilerParams(vmem_limit_bytes=VMEM_LIMIT),
        name="sample_out",
    )(xs2, mod_s, y_raw, xbc_s, z_s, gm_s, p_lru, dexp, normw, wsp, wo, lng, lnb)

    return (y_p,
            y_s.reshape(nbs, 1, D_MODEL),
            lh_p.reshape(1, nbp, D_LRU),
            lc_p,
            sh_p.reshape(1, nbp, HEADS, HEAD_DIM, STATE),
            sc_p,
            lh_s.reshape(1, nbs, D_LRU),
            jnp.transpose(lco, (1, 0, 2)).reshape(1, nbs, CONV_W - 1, D_LRU),
            sh_s.reshape(1, nbs, HEADS, HEAD_DIM, STATE),
            jnp.transpose(sco, (1, 0, 2)).reshape(1, nbs, CONV_W - 1, CONV_DIM))
```

```python
import jax
import jax.numpy as jnp
from jax import lax
from jax.experimental import pallas as pl
from jax.experimental.pallas import tpu as pltpu

F32 = jnp.float32
BF16 = jnp.bfloat16

D_MODEL = 1024
D_LRU = 1024
LRU_BLOCK_W = 64
LRU_C = 8.0
D_SSD = 2048
HEAD_DIM = 64
HEADS = 32
GROUPS = 4
HEADS_PER_GROUP = HEADS // GROUPS
STATE = 128
GROUP_W = D_SSD // GROUPS
CHUNK = 128
CONV_DIM = D_SSD + 2 * GROUPS * STATE
CONV_W = 4
LN_EPS = 1e-5
RMS_EPS = 1e-5
ALPHA = 2.0 ** 0.25
LOG2_E = 1.4426950408889634

LANES = 128
SUBLANES = 8
SEG = CHUNK // SUBLANES
HALO = (CONV_W - 1) * SUBLANES
N_MAIN = 2 * D_LRU + D_SSD + CONV_DIM
OFF_LRU_X, OFF_LRU_Z, OFF_SSD_Z, OFF_XBC = 0, D_LRU, 2 * D_LRU, 2 * D_LRU + D_SSD

PROMPT_TC = 512
SAMPLE_BB = 8
CONV_SLAB = 512
VMEM_LIMIT = 60 * 1024 * 1024


def _sigmoid(x):
    return 0.5 + 0.5 * jnp.tanh(0.5 * x)


def _silu(x):
    h = 0.5 * x
    return h + h * jnp.tanh(h)


def _softplus(x):
    return jnp.maximum(x, 0.0) + jnp.log1p(jnp.exp(-jnp.abs(x)))


def _dot(a, b):
    return jnp.dot(a, b, preferred_element_type=F32)


def _dot_nt(a, b):
    return lax.dot_general(a, b, (((1,), (1,)), ((), ())), preferred_element_type=F32)


def _dot_tn(a, b):
    return lax.dot_general(a, b, (((0,), (0,)), ((), ())), preferred_element_type=F32)


def _w(ref, c0=None, c1=None):
    return ref[...] if c0 is None else ref[:, c0:c1]


def _split3(x):
    hi = x.astype(BF16).astype(F32)
    r = x - hi
    mid = r.astype(BF16).astype(F32)
    lo = r - mid
    return hi, mid, lo


def _dot_exact_rhs(lhs_bf16, x):
    hi, mid, lo = _split3(x)
    return (_dot(lhs_bf16, hi.astype(BF16)) + _dot(lhs_bf16, mid.astype(BF16))
            + _dot(lhs_bf16, lo.astype(BF16)))


def _dot_exact_lhs(x, rhs_bf16):
    hi, mid, lo = _split3(x)
    return (_dot(hi.astype(BF16), rhs_bf16) + _dot(mid.astype(BF16), rhs_bf16)
            + _dot(lo.astype(BF16), rhs_bf16))


def _expand_heads(v, e3):
    lane = lax.broadcasted_iota(jnp.int32, v.shape, 1)
    v = jnp.where(lane < HEADS, v, 0.0)
    hi, mid, lo = _split3(v)
    packed = hi + pltpu.roll(mid, HEADS, axis=1) + pltpu.roll(lo, 2 * HEADS, axis=1)
    return _dot(packed.astype(BF16), e3)


def _lru_gates(u, w_gate_ref, ba, bx, lam):
    ub = u.astype(BF16)
    r_parts, i_parts = [], []
    for q in range(D_LRU // LANES):
        gq = _dot(ub[:, q * LANES:(q + 1) * LANES], w_gate_ref[q])
        r_parts.append(gq[:, :LANES])
        i_parts.append(gq[:, LANES:])
    r = _sigmoid(jnp.concatenate(r_parts, axis=1) + ba)
    i = _sigmoid(jnp.concatenate(i_parts, axis=1) + bx)
    log_a = (-LRU_C) * r * _softplus(-lam)
    a = jnp.exp(log_a)
    v = -jnp.tanh(log_a) * (a * a + 1.0)
    mult = jnp.where(v > 0.0, v * lax.rsqrt(v), 0.0)
    return a, mult, i


def _gated_rmsnorm(y, z, norm_w):
    g = y * _silu(z)
    parts = []
    for gi in range(GROUPS):
        gg = g[:, gi * GROUP_W:(gi + 1) * GROUP_W]
        ms = jnp.mean(gg * gg, axis=-1, keepdims=True)
        parts.append(gg * lax.rsqrt(ms + RMS_EPS))
    return jnp.concatenate(parts, axis=1) * norm_w


def _layer_norm(v, g, b):
    mu = jnp.mean(v, axis=-1, keepdims=True)
    d = v - mu
    var = jnp.mean(d * d, axis=-1, keepdims=True)
    return d * lax.rsqrt(var + LN_EPS) * g + b


def _cond_kernel(c_ref, w_ref, b_ref, o_ref):
    o_ref[...] = _dot(c_ref[...].astype(BF16), w_ref[...].astype(BF16)) + b_ref[...]


def _time_permute(v, n_chunks, to_permuted):
    sub = lax.broadcasted_iota(jnp.int32, (SUBLANES, 1), 0)
    n_half = SEG // SUBLANES
    out = [None] * (n_chunks * SEG)
    for c in range(n_chunks):
        for h in range(n_half):
            natural = [c * SEG + n_half * s + h for s in range(SUBLANES)]
            permuted = [c * SEG + SUBLANES * h + i for i in range(SUBLANES)]
            src, dst = (natural, permuted) if to_permuted else (permuted, natural)
            rows = [v[g * SUBLANES:(g + 1) * SUBLANES] for g in src]
            d = SUBLANES // 2
            while d >= 1:
                keep = jnp.bitwise_and(sub, d) == 0
                for a in range(SUBLANES):
                    if a & d == 0:
                        lo, hi = rows[a], rows[a + d]
                        rows[a] = jnp.where(keep, lo, pltpu.roll(hi, d, axis=0))
                        rows[a + d] = jnp.where(keep, pltpu.roll(lo, SUBLANES - d, axis=0), hi)
                d //= 2
            for i, g in enumerate(dst):
                out[g] = rows[i]
    return jnp.concatenate(out, axis=0)


def _permuted_conv_slab(raw, buf, tail, w_ref, b_ref, n_chunks, c0):
    cols = slice(c0, c0 + raw.shape[1])
    sub = lax.broadcasted_iota(jnp.int32, (SUBLANES, 1), 0)
    outs = []
    for c in range(n_chunks):
        buf[c, HALO:HALO + CHUNK, cols] = raw[c * CHUNK:(c + 1) * CHUNK]
        for i in range(CONV_W - 1):
            r0 = CHUNK + i * SUBLANES
            cur = buf[c, r0:r0 + SUBLANES, cols]
            if c == 0:
                prev = tail[i * SUBLANES:(i + 1) * SUBLANES, cols]
            else:
                prev = buf[c - 1, r0:r0 + SUBLANES, cols]
            buf[c, i * SUBLANES:(i + 1) * SUBLANES, cols] = pltpu.roll(
                jnp.where(sub == SUBLANES - 1, prev, cur), 1, axis=0)
        acc = b_ref[:, cols] + w_ref[0:1, cols] * buf[c, 0:CHUNK, cols]
        for k in range(1, CONV_W):
            acc = acc + w_ref[k:k + 1, cols] * buf[c, k * SUBLANES:k * SUBLANES + CHUNK, cols]
        outs.append(acc)
    tail[:, cols] = buf[n_chunks - 1, CHUNK:CHUNK + HALO, cols]
    return jnp.concatenate(outs, axis=0) if n_chunks > 1 else outs[0]


def _project_and_conv(hb, w_ref, off, width, buf, tail, cw_ref, cb_ref, n_chunks, post=None):
    n_slabs = width // CONV_SLAB
    raws, outs = [], []
    for k in range(n_slabs + 1):
        if k < n_slabs:
            raws.append(_dot(hb, _w(w_ref, off + k * CONV_SLAB, off + (k + 1) * CONV_SLAB)))
        if k >= 1:
            o = _permuted_conv_slab(raws[k - 1], buf, tail, cw_ref, cb_ref, n_chunks, (k - 1) * CONV_SLAB)
            outs.append(o if post is None else post(o))
    return jnp.concatenate(outs, axis=1)


def _permuted_scan(a, b, h0, n_chunks):
    hs = []
    h_in = h0
    for c in range(n_chunks):
        ps, hz = [], []
        for q in range(SEG):
            r0 = c * CHUNK + q * SUBLANES
            aq, bq = a[r0:r0 + SUBLANES], b[r0:r0 + SUBLANES]
            if q == 0:
                p, h = aq, bq
            else:
                p, h = aq * p, aq * h + bq
            ps.append(p)
            hz.append(h)
        carries = [h_in]
        for s in range(SUBLANES):
            carries.append(h[s:s + 1] + p[s:s + 1] * carries[-1])
        h_in = carries[SUBLANES]
        carry = jnp.concatenate(carries[:SUBLANES], axis=0)
        hs.extend(hz[q] + ps[q] * carry for q in range(SEG))
    return jnp.concatenate(hs, axis=0), h_in


def _prompt_kernel(x_ref, mod_ref, w_main, w_dt, w_dtT, w_mg, lcw, lcb, w_gate, ba, bx, lam,
                   scw, scb, dtb, dtbT, alog, alogT, dexp, normw, wlp, wsp, wo, lng, lnb, e3,
                   y_ref, lh_ref, lc_ref, sh_ref, sc_ref,
                   lbuf, sbuf, ltail, stail, hl, hT):
    tc = x_ref.shape[1]
    n_chunks = tc // CHUNK
    j = pl.program_id(1)
    last = pl.num_programs(1) - 1

    @pl.when(j == 0)
    def _():
        ltail[...] = jnp.zeros_like(ltail)
        stail[...] = jnp.zeros_like(stail)
        hl[...] = jnp.zeros_like(hl)
        hT[...] = jnp.zeros_like(hT)

    x = x_ref[0]
    mod = mod_ref[0]
    shift = mod[:, :D_MODEL]
    scale = mod[:, D_MODEL:2 * D_MODEL]
    gate = mod[:, 2 * D_MODEL:]
    hb = _time_permute(x * (1.0 + scale) + shift, n_chunks, True).astype(BF16)

    u = _project_and_conv(hb, w_main, OFF_LRU_X, D_LRU, lbuf, ltail, lcw, lcb, n_chunks)

    a, mult, ig = _lru_gates(u, w_gate, ba[...], bx[...], lam[...])
    iu = ig * u
    row = lax.broadcasted_iota(jnp.int32, (tc, 1), 0)
    first = jnp.logical_and(row == 0, j == 0)
    lru_z = _dot(hb, _w(w_main, OFF_LRU_Z, OFF_LRU_Z + D_LRU))
    hs, h_end = _permuted_scan(a, jnp.where(first, iu, mult * iu), hl[...], n_chunks)
    hl[...] = h_end
    y_lru = hs * _silu(lru_z)
    xbc = _project_and_conv(hb, w_main, OFF_XBC, CONV_DIM, sbuf, stail, scw, scb, n_chunks, post=_silu)
    p_lru = _dot(y_lru.astype(BF16), _w(wlp))

    xs = xbc[:, :D_SSD]
    bm = xbc[:, D_SSD:D_SSD + GROUPS * STATE]
    cm = xbc[:, D_SSD + GROUPS * STATE:]
    dt = _softplus(_dot(hb, w_dt[...]) + dtb[...])
    dtT = _softplus(_dot_nt(w_dtT[...], hb) + dtbT[...])
    a_row = -LOG2_E * jnp.exp(alog[...])
    a_col = -LOG2_E * jnp.exp(alogT[...])

    ri = lax.broadcasted_iota(jnp.int32, (CHUNK, CHUNK), 0)
    ci = lax.broadcasted_iota(jnp.int32, (CHUNK, CHUNK), 1)
    t_row = jnp.bitwise_and(ri, SUBLANES - 1) * SEG + jnp.right_shift(ri, 3)
    t_col = jnp.bitwise_and(ci, SUBLANES - 1) * SEG + jnp.right_shift(ci, 3)
    causal = t_row >= t_col
    tril = causal.astype(BF16)
    triu = (t_row <= t_col).astype(BF16)
    lane = lax.broadcasted_iota(jnp.int32, (CHUNK, LANES), 1)
    lo_half = lane < HEAD_DIM

    def chunk_prologue(c):
        r0 = c * CHUNK
        xs_c = xs[r0:r0 + CHUNK]
        dt_c = dt[r0:r0 + CHUNK]
        dtT_c = dtT[:, r0:r0 + CHUNK]
        acs = _dot_exact_rhs(tril, dt_c * a_row)
        acsT = _dot_exact_lhs(dtT_c * a_col, triu)
        e_exp = _expand_heads(jnp.exp2(acs), e3[...])
        w_exp = _expand_heads(jnp.exp2(acs[CHUNK - 1:CHUNK, :] - acs) * dt_c, e3[...])
        return dict(xs=xs_c, xs_b=xs_c.astype(BF16), bm=bm[r0:r0 + CHUNK].astype(BF16),
                    cm=cm[r0:r0 + CHUNK].astype(BF16), dtT=dtT_c, acs=acs, acsT=acsT,
                    e_exp=e_exp, xsw=(xs_c * w_exp).astype(BF16))

    def group_cb(p, g):
        return _dot_nt(p["cm"][:, g * STATE:(g + 1) * STATE], p["bm"][:, g * STATE:(g + 1) * STATE])

    y_chunks = []
    late_proj = []
    nxt = chunk_prologue(0)
    for c in range(n_chunks):
        p = nxt
        if c == 0:
            late_proj.append(_dot(hb, _w(w_main, OFF_SSD_Z, OFF_SSD_Z + D_SSD)))
        if c == n_chunks - 1:
            late_proj.append(_dot(hb, _w(w_mg)))
        if c + 1 < n_chunks:
            nxt = chunk_prologue(c + 1)
        acs, acsT, dtT_c, xs_cb = p["acs"], p["acsT"], p["dtT"], p["xs_b"]

        y_parts = []
        cb_next = group_cb(p, 0)
        for g in range(GROUPS):
            cb = cb_next
            if g + 1 < GROUPS:
                cb_next = group_cb(p, g + 1)
            for q in range(HEADS_PER_GROUP // 2):
                pair = g * (HEADS_PER_GROUP // 2) + q
                xp = xs_cb[:, pair * LANES:(pair + 1) * LANES]
                zero = jnp.zeros_like(xp)
                ms = []
                for half in range(2):
                    h = 2 * pair + half
                    seg = acs[:, h:h + 1] - acsT[h:h + 1, :]
                    m = jnp.exp2(jnp.where(causal, seg, -jnp.inf)) * cb * dtT_c[h:h + 1, :]
                    ms.append(m.astype(BF16))
                x2 = jnp.concatenate([jnp.where(lo_half, xp, zero), jnp.where(lo_half, zero, xp)], axis=0)
                y_parts.append(_dot(jnp.concatenate(ms, axis=1), x2))
        y_diag = jnp.concatenate(y_parts, axis=1)

        h_prev = hT[...]
        h_prev_b = h_prev.astype(BF16)
        off_parts, st_parts = [], []
        for g in range(GROUPS):
            sl = slice(g * GROUP_W, (g + 1) * GROUP_W)
            off_parts.append(_dot(p["cm"][:, g * STATE:(g + 1) * STATE], h_prev_b[:, sl]))
            st_parts.append(_dot_tn(p["bm"][:, g * STATE:(g + 1) * STATE], p["xsw"][:, sl]))
        y_off = jnp.concatenate(off_parts, axis=1) * p["e_exp"]
        hT[...] = p["e_exp"][CHUNK - 1:CHUNK, :] * h_prev + jnp.concatenate(st_parts, axis=1)
        y_chunks.append(y_diag + y_off + dexp[...] * p["xs"])
    ssd_z, merge_logits = late_proj

    halves = [(0, tc)] if n_chunks == 1 else [(0, tc // 2), (tc // 2, tc)]
    y_ssd_b, p_ssd, merged_b, outs = {}, {}, {}, {}

    def norm_half(i):
        r0, r1 = halves[i]
        yh = jnp.concatenate(y_chunks[r0 // CHUNK:r1 // CHUNK], axis=0)
        y_ssd_b[i] = _gated_rmsnorm(yh, ssd_z[r0:r1], normw[...]).astype(BF16)

    def merge_half(i):
        r0, r1 = halves[i]
        gm = _sigmoid(merge_logits[r0:r1])
        merged_b[i] = (gm[:, :D_MODEL] * p_lru[r0:r1] + gm[:, D_MODEL:] * p_ssd[i]).astype(BF16)

    def finish_half(i):
        r0, r1 = halves[i]
        o = _time_permute(outs[i], (r1 - r0) // CHUNK, False)
        y_ref[0, r0:r1, :] = _layer_norm(ALPHA * x[r0:r1] + gate * o, lng[...], lnb[...])

    norm_half(0)
    for i in range(len(halves)):
        p_ssd[i] = _dot(y_ssd_b[i], _w(wsp))
        if i + 1 < len(halves):
            norm_half(i + 1)
        merge_half(i)
        outs[i] = _dot(merged_b[i], _w(wo))
        if i >= 1:
            finish_half(i - 1)
    finish_half(len(halves) - 1)

    @pl.when(j == last)
    def _():
        for i in range(CONV_W - 1):
            r = (i + 1) * SUBLANES - 1
            lc_ref[0, 0, i:i + 1, :] = ltail[r:r + 1, :]
            sc_ref[0, 0, i:i + 1, :] = stail[r:r + 1, :]
        lh_ref[0] = hl[...]
        hfin = hT[...]
        for q in range(D_SSD // LANES):
            sh_ref[0, q * LANES:(q + 1) * LANES, :] = hfin[:, q * LANES:(q + 1) * LANES].T


def _sample_in_kernel(x_ref, mod_ref, lh0_ref, lst_ref, sst_ref, w_main, w_dt, w_mg, lcw, lcb,
                      w_gate, ba, bx, lam, scw, scb, dtb, alog, wlp, e3,
                      lh_ref, lco_ref, sco_ref, plru_ref, xbc_ref, xdt_ref, da_ref, z_ref, gm_ref):
    x = x_ref[...]
    mod = mod_ref[...]
    shift = mod[:, :D_MODEL]
    scale = mod[:, D_MODEL:2 * D_MODEL]
    hb = (x * (1.0 + scale) + shift).astype(BF16)

    lru_x = _dot(hb, _w(w_main, OFF_LRU_X, OFF_LRU_X + D_LRU))
    xbc_raw = _dot(hb, _w(w_main, OFF_XBC, OFF_XBC + CONV_DIM))
    u = lcb[...] + lcw[CONV_W - 1:CONV_W, :] * lru_x
    xbc = scb[...] + scw[CONV_W - 1:CONV_W, :] * xbc_raw
    for k in range(CONV_W - 1):
        u = u + lcw[k:k + 1, :] * lst_ref[k]
        xbc = xbc + scw[k:k + 1, :] * sst_ref[k]
    for k in range(CONV_W - 2):
        lco_ref[k] = lst_ref[k + 1]
        sco_ref[k] = sst_ref[k + 1]
    lco_ref[CONV_W - 2] = lru_x
    sco_ref[CONV_W - 2] = xbc_raw
    xbc = _silu(xbc)
    xbc_ref[...] = xbc

    a, mult, ig = _lru_gates(u, w_gate, ba[...], bx[...], lam[...])
    h_new = a * lh0_ref[...] + mult * ig * u
    lh_ref[...] = h_new
    lru_z = _dot(hb, _w(w_main, OFF_LRU_Z, OFF_LRU_Z + D_LRU))
    plru_ref[...] = _dot((h_new * _silu(lru_z)).astype(BF16), _w(wlp))

    dt = _softplus(_dot(hb, w_dt[...]) + dtb[...])
    da_ref[...] = jnp.exp(dt * (-jnp.exp(alog[...])))
    xdt_ref[...] = xbc[:, :D_SSD] * _expand_heads(dt, e3[...])
    z_ref[...] = _dot(hb, _w(w_main, OFF_SSD_Z, OFF_SSD_Z + D_SSD))
    gm_ref[...] = _sigmoid(_dot(hb, _w(w_mg)))


def _sample_state_kernel(da_ref, h0_ref, xdt_ref, b_ref, c_ref, ho_ref, y_ref):
    i = pl.program_id(0)
    bb = h0_ref.shape[0]
    n_terms = 6
    pad_rows = LANES - n_terms * bb

    def stack(terms, width):
        return jnp.concatenate(list(terms) + [jnp.zeros((pad_rows, width), F32)], axis=0)

    xh, xm, xl = _split3(xdt_ref[...])
    bh, bm, bl = _split3(b_ref[...])
    x6t = stack((xh, xh, xm, xm, xh, xl), D_SSD).T.astype(BF16)
    b6 = stack((bh, bm, bh, bm, bl, bh), GROUPS * STATE)
    ch, cmid, cl = _split3(c_ref[...])
    c3 = jnp.concatenate([ch, cmid, cl], axis=0).astype(BF16)
    krow = lax.broadcasted_iota(jnp.int32, (LANES, 1), 0)
    yrow = lax.broadcasted_iota(jnp.int32, (3 * bb, 1), 0)
    for b in range(bb):
        own = jnp.logical_and(jnp.bitwise_and(krow, bb - 1) == b, krow < n_terms * bb)
        rhs = jnp.where(own, b6, 0.0).astype(BF16)
        y_own = jnp.bitwise_and(yrow, bb - 1) == b
        y_parts = []
        for g in range(GROUPS):
            st = _dot(x6t[g * GROUP_W:(g + 1) * GROUP_W, :], rhs[:, g * STATE:(g + 1) * STATE])
            hn_parts = []
            for hh in range(HEADS_PER_GROUP):
                h = g * HEADS_PER_GROUP + hh
                rows = slice(h * HEAD_DIM, (h + 1) * HEAD_DIM)
                hn = da_ref[i * bb + b, h] * h0_ref[b, rows, :] + st[hh * HEAD_DIM:(hh + 1) * HEAD_DIM, :]
                ho_ref[b, rows, :] = hn
                hn_parts.append(hn.astype(BF16))
            hn_g = jnp.concatenate(hn_parts, axis=0)
            r = _dot_nt(c3[:, g * STATE:(g + 1) * STATE], hn_g)
            y_parts.append(jnp.sum(jnp.where(y_own, r, 0.0), axis=0, keepdims=True))
        y_ref[b:b + 1, :] = jnp.concatenate(y_parts, axis=1)


def _sample_out_kernel(x_ref, mod_ref, yraw_ref, xbc_ref, z_ref, gm_ref, plru_ref,
                       dexp, normw, wsp, wo, lng, lnb, o_ref):
    x = x_ref[...]
    gate = mod_ref[:, 2 * D_MODEL:]
    y = yraw_ref[...] + dexp[...] * xbc_ref[:, :D_SSD]
    y_ssd = _gated_rmsnorm(y, z_ref[...], normw[...])
    p_ssd = _dot(y_ssd.astype(BF16), _w(wsp))
    gm = gm_ref[...]
    merged = gm[:, :D_MODEL] * plru_ref[...] + gm[:, D_MODEL:] * p_ssd
    out = _dot(merged.astype(BF16), _w(wo))
    o_ref[...] = _layer_norm(ALPHA * x + gate * out, lng[...], lnb[...])


def _resident(shape, grid_rank):
    zeros = (0,) * len(shape)
    if grid_rank == 1:
        imap = lambda i: zeros
    else:
        imap = lambda b, j: zeros
    return pl.BlockSpec(shape, imap, pipeline_mode=pl.Buffered(1))


def _whole(shape):
    return jax.ShapeDtypeStruct(shape, F32)


def kernel(x_prompt, x_sample, state_lru_h, state_lru_conv, state_ssd_h, state_ssd_conv, c_prompt, c_sample, w_cond, b_cond, w_in, lru_conv_w, lru_conv_b, lru_wa, lru_ba, lru_wx, lru_bx, lru_lambda, ssd_conv_w, ssd_conv_b, ssd_dt_bias, ssd_a_log, ssd_d, ssd_norm_w, w_lru_proj, w_ssd_proj, w_out, ln_g, ln_b):
    nbp, seq, _ = x_prompt.shape
    nbs = x_sample.shape[0]
    tc = PROMPT_TC
    assert seq % tc == 0 and tc % CHUNK == 0 and nbs % SAMPLE_BB == 0 and SAMPLE_BB == SUBLANES

    w_in0 = w_in[0]
    w_main = w_in0[:, :N_MAIN].astype(BF16)
    w_mg = w_in0[:, N_MAIN + HEADS:].astype(BF16)
    w_dt_cols = w_in0[:, N_MAIN:N_MAIN + HEADS]
    w_dt = jnp.pad(w_dt_cols, ((0, 0), (0, LANES - HEADS))).astype(BF16)
    w_dtT = w_dt_cols.T.astype(BF16)
    nq = D_LRU // LANES
    wa = lru_wa[0].reshape(nq, 2, LRU_BLOCK_W, LRU_BLOCK_W)
    wx = lru_wx[0].reshape(nq, 2, LRU_BLOCK_W, LRU_BLOCK_W)
    zb = jnp.zeros((nq, LRU_BLOCK_W, LRU_BLOCK_W), F32)

    def blockdiag(w):
        top = jnp.concatenate([w[:, 0], zb], axis=2)
        bot = jnp.concatenate([zb, w[:, 1]], axis=2)
        return jnp.concatenate([top, bot], axis=1)

    w_gate = jnp.concatenate([blockdiag(wa), blockdiag(wx)], axis=2).astype(BF16)
    row = lambda v: v.reshape(1, -1)
    ba, bx, lam = row(lru_ba[0]), row(lru_bx[0]), row(lru_lambda[0])
    lcw, lcb = lru_conv_w[0], row(lru_conv_b[0])
    scw, scb = ssd_conv_w[0], row(ssd_conv_b[0])
    dtb = jnp.pad(row(ssd_dt_bias[0]), ((0, 0), (0, LANES - HEADS)))
    alog = jnp.pad(row(ssd_a_log[0]), ((0, 0), (0, LANES - HEADS)))
    dtbT = ssd_dt_bias[0].reshape(HEADS, 1)
    alogT = ssd_a_log[0].reshape(HEADS, 1)
    dexp = row(jnp.repeat(ssd_d[0], HEAD_DIM))
    normw = row(ssd_norm_w[0])
    wlp = w_lru_proj[0].astype(BF16)
    wsp = w_ssd_proj[0].astype(BF16)
    wo = w_out[0].astype(BF16)
    lng, lnb = row(ln_g[0]), row(ln_b[0])
    k_idx = jnp.arange(LANES)[:, None]
    c_idx = jnp.arange(D_SSD)[None, :]
    e3 = jnp.logical_and(k_idx % HEADS == c_idx // HEAD_DIM, k_idx < 3 * HEADS).astype(BF16)

    c_all = jnp.concatenate([c_prompt, c_sample], axis=0)
    mod = pl.pallas_call(
        _cond_kernel,
        out_shape=_whole((nbp + nbs, 3 * D_MODEL)),
        name="cond",
    )(c_all, w_cond[0], row(b_cond[0]))
    mod_p = mod[:nbp].reshape(nbp, 1, 3 * D_MODEL)
    mod_s = mod[nbp:]

    weights = (w_main, w_dt, w_dtT, w_mg, lcw, lcb, w_gate, ba, bx, lam, scw, scb, dtb, dtbT,
               alog, alogT, dexp, normw, wlp, wsp, wo, lng, lnb, e3)
    n_chunks = tc // CHUNK
    y_p, lh_p, lc_p, sh_p, sc_p = pl.pallas_call(
        _prompt_kernel,
        grid=(nbp, seq // tc),
        in_specs=[pl.BlockSpec((1, tc, D_MODEL), lambda b, j: (b, j, 0)),
                  pl.BlockSpec((1, 1, 3 * D_MODEL), lambda b, j: (b, 0, 0))]
                 + [_resident(w.shape, 2) for w in weights],
        out_specs=[pl.BlockSpec((1, tc, D_MODEL), lambda b, j: (b, j, 0)),
                   pl.BlockSpec((1, 1, D_LRU), lambda b, j: (b, 0, 0)),
                   pl.BlockSpec((1, 1, CONV_W - 1, D_LRU), lambda b, j: (0, b, 0, 0)),
                   pl.BlockSpec((1, D_SSD, STATE), lambda b, j: (b, 0, 0)),
                   pl.BlockSpec((1, 1, CONV_W - 1, CONV_DIM), lambda b, j: (0, b, 0, 0))],
        out_shape=[_whole((nbp, seq, D_MODEL)), _whole((nbp, 1, D_LRU)),
                   _whole((1, nbp, CONV_W - 1, D_LRU)), _whole((nbp, D_SSD, STATE)),
                   _whole((1, nbp, CONV_W - 1, CONV_DIM))],
        scratch_shapes=[pltpu.VMEM((n_chunks, HALO + CHUNK, D_LRU), F32),
                        pltpu.VMEM((n_chunks, HALO + CHUNK, CONV_DIM), F32),
                        pltpu.VMEM((HALO, D_LRU), F32),
                        pltpu.VMEM((HALO, CONV_DIM), F32),
                        pltpu.VMEM((1, D_LRU), F32),
                        pltpu.VMEM((STATE, D_SSD), F32)],
        compiler_params=pltpu.CompilerParams(
            dimension_semantics=("arbitrary", "arbitrary"), vmem_limit_bytes=VMEM_LIMIT),
        name="prompt",
    )(x_prompt, mod_p, *weights)

    xs2 = x_sample.reshape(nbs, D_MODEL)
    lh_s, lco, sco, p_lru, xbc_s, xdt, da, z_s, gm_s = pl.pallas_call(
        _sample_in_kernel,
        out_shape=[_whole((nbs, D_LRU)), _whole((CONV_W - 1, nbs, D_LRU)),
                   _whole((CONV_W - 1, nbs, CONV_DIM)), _whole((nbs, D_MODEL)),
                   _whole((nbs, CONV_DIM)), _whole((nbs, D_SSD)), _whole((nbs, LANES)),
                   _whole((nbs, D_SSD)), _whole((nbs, 2 * D_MODEL))],
        compiler_params=pltpu.CompilerParams(vmem_limit_bytes=VMEM_LIMIT),
        name="sample_in",
    )(xs2, mod_s, state_lru_h[0], jnp.transpose(state_lru_conv[0], (1, 0, 2)),
      jnp.transpose(state_ssd_conv[0], (1, 0, 2)), w_main, w_dt, w_mg,
      lcw, lcb, w_gate, ba, bx, lam, scw, scb, dtb, alog, wlp, e3)

    bb = SAMPLE_BB
    n_bc = GROUPS * STATE
    sh_s, y_raw = pl.pallas_call(
        _sample_state_kernel,
        grid=(nbs // bb,),
        in_specs=[pl.BlockSpec(memory_space=pltpu.SMEM),
                  pl.BlockSpec((bb, D_SSD, STATE), lambda i: (i, 0, 0)),
                  pl.BlockSpec((bb, D_SSD), lambda i: (i, 0)),
                  pl.BlockSpec((bb, n_bc), lambda i: (i, D_SSD // n_bc)),
                  pl.BlockSpec((bb, n_bc), lambda i: (i, D_SSD // n_bc + 1))],
        out_specs=[pl.BlockSpec((bb, D_SSD, STATE), lambda i: (i, 0, 0)),
                   pl.BlockSpec((bb, D_SSD), lambda i: (i, 0))],
        out_shape=[_whole((nbs, D_SSD, STATE)), _whole((nbs, D_SSD))],
        compiler_params=pltpu.CompilerParams(
            dimension_semantics=("arbitrary",), vmem_limit_bytes=VMEM_LIMIT),
        name="sample_state",
    )(da[:, :HEADS], state_ssd_h[0].reshape(nbs, D_SSD, STATE), xdt, xbc_s, xbc_s)

    y_s = pl.pallas_call(
        _sample_out_kernel,
        out_shape=_whole((nbs, D_MODEL)),
        compiler_params=pltpu.CompilerParams(vmem_limit_bytes=VMEM_LIMIT),
        name="sample_out",
    )(xs2, mod_s, y_raw, xbc_s, z_s, gm_s, p_lru, dexp, normw, wsp, wo, lng, lnb)

    return (y_p,
            y_s.reshape(nbs, 1, D_MODEL),
            lh_p.reshape(1, nbp, D_LRU),
            lc_p,
            sh_p.reshape(1, nbp, HEADS, HEAD_DIM, STATE),
            sc_p,
            lh_s.reshape(1, nbs, D_LRU),
            jnp.transpose(lco, (1, 0, 2)).reshape(1, nbs, CONV_W - 1, D_LRU),
            sh_s.reshape(1, nbs, HEADS, HEAD_DIM, STATE),
            jnp.transpose(sco, (1, 0, 2)).reshape(1, nbs, CONV_W - 1, CONV_DIM))
```

```python
import jax
import jax.numpy as jnp
from jax import lax
from jax.experimental import pallas as pl
from jax.experimental.pallas import tpu as pltpu

F32 = jnp.float32
BF16 = jnp.bfloat16

D_MODEL = 1024
D_LRU = 1024
LRU_BLOCK_W = 64
LRU_C = 8.0
D_SSD = 2048
HEAD_DIM = 64
HEADS = 32
GROUPS = 4
HEADS_PER_GROUP = HEADS // GROUPS
STATE = 128
GROUP_W = D_SSD // GROUPS
CHUNK = 128
CONV_DIM = D_SSD + 2 * GROUPS * STATE
CONV_W = 4
LN_EPS = 1e-5
RMS_EPS = 1e-5
ALPHA = 2.0 ** 0.25
LOG2_E = 1.4426950408889634

LANES = 128
SUBLANES = 8
SEG = CHUNK // SUBLANES
HALO = (CONV_W - 1) * SUBLANES
N_MAIN = 2 * D_LRU + D_SSD + CONV_DIM
OFF_LRU_X, OFF_LRU_Z, OFF_SSD_Z, OFF_XBC = 0, D_LRU, 2 * D_LRU, 2 * D_LRU + D_SSD
N_IN = N_MAIN + HEADS + 2 * D_MODEL

PROMPT_TC = 512
SAMPLE_BB = 8
CONV_SLAB = 512
VMEM_LIMIT = 61 * 1024 * 1024


def _sigmoid(x):
    return 0.5 + 0.5 * jnp.tanh(0.5 * x)


def _silu(x):
    h = 0.5 * x
    return h + h * jnp.tanh(h)


def _softplus(x):
    return jnp.maximum(x, 0.0) + jnp.log1p(jnp.exp(-jnp.abs(x)))


def _dot(a, b):
    return jnp.dot(a, b, preferred_element_type=F32)


def _dot_nt(a, b):
    return lax.dot_general(a, b, (((1,), (1,)), ((), ())), preferred_element_type=F32)


def _dot_tn(a, b):
    return lax.dot_general(a, b, (((0,), (0,)), ((), ())), preferred_element_type=F32)


def _w(ref, c0=None, c1=None):
    return ref[...] if c0 is None else ref[:, c0:c1]


def _split3(x):
    hi = x.astype(BF16).astype(F32)
    r = x - hi
    mid = r.astype(BF16).astype(F32)
    lo = r - mid
    return hi, mid, lo


def _dot_exact_rhs(lhs_bf16, x):
    hi, mid, lo = _split3(x)
    return (_dot(lhs_bf16, hi.astype(BF16)) + _dot(lhs_bf16, mid.astype(BF16))
            + _dot(lhs_bf16, lo.astype(BF16)))


def _dot_exact_lhs(x, rhs_bf16):
    hi, mid, lo = _split3(x)
    return (_dot(hi.astype(BF16), rhs_bf16) + _dot(mid.astype(BF16), rhs_bf16)
            + _dot(lo.astype(BF16), rhs_bf16))


def _expand_heads(v, e3):
    lane = lax.broadcasted_iota(jnp.int32, v.shape, 1)
    v = jnp.where(lane < HEADS, v, 0.0)
    hi, mid, lo = _split3(v)
    packed = hi + pltpu.roll(mid, HEADS, axis=1) + pltpu.roll(lo, 2 * HEADS, axis=1)
    return _dot(packed.astype(BF16), e3)


def _lru_gates(u, w_gate_ref, ba, bx, lam):
    ub = u.astype(BF16)
    r_parts, i_parts = [], []
    for q in range(D_LRU // LANES):
        gq = _dot(ub[:, q * LANES:(q + 1) * LANES], w_gate_ref[q])
        r_parts.append(gq[:, :LANES])
        i_parts.append(gq[:, LANES:])
    r = _sigmoid(jnp.concatenate(r_parts, axis=1) + ba)
    i = _sigmoid(jnp.concatenate(i_parts, axis=1) + bx)
    log_a = (-LRU_C) * r * _softplus(-lam)
    a = jnp.exp(log_a)
    v = -jnp.tanh(log_a) * (a * a + 1.0)
    mult = jnp.where(v > 0.0, v * lax.rsqrt(v), 0.0)
    return a, mult, i


def _gated_rmsnorm(y, z, norm_w):
    g = y * _silu(z)
    parts = []
    for gi in range(GROUPS):
        gg = g[:, gi * GROUP_W:(gi + 1) * GROUP_W]
        ms = jnp.mean(gg * gg, axis=-1, keepdims=True)
        parts.append(gg * lax.rsqrt(ms + RMS_EPS))
    return jnp.concatenate(parts, axis=1) * norm_w


def _layer_norm(v, g, b):
    mu = jnp.mean(v, axis=-1, keepdims=True)
    d = v - mu
    var = jnp.mean(d * d, axis=-1, keepdims=True)
    return d * lax.rsqrt(var + LN_EPS) * g + b


def _cond_kernel(c_ref, w_ref, b_ref, o_ref):
    o_ref[...] = _dot(c_ref[...].astype(BF16), w_ref[...].astype(BF16)) + b_ref[...]


def _time_permute(v, n_chunks, to_permuted):
    sub = lax.broadcasted_iota(jnp.int32, (SUBLANES, 1), 0)
    n_half = SEG // SUBLANES
    out = [None] * (n_chunks * SEG)
    for c in range(n_chunks):
        for h in range(n_half):
            natural = [c * SEG + n_half * s + h for s in range(SUBLANES)]
            permuted = [c * SEG + SUBLANES * h + i for i in range(SUBLANES)]
            src, dst = (natural, permuted) if to_permuted else (permuted, natural)
            rows = [v[g * SUBLANES:(g + 1) * SUBLANES] for g in src]
            d = SUBLANES // 2
            while d >= 1:
                keep = jnp.bitwise_and(sub, d) == 0
                for a in range(SUBLANES):
                    if a & d == 0:
                        lo, hi = rows[a], rows[a + d]
                        rows[a] = jnp.where(keep, lo, pltpu.roll(hi, d, axis=0))
                        rows[a + d] = jnp.where(keep, pltpu.roll(lo, SUBLANES - d, axis=0), hi)
                d //= 2
            for i, g in enumerate(dst):
                out[g] = rows[i]
    return jnp.concatenate(out, axis=0)


def _permuted_conv_slab(raw, buf, tail, w_ref, b_ref, n_chunks, c0):
    cols = slice(c0, c0 + raw.shape[1])
    sub = lax.broadcasted_iota(jnp.int32, (SUBLANES, 1), 0)
    outs = []
    for c in range(n_chunks):
        buf[c, HALO:HALO + CHUNK, cols] = raw[c * CHUNK:(c + 1) * CHUNK]
        for i in range(CONV_W - 1):
            r0 = CHUNK + i * SUBLANES
            cur = buf[c, r0:r0 + SUBLANES, cols]
            if c == 0:
                prev = tail[i * SUBLANES:(i + 1) * SUBLANES, cols]
            else:
                prev = buf[c - 1, r0:r0 + SUBLANES, cols]
            buf[c, i * SUBLANES:(i + 1) * SUBLANES, cols] = pltpu.roll(
                jnp.where(sub == SUBLANES - 1, prev, cur), 1, axis=0)
        acc = b_ref[:, cols] + w_ref[0:1, cols] * buf[c, 0:CHUNK, cols]
        for k in range(1, CONV_W):
            acc = acc + w_ref[k:k + 1, cols] * buf[c, k * SUBLANES:k * SUBLANES + CHUNK, cols]
        outs.append(acc)
    tail[:, cols] = buf[n_chunks - 1, CHUNK:CHUNK + HALO, cols]
    return jnp.concatenate(outs, axis=0) if n_chunks > 1 else outs[0]


def _project_and_conv(hb, w_ref, off, width, buf, tail, cw_ref, cb_ref, n_chunks, post=None):
    n_slabs = width // CONV_SLAB
    raws, outs = [], []
    for k in range(n_slabs + 1):
        if k < n_slabs:
            raws.append(_dot(hb, _w(w_ref, off + k * CONV_SLAB, off + (k + 1) * CONV_SLAB)))
        if k >= 1:
            o = _permuted_conv_slab(raws[k - 1], buf, tail, cw_ref, cb_ref, n_chunks, (k - 1) * CONV_SLAB)
            outs.append(o if post is None else post(o))
    return jnp.concatenate(outs, axis=1)


def _permuted_scan(a, b, h0, n_chunks):
    hs = []
    h_in = h0
    for c in range(n_chunks):
        ps, hz = [], []
        for q in range(SEG):
            r0 = c * CHUNK + q * SUBLANES
            aq, bq = a[r0:r0 + SUBLANES], b[r0:r0 + SUBLANES]
            if q == 0:
                p, h = aq, bq
            else:
                p, h = aq * p, aq * h + bq
            ps.append(p)
            hz.append(h)
        carries = [h_in]
        for s in range(SUBLANES):
            carries.append(h[s:s + 1] + p[s:s + 1] * carries[-1])
        h_in = carries[SUBLANES]
        carry = jnp.concatenate(carries[:SUBLANES], axis=0)
        hs.extend(hz[q] + ps[q] * carry for q in range(SEG))
    return jnp.concatenate(hs, axis=0), h_in


def _prompt_kernel(x_ref, mod_ref, w_main, w_dtT, lcw, lcb, w_gate, ba, bx, lam,
                   scw, scb, dtb, dtbT, alog, alogT, dexp, normw, wlp, wsp, wo, lng, lnb, e3,
                   y_ref, lh_ref, lc_ref, sh_ref, sc_ref,
                   lbuf, sbuf, ltail, stail, hl, hT):
    tc = x_ref.shape[1]
    n_chunks = tc // CHUNK
    j = pl.program_id(1)
    last = pl.num_programs(1) - 1

    @pl.when(j == 0)
    def _():
        ltail[...] = jnp.zeros_like(ltail)
        stail[...] = jnp.zeros_like(stail)
        hl[...] = jnp.zeros_like(hl)
        hT[...] = jnp.zeros_like(hT)

    x = x_ref[0]
    mod = mod_ref[0]
    shift = mod[:, :D_MODEL]
    scale = mod[:, D_MODEL:2 * D_MODEL]
    gate = mod[:, 2 * D_MODEL:]
    hb = _time_permute(x * (1.0 + scale) + shift, n_chunks, True).astype(BF16)

    u = _project_and_conv(hb, w_main, OFF_LRU_X, D_LRU, lbuf, ltail, lcw, lcb, n_chunks)

    a, mult, ig = _lru_gates(u, w_gate, ba[...], bx[...], lam[...])
    iu = ig * u
    row = lax.broadcasted_iota(jnp.int32, (tc, 1), 0)
    first = jnp.logical_and(row == 0, j == 0)
    lru_z = _dot(hb, _w(w_main, OFF_LRU_Z, OFF_LRU_Z + D_LRU))
    hs, h_end = _permuted_scan(a, jnp.where(first, iu, mult * iu), hl[...], n_chunks)
    hl[...] = h_end
    y_lru = hs * _silu(lru_z)
    xbc = _project_and_conv(hb, w_main, OFF_XBC, CONV_DIM, sbuf, stail, scw, scb, n_chunks, post=_silu)
    p_lru = _dot(y_lru.astype(BF16), _w(wlp))

    xs = xbc[:, :D_SSD]
    bm = xbc[:, D_SSD:D_SSD + GROUPS * STATE]
    cm = xbc[:, D_SSD + GROUPS * STATE:]
    tail_proj = _dot(hb, _w(w_main, N_MAIN, N_IN))
    dt = _softplus(tail_proj[:, :LANES] + dtb[...])
    merge_logits = tail_proj[:, HEADS:HEADS + 2 * D_MODEL]
    dtT = _softplus(_dot_nt(w_dtT[...], hb) + dtbT[...])
    a_row = -LOG2_E * jnp.exp(alog[...])
    a_col = -LOG2_E * jnp.exp(alogT[...])

    ri = lax.broadcasted_iota(jnp.int32, (CHUNK, CHUNK), 0)
    ci = lax.broadcasted_iota(jnp.int32, (CHUNK, CHUNK), 1)
    t_row = jnp.bitwise_and(ri, SUBLANES - 1) * SEG + jnp.right_shift(ri, 3)
    t_col = jnp.bitwise_and(ci, SUBLANES - 1) * SEG + jnp.right_shift(ci, 3)
    causal = t_row >= t_col
    tril = causal.astype(BF16)
    triu = (t_row <= t_col).astype(BF16)
    lane = lax.broadcasted_iota(jnp.int32, (CHUNK, LANES), 1)
    lo_half = lane < HEAD_DIM

    def chunk_prologue(c):
        r0 = c * CHUNK
        xs_c = xs[r0:r0 + CHUNK]
        dt_c = dt[r0:r0 + CHUNK]
        dtT_c = dtT[:, r0:r0 + CHUNK]
        acs = _dot_exact_rhs(tril, dt_c * a_row)
        acsT = _dot_exact_lhs(dtT_c * a_col, triu)
        e_exp = _expand_heads(jnp.exp2(acs), e3[...])
        w_exp = _expand_heads(jnp.exp2(acs[CHUNK - 1:CHUNK, :] - acs) * dt_c, e3[...])
        return dict(xs=xs_c, xs_b=xs_c.astype(BF16), bm=bm[r0:r0 + CHUNK].astype(BF16),
                    cm=cm[r0:r0 + CHUNK].astype(BF16), dtT=dtT_c, acs=acs, acsT=acsT,
                    e_exp=e_exp, xsw=(xs_c * w_exp).astype(BF16))

    def group_cb(p, g):
        return _dot_nt(p["cm"][:, g * STATE:(g + 1) * STATE], p["bm"][:, g * STATE:(g + 1) * STATE])

    y_chunks = []
    late_proj = []
    nxt = chunk_prologue(0)
    for c in range(n_chunks):
        p = nxt
        if c == 0:
            late_proj.append(_dot(hb, _w(w_main, OFF_SSD_Z, OFF_SSD_Z + D_SSD)))
        if c + 1 < n_chunks:
            nxt = chunk_prologue(c + 1)
        acs, acsT, dtT_c, xs_cb = p["acs"], p["acsT"], p["dtT"], p["xs_b"]

        y_parts = []
        cb_next = group_cb(p, 0)
        for g in range(GROUPS):
            cb = cb_next
            if g + 1 < GROUPS:
                cb_next = group_cb(p, g + 1)
            for q in range(HEADS_PER_GROUP // 2):
                pair = g * (HEADS_PER_GROUP // 2) + q
                xp = xs_cb[:, pair * LANES:(pair + 1) * LANES]
                zero = jnp.zeros_like(xp)
                ms = []
                for half in range(2):
                    h = 2 * pair + half
                    seg = acs[:, h:h + 1] - acsT[h:h + 1, :]
                    m = jnp.exp2(jnp.where(causal, seg, -jnp.inf)) * cb * dtT_c[h:h + 1, :]
                    ms.append(m.astype(BF16))
                x2 = jnp.concatenate([jnp.where(lo_half, xp, zero), jnp.where(lo_half, zero, xp)], axis=0)
                y_parts.append(_dot(jnp.concatenate(ms, axis=1), x2))
        y_diag = jnp.concatenate(y_parts, axis=1)

        h_prev = hT[...]
        h_prev_b = h_prev.astype(BF16)
        off_parts, st_parts = [], []
        for g in range(GROUPS):
            sl = slice(g * GROUP_W, (g + 1) * GROUP_W)
            off_parts.append(_dot(p["cm"][:, g * STATE:(g + 1) * STATE], h_prev_b[:, sl]))
            st_parts.append(_dot_tn(p["bm"][:, g * STATE:(g + 1) * STATE], p["xsw"][:, sl]))
        y_off = jnp.concatenate(off_parts, axis=1) * p["e_exp"]
        hT[...] = p["e_exp"][CHUNK - 1:CHUNK, :] * h_prev + jnp.concatenate(st_parts, axis=1)
        y_chunks.append(y_diag + y_off + dexp[...] * p["xs"])
    ssd_z, = late_proj

    halves = [(0, tc)] if n_chunks == 1 else [(0, tc // 2), (tc // 2, tc)]
    y_ssd_b, p_ssd, merged_b, outs = {}, {}, {}, {}

    def norm_half(i):
        r0, r1 = halves[i]
        yh = jnp.concatenate(y_chunks[r0 // CHUNK:r1 // CHUNK], axis=0)
        y_ssd_b[i] = _gated_rmsnorm(yh, ssd_z[r0:r1], normw[...]).astype(BF16)

    def merge_half(i):
        r0, r1 = halves[i]
        gm = _sigmoid(merge_logits[r0:r1])
        merged_b[i] = (gm[:, :D_MODEL] * p_lru[r0:r1] + gm[:, D_MODEL:] * p_ssd[i]).astype(BF16)

    def finish_half(i):
        r0, r1 = halves[i]
        o = _time_permute(outs[i], (r1 - r0) // CHUNK, False)
        y_ref[0, r0:r1, :] = _layer_norm(ALPHA * x[r0:r1] + gate * o, lng[...], lnb[...])

    norm_half(0)
    for i in range(len(halves)):
        p_ssd[i] = _dot(y_ssd_b[i], _w(wsp))
        if i + 1 < len(halves):
            norm_half(i + 1)
        merge_half(i)
        outs[i] = _dot(merged_b[i], _w(wo))
        if i >= 1:
            finish_half(i - 1)
    finish_half(len(halves) - 1)

    @pl.when(j == last)
    def _():
        for i in range(CONV_W - 1):
            r = (i + 1) * SUBLANES - 1
            lc_ref[0, 0, i:i + 1, :] = ltail[r:r + 1, :]
            sc_ref[0, 0, i:i + 1, :] = stail[r:r + 1, :]
        lh_ref[0] = hl[...]
        hfin = hT[...]
        for q in range(D_SSD // LANES):
            sh_ref[0, q * LANES:(q + 1) * LANES, :] = hfin[:, q * LANES:(q + 1) * LANES].T


def _sample_in_kernel(x_ref, mod_ref, lh0_ref, lst_ref, sst_ref, w_main, lcw, lcb,
                      w_gate, ba, bx, lam, scw, scb, dtb, alog, wlp, e3,
                      lh_ref, lco_ref, sco_ref, plru_ref, xbc_ref, xdt_ref, da_ref, z_ref, gm_ref):
    x = x_ref[...]
    mod = mod_ref[...]
    shift = mod[:, :D_MODEL]
    scale = mod[:, D_MODEL:2 * D_MODEL]
    hb = (x * (1.0 + scale) + shift).astype(BF16)

    lru_x = _dot(hb, _w(w_main, OFF_LRU_X, OFF_LRU_X + D_LRU))
    xbc_raw = _dot(hb, _w(w_main, OFF_XBC, OFF_XBC + CONV_DIM))
    u = lcb[...] + lcw[CONV_W - 1:CONV_W, :] * lru_x
    xbc = scb[...] + scw[CONV_W - 1:CONV_W, :] * xbc_raw
    for k in range(CONV_W - 1):
        u = u + lcw[k:k + 1, :] * lst_ref[k]
        xbc = xbc + scw[k:k + 1, :] * sst_ref[k]
    for k in range(CONV_W - 2):
        lco_ref[k] = lst_ref[k + 1]
        sco_ref[k] = sst_ref[k + 1]
    lco_ref[CONV_W - 2] = lru_x
    sco_ref[CONV_W - 2] = xbc_raw
    xbc = _silu(xbc)
    xbc_ref[...] = xbc

    a, mult, ig = _lru_gates(u, w_gate, ba[...], bx[...], lam[...])
    h_new = a * lh0_ref[...] + mult * ig * u
    lh_ref[...] = h_new
    lru_z = _dot(hb, _w(w_main, OFF_LRU_Z, OFF_LRU_Z + D_LRU))
    plru_ref[...] = _dot((h_new * _silu(lru_z)).astype(BF16), _w(wlp))

    tail_proj = _dot(hb, _w(w_main, N_MAIN, N_IN))
    dt = _softplus(tail_proj[:, :LANES] + dtb[...])
    da_ref[...] = jnp.exp(dt * (-jnp.exp(alog[...])))
    xdt_ref[...] = xbc[:, :D_SSD] * _expand_heads(dt, e3[...])
    z_ref[...] = _dot(hb, _w(w_main, OFF_SSD_Z, OFF_SSD_Z + D_SSD))
    gm_ref[...] = _sigmoid(tail_proj[:, HEADS:HEADS + 2 * D_MODEL])


def _sample_state_kernel(da_ref, h0_ref, xdt_ref, b_ref, c_ref, ho_ref, y_ref):
    i = pl.program_id(0)
    bb = h0_ref.shape[0]
    n_terms = 6
    pad_rows = LANES - n_terms * bb

    def stack(terms, width):
        return jnp.concatenate(list(terms) + [jnp.zeros((pad_rows, width), F32)], axis=0)

    xh, xm, xl = _split3(xdt_ref[...])
    bh, bm, bl = _split3(b_ref[...])
    x6t = stack((xh, xh, xm, xm, xh, xl), D_SSD).T.astype(BF16)
    b6 = stack((bh, bm, bh, bm, bl, bh), GROUPS * STATE)
    ch, cmid, cl = _split3(c_ref[...])
    c3 = jnp.concatenate([ch, cmid, cl], axis=0).astype(BF16)
    krow = lax.broadcasted_iota(jnp.int32, (LANES, 1), 0)
    yrow = lax.broadcasted_iota(jnp.int32, (3 * bb, 1), 0)
    for b in range(bb):
        own = jnp.logical_and(jnp.bitwise_and(krow, bb - 1) == b, krow < n_terms * bb)
        rhs = jnp.where(own, b6, 0.0).astype(BF16)
        y_own = jnp.bitwise_and(yrow, bb - 1) == b
        y_parts = []
        for g in range(GROUPS):
            st = _dot(x6t[g * GROUP_W:(g + 1) * GROUP_W, :], rhs[:, g * STATE:(g + 1) * STATE])
            hn_parts = []
            for hh in range(HEADS_PER_GROUP):
                h = g * HEADS_PER_GROUP + hh
                rows = slice(h * HEAD_DIM, (h + 1) * HEAD_DIM)
                hn = da_ref[i * bb + b, h] * h0_ref[b, rows, :] + st[hh * HEAD_DIM:(hh + 1) * HEAD_DIM, :]
                ho_ref[b, rows, :] = hn
                hn_parts.append(hn.astype(BF16))
            hn_g = jnp.concatenate(hn_parts, axis=0)
            r = _dot_nt(c3[:, g * STATE:(g + 1) * STATE], hn_g)
            y_parts.append(jnp.sum(jnp.where(y_own, r, 0.0), axis=0, keepdims=True))
        y_ref[b:b + 1, :] = jnp.concatenate(y_parts, axis=1)


def _sample_out_kernel(x_ref, mod_ref, yraw_ref, xbc_ref, z_ref, gm_ref, plru_ref,
                       dexp, normw, wsp, wo, lng, lnb, o_ref):
    x = x_ref[...]
    gate = mod_ref[:, 2 * D_MODEL:]
    y = yraw_ref[...] + dexp[...] * xbc_ref[:, :D_SSD]
    y_ssd = _gated_rmsnorm(y, z_ref[...], normw[...])
    p_ssd = _dot(y_ssd.astype(BF16), _w(wsp))
    gm = gm_ref[...]
    merged = gm[:, :D_MODEL] * plru_ref[...] + gm[:, D_MODEL:] * p_ssd
    out = _dot(merged.astype(BF16), _w(wo))
    o_ref[...] = _layer_norm(ALPHA * x + gate * out, lng[...], lnb[...])


def _resident(shape, grid_rank):
    zeros = (0,) * len(shape)
    if grid_rank == 1:
        imap = lambda i: zeros
    else:
        imap = lambda b, j: zeros
    return pl.BlockSpec(shape, imap, pipeline_mode=pl.Buffered(1))


def _whole(shape):
    return jax.ShapeDtypeStruct(shape, F32)


def kernel(x_prompt, x_sample, state_lru_h, state_lru_conv, state_ssd_h, state_ssd_conv, c_prompt, c_sample, w_cond, b_cond, w_in, lru_conv_w, lru_conv_b, lru_wa, lru_ba, lru_wx, lru_bx, lru_lambda, ssd_conv_w, ssd_conv_b, ssd_dt_bias, ssd_a_log, ssd_d, ssd_norm_w, w_lru_proj, w_ssd_proj, w_out, ln_g, ln_b):
    nbp, seq, _ = x_prompt.shape
    nbs = x_sample.shape[0]
    tc = PROMPT_TC
    assert seq % tc == 0 and tc % CHUNK == 0 and nbs % SAMPLE_BB == 0 and SAMPLE_BB == SUBLANES

    w_in0 = w_in[0]
    w_main = w_in0.astype(BF16)
    w_dt_cols = w_in0[:, N_MAIN:N_MAIN + HEADS]
    w_dtT = w_dt_cols.T.astype(BF16)
    nq = D_LRU // LANES
    wa = lru_wa[0].reshape(nq, 2, LRU_BLOCK_W, LRU_BLOCK_W)
    wx = lru_wx[0].reshape(nq, 2, LRU_BLOCK_W, LRU_BLOCK_W)
    zb = jnp.zeros((nq, LRU_BLOCK_W, LRU_BLOCK_W), F32)

    def blockdiag(w):
        top = jnp.concatenate([w[:, 0], zb], axis=2)
        bot = jnp.concatenate([zb, w[:, 1]], axis=2)
        return jnp.concatenate([top, bot], axis=1)

    w_gate = jnp.concatenate([blockdiag(wa), blockdiag(wx)], axis=2).astype(BF16)
    row = lambda v: v.reshape(1, -1)
    ba, bx, lam = row(lru_ba[0]), row(lru_bx[0]), row(lru_lambda[0])
    lcw, lcb = lru_conv_w[0], row(lru_conv_b[0])
    scw, scb = ssd_conv_w[0], row(ssd_conv_b[0])
    dtb = jnp.pad(row(ssd_dt_bias[0]), ((0, 0), (0, LANES - HEADS)))
    alog = jnp.pad(row(ssd_a_log[0]), ((0, 0), (0, LANES - HEADS)))
    dtbT = ssd_dt_bias[0].reshape(HEADS, 1)
    alogT = ssd_a_log[0].reshape(HEADS, 1)
    dexp = row(jnp.repeat(ssd_d[0], HEAD_DIM))
    normw = row(ssd_norm_w[0])
    wlp = w_lru_proj[0].astype(BF16)
    wsp = w_ssd_proj[0].astype(BF16)
    wo = w_out[0].astype(BF16)
    lng, lnb = row(ln_g[0]), row(ln_b[0])
    k_idx = jnp.arange(LANES)[:, None]
    c_idx = jnp.arange(D_SSD)[None, :]
    e3 = jnp.logical_and(k_idx % HEADS == c_idx // HEAD_DIM, k_idx < 3 * HEADS).astype(BF16)

    c_all = jnp.concatenate([c_prompt, c_sample], axis=0)
    mod = pl.pallas_call(
        _cond_kernel,
        out_shape=_whole((nbp + nbs, 3 * D_MODEL)),
        name="cond",
    )(c_all, w_cond[0], row(b_cond[0]))
    mod_p = mod[:nbp].reshape(nbp, 1, 3 * D_MODEL)
    mod_s = mod[nbp:]

    weights = (w_main, w_dtT, lcw, lcb, w_gate, ba, bx, lam, scw, scb, dtb, dtbT,
               alog, alogT, dexp, normw, wlp, wsp, wo, lng, lnb, e3)
    n_chunks = tc // CHUNK
    y_p, lh_p, lc_p, sh_p, sc_p = pl.pallas_call(
        _prompt_kernel,
        grid=(nbp, seq // tc),
        in_specs=[pl.BlockSpec((1, tc, D_MODEL), lambda b, j: (b, j, 0)),
                  pl.BlockSpec((1, 1, 3 * D_MODEL), lambda b, j: (b, 0, 0))]
                 + [_resident(w.shape, 2) for w in weights],
        out_specs=[pl.BlockSpec((1, tc, D_MODEL), lambda b, j: (b, j, 0)),
                   pl.BlockSpec((1, 1, D_LRU), lambda b, j: (b, 0, 0)),
                   pl.BlockSpec((1, 1, CONV_W - 1, D_LRU), lambda b, j: (0, b, 0, 0)),
                   pl.BlockSpec((1, D_SSD, STATE), lambda b, j: (b, 0, 0)),
                   pl.BlockSpec((1, 1, CONV_W - 1, CONV_DIM), lambda b, j: (0, b, 0, 0))],
        out_shape=[_whole((nbp, seq, D_MODEL)), _whole((nbp, 1, D_LRU)),
                   _whole((1, nbp, CONV_W - 1, D_LRU)), _whole((nbp, D_SSD, STATE)),
                   _whole((1, nbp, CONV_W - 1, CONV_DIM))],
        scratch_shapes=[pltpu.VMEM((n_chunks, HALO + CHUNK, D_LRU), F32),
                        pltpu.VMEM((n_chunks, HALO + CHUNK, CONV_DIM), F32),
                        pltpu.VMEM((HALO, D_LRU), F32),
                        pltpu.VMEM((HALO, CONV_DIM), F32),
                        pltpu.VMEM((1, D_LRU), F32),
                        pltpu.VMEM((STATE, D_SSD), F32)],
        compiler_params=pltpu.CompilerParams(
            dimension_semantics=("arbitrary", "arbitrary"), vmem_limit_bytes=VMEM_LIMIT),
        name="prompt",
    )(x_prompt, mod_p, *weights)

    xs2 = x_sample.reshape(nbs, D_MODEL)
    lh_s, lco, sco, p_lru, xbc_s, xdt, da, z_s, gm_s = pl.pallas_call(
        _sample_in_kernel,
        out_shape=[_whole((nbs, D_LRU)), _whole((CONV_W - 1, nbs, D_LRU)),
                   _whole((CONV_W - 1, nbs, CONV_DIM)), _whole((nbs, D_MODEL)),
                   _whole((nbs, CONV_DIM)), _whole((nbs, D_SSD)), _whole((nbs, LANES)),
                   _whole((nbs, D_SSD)), _whole((nbs, 2 * D_MODEL))],
        compiler_params=pltpu.CompilerParams(vmem_limit_bytes=VMEM_LIMIT),
        name="sample_in",
    )(xs2, mod_s, state_lru_h[0], jnp.transpose(state_lru_conv[0], (1, 0, 2)),
      jnp.transpose(state_ssd_conv[0], (1, 0, 2)), w_main,
      lcw, lcb, w_gate, ba, bx, lam, scw, scb, dtb, alog, wlp, e3)

    bb = SAMPLE_BB
    n_bc = GROUPS * STATE
    sh_s, y_raw = pl.pallas_call(
        _sample_state_kernel,
        grid=(nbs // bb,),
        in_specs=[pl.BlockSpec(memory_space=pltpu.SMEM),
                  pl.BlockSpec((bb, D_SSD, STATE), lambda i: (i, 0, 0)),
                  pl.BlockSpec((bb, D_SSD), lambda i: (i, 0)),
                  pl.BlockSpec((bb, n_bc), lambda i: (i, D_SSD // n_bc)),
                  pl.BlockSpec((bb, n_bc), lambda i: (i, D_SSD // n_bc + 1))],
        out_specs=[pl.BlockSpec((bb, D_SSD, STATE), lambda i: (i, 0, 0)),
                   pl.BlockSpec((bb, D_SSD), lambda i: (i, 0))],
        out_shape=[_whole((nbs, D_SSD, STATE)), _whole((nbs, D_SSD))],
        compiler_params=pltpu.CompilerParams(
            dimension_semantics=("arbitrary",), vmem_limit_bytes=VMEM_LIMIT),
        name="sample_state",
    )(da[:, :HEADS], state_ssd_h[0].reshape(nbs, D_SSD, STATE), xdt, xbc_s, xbc_s)

    y_s = pl.pallas_call(
        _sample_out_kernel,
        out_shape=_whole((nbs, D_MODEL)),
        compiler_params=pltpu.CompilerParams(vmem_limit_bytes=VMEM_LIMIT),
        name="sample_out",
    )(xs2, mod_s, y_raw, xbc_s, z_s, gm_s, p_lru, dexp, normw, wsp, wo, lng, lnb)

    return (y_p,
            y_s.reshape(nbs, 1, D_MODEL),
            lh_p.reshape(1, nbp, D_LRU),
            lc_p,
            sh_p.reshape(1, nbp, HEADS, HEAD_DIM, STATE),
            sc_p,
            lh_s.reshape(1, nbs, D_LRU),
            jnp.transpose(lco, (1, 0, 2)).reshape(1, nbs, CONV_W - 1, D_LRU),
            sh_s.reshape(1, nbs, HEADS, HEAD_DIM, STATE),
            jnp.transpose(sco, (1, 0, 2)).reshape(1, nbs, CONV_W - 1, CONV_DIM))
```

```python
import jax
import jax.numpy as jnp
from jax import lax
from jax.experimental import pallas as pl
from jax.experimental.pallas import tpu as pltpu

F32 = jnp.float32
BF16 = jnp.bfloat16

D_MODEL = 1024
D_LRU = 1024
LRU_BLOCK_W = 64
LRU_C = 8.0
D_SSD = 2048
HEAD_DIM = 64
HEADS = 32
GROUPS = 4
HEADS_PER_GROUP = HEADS // GROUPS
STATE = 128
GROUP_W = D_SSD // GROUPS
CHUNK = 128
CONV_DIM = D_SSD + 2 * GROUPS * STATE
CONV_W = 4
LN_EPS = 1e-5
RMS_EPS = 1e-5
ALPHA = 2.0 ** 0.25
LOG2_E = 1.4426950408889634

LANES = 128
SUBLANES = 8
SEG = CHUNK // SUBLANES
HALO = (CONV_W - 1) * SUBLANES
N_MAIN = 2 * D_LRU + D_SSD + CONV_DIM
OFF_LRU_X, OFF_LRU_Z, OFF_SSD_Z, OFF_XBC = 0, D_LRU, 2 * D_LRU, 2 * D_LRU + D_SSD
N_IN = N_MAIN + HEADS + 2 * D_MODEL

PROMPT_TC = 512
SAMPLE_BB = 8
SAMPLE_COLS = 1024
CONV_SLAB = 512
VMEM_LIMIT = 61 * 1024 * 1024


def _sigmoid(x):
    return 0.5 + 0.5 * jnp.tanh(0.5 * x)


def _silu(x):
    h = 0.5 * x
    return h + h * jnp.tanh(h)


def _softplus(x):
    return jnp.maximum(x, 0.0) + jnp.log1p(jnp.exp(-jnp.abs(x)))


def _dot(a, b):
    return jnp.dot(a, b, preferred_element_type=F32)


def _dot_nt(a, b):
    return lax.dot_general(a, b, (((1,), (1,)), ((), ())), preferred_element_type=F32)


def _dot_tn(a, b):
    return lax.dot_general(a, b, (((0,), (0,)), ((), ())), preferred_element_type=F32)


def _w(ref, c0=None, c1=None):
    return ref[...] if c0 is None else ref[:, c0:c1]


def _split3(x):
    hi = x.astype(BF16).astype(F32)
    r = x - hi
    mid = r.astype(BF16).astype(F32)
    lo = r - mid
    return hi, mid, lo


def _dot_exact_rhs(lhs_bf16, x):
    hi, mid, lo = _split3(x)
    return (_dot(lhs_bf16, hi.astype(BF16)) + _dot(lhs_bf16, mid.astype(BF16))
            + _dot(lhs_bf16, lo.astype(BF16)))


def _dot_exact_lhs(x, rhs_bf16):
    hi, mid, lo = _split3(x)
    return (_dot(hi.astype(BF16), rhs_bf16) + _dot(mid.astype(BF16), rhs_bf16)
            + _dot(lo.astype(BF16), rhs_bf16))


def _expand_heads(v, e3):
    lane = lax.broadcasted_iota(jnp.int32, v.shape, 1)
    v = jnp.where(lane < HEADS, v, 0.0)
    hi, mid, lo = _split3(v)
    packed = hi + pltpu.roll(mid, HEADS, axis=1) + pltpu.roll(lo, 2 * HEADS, axis=1)
    return _dot(packed.astype(BF16), e3)


def _lru_gates(u, w_gate_ref, ba, bx, lam):
    ub = u.astype(BF16)
    r_parts, i_parts = [], []
    for q in range(D_LRU // LANES):
        gq = _dot(ub[:, q * LANES:(q + 1) * LANES], w_gate_ref[q])
        r_parts.append(gq[:, :LANES])
        i_parts.append(gq[:, LANES:])
    r = _sigmoid(jnp.concatenate(r_parts, axis=1) + ba)
    i = _sigmoid(jnp.concatenate(i_parts, axis=1) + bx)
    log_a = (-LRU_C) * r * _softplus(-lam)
    a = jnp.exp(log_a)
    v = -jnp.tanh(log_a) * (a * a + 1.0)
    mult = jnp.where(v > 0.0, v * lax.rsqrt(v), 0.0)
    return a, mult, i


def _gated_rmsnorm(y, z, norm_w):
    g = y * _silu(z)
    parts = []
    for gi in range(GROUPS):
        gg = g[:, gi * GROUP_W:(gi + 1) * GROUP_W]
        ms = jnp.mean(gg * gg, axis=-1, keepdims=True)
        parts.append(gg * lax.rsqrt(ms + RMS_EPS))
    return jnp.concatenate(parts, axis=1) * norm_w


def _layer_norm(v, g, b):
    mu = jnp.mean(v, axis=-1, keepdims=True)
    d = v - mu
    var = jnp.mean(d * d, axis=-1, keepdims=True)
    return d * lax.rsqrt(var + LN_EPS) * g + b


def _cond_kernel(c_ref, w_ref, b_ref, o_ref):
    o_ref[...] = _dot(c_ref[...].astype(BF16), w_ref[...].astype(BF16)) + b_ref[...]


def _time_permute(v, n_chunks, to_permuted):
    sub = lax.broadcasted_iota(jnp.int32, (SUBLANES, 1), 0)
    n_half = SEG // SUBLANES
    out = [None] * (n_chunks * SEG)
    for c in range(n_chunks):
        for h in range(n_half):
            natural = [c * SEG + n_half * s + h for s in range(SUBLANES)]
            permuted = [c * SEG + SUBLANES * h + i for i in range(SUBLANES)]
            src, dst = (natural, permuted) if to_permuted else (permuted, natural)
            rows = [v[g * SUBLANES:(g + 1) * SUBLANES] for g in src]
            d = SUBLANES // 2
            while d >= 1:
                keep = jnp.bitwise_and(sub, d) == 0
                for a in range(SUBLANES):
                    if a & d == 0:
                        lo, hi = rows[a], rows[a + d]
                        rows[a] = jnp.where(keep, lo, pltpu.roll(hi, d, axis=0))
                        rows[a + d] = jnp.where(keep, pltpu.roll(lo, SUBLANES - d, axis=0), hi)
                d //= 2
            for i, g in enumerate(dst):
                out[g] = rows[i]
    return jnp.concatenate(out, axis=0)


def _permuted_conv_slab(raw, buf, tail, w_ref, b_ref, n_chunks, c0):
    cols = slice(c0, c0 + raw.shape[1])
    sub = lax.broadcasted_iota(jnp.int32, (SUBLANES, 1), 0)
    outs = []
    for c in range(n_chunks):
        buf[c, HALO:HALO + CHUNK, cols] = raw[c * CHUNK:(c + 1) * CHUNK]
        for i in range(CONV_W - 1):
            r0 = CHUNK + i * SUBLANES
            cur = buf[c, r0:r0 + SUBLANES, cols]
            if c == 0:
                prev = tail[i * SUBLANES:(i + 1) * SUBLANES, cols]
            else:
                prev = buf[c - 1, r0:r0 + SUBLANES, cols]
            buf[c, i * SUBLANES:(i + 1) * SUBLANES, cols] = pltpu.roll(
                jnp.where(sub == SUBLANES - 1, prev, cur), 1, axis=0)
        acc = b_ref[:, cols] + w_ref[0:1, cols] * buf[c, 0:CHUNK, cols]
        for k in range(1, CONV_W):
            acc = acc + w_ref[k:k + 1, cols] * buf[c, k * SUBLANES:k * SUBLANES + CHUNK, cols]
        outs.append(acc)
    tail[:, cols] = buf[n_chunks - 1, CHUNK:CHUNK + HALO, cols]
    return jnp.concatenate(outs, axis=0) if n_chunks > 1 else outs[0]


def _project_and_conv(hb, w_ref, off, width, buf, tail, cw_ref, cb_ref, n_chunks, post=None):
    n_slabs = width // CONV_SLAB
    raws, outs = [], []
    for k in range(n_slabs + 1):
        if k < n_slabs:
            raws.append(_dot(hb, _w(w_ref, off + k * CONV_SLAB, off + (k + 1) * CONV_SLAB)))
        if k >= 1:
            o = _permuted_conv_slab(raws[k - 1], buf, tail, cw_ref, cb_ref, n_chunks, (k - 1) * CONV_SLAB)
            outs.append(o if post is None else post(o))
    return jnp.concatenate(outs, axis=1)


def _permuted_scan(a, b, h0, n_chunks):
    hs = []
    h_in = h0
    for c in range(n_chunks):
        ps, hz = [], []
        for q in range(SEG):
            r0 = c * CHUNK + q * SUBLANES
            aq, bq = a[r0:r0 + SUBLANES], b[r0:r0 + SUBLANES]
            if q == 0:
                p, h = aq, bq
            else:
                p, h = aq * p, aq * h + bq
            ps.append(p)
            hz.append(h)
        carries = [h_in]
        for s in range(SUBLANES):
            carries.append(h[s:s + 1] + p[s:s + 1] * carries[-1])
        h_in = carries[SUBLANES]
        carry = jnp.concatenate(carries[:SUBLANES], axis=0)
        hs.extend(hz[q] + ps[q] * carry for q in range(SEG))
    return jnp.concatenate(hs, axis=0), h_in


def _prompt_kernel(x_ref, mod_ref, w_main, w_dtT, lcw, lcb, w_gate, ba, bx, lam,
                   scw, scb, dtb, dtbT, alog, alogT, dexp, normw, wlp, wsp, wo, lng, lnb, e3,
                   y_ref, lh_ref, lc_ref, sh_ref, sc_ref,
                   lbuf, sbuf, ltail, stail, hl, hT):
    tc = x_ref.shape[1]
    n_chunks = tc // CHUNK
    j = pl.program_id(1)
    last = pl.num_programs(1) - 1

    @pl.when(j == 0)
    def _():
        ltail[...] = jnp.zeros_like(ltail)
        stail[...] = jnp.zeros_like(stail)
        hl[...] = jnp.zeros_like(hl)
        hT[...] = jnp.zeros_like(hT)

    x = x_ref[0]
    mod = mod_ref[0]
    shift = mod[:, :D_MODEL]
    scale = mod[:, D_MODEL:2 * D_MODEL]
    gate = mod[:, 2 * D_MODEL:]
    hb = _time_permute(x * (1.0 + scale) + shift, n_chunks, True).astype(BF16)

    u = _project_and_conv(hb, w_main, OFF_LRU_X, D_LRU, lbuf, ltail, lcw, lcb, n_chunks)

    a, mult, ig = _lru_gates(u, w_gate, ba[...], bx[...], lam[...])
    iu = ig * u
    row = lax.broadcasted_iota(jnp.int32, (tc, 1), 0)
    first = jnp.logical_and(row == 0, j == 0)
    lru_z = _dot(hb, _w(w_main, OFF_LRU_Z, OFF_LRU_Z + D_LRU))
    hs, h_end = _permuted_scan(a, jnp.where(first, iu, mult * iu), hl[...], n_chunks)
    hl[...] = h_end
    y_lru = hs * _silu(lru_z)
    xbc = _project_and_conv(hb, w_main, OFF_XBC, CONV_DIM, sbuf, stail, scw, scb, n_chunks, post=_silu)
    p_lru = _dot(y_lru.astype(BF16), _w(wlp))

    xs = xbc[:, :D_SSD]
    bm = xbc[:, D_SSD:D_SSD + GROUPS * STATE]
    cm = xbc[:, D_SSD + GROUPS * STATE:]
    tail_proj = _dot(hb, _w(w_main, N_MAIN, N_IN))
    dt = _softplus(tail_proj[:, :LANES] + dtb[...])
    merge_logits = tail_proj[:, HEADS:HEADS + 2 * D_MODEL]
    dtT = _softplus(_dot_nt(w_dtT[...], hb) + dtbT[...])
    a_row = -LOG2_E * jnp.exp(alog[...])
    a_col = -LOG2_E * jnp.exp(alogT[...])

    ri = lax.broadcasted_iota(jnp.int32, (CHUNK, CHUNK), 0)
    ci = lax.broadcasted_iota(jnp.int32, (CHUNK, CHUNK), 1)
    t_row = jnp.bitwise_and(ri, SUBLANES - 1) * SEG + jnp.right_shift(ri, 3)
    t_col = jnp.bitwise_and(ci, SUBLANES - 1) * SEG + jnp.right_shift(ci, 3)
    causal = t_row >= t_col
    tril = causal.astype(BF16)
    triu = (t_row <= t_col).astype(BF16)
    lane = lax.broadcasted_iota(jnp.int32, (CHUNK, LANES), 1)
    lo_half = lane < HEAD_DIM

    def chunk_prologue(c):
        r0 = c * CHUNK
        xs_c = xs[r0:r0 + CHUNK]
        dt_c = dt[r0:r0 + CHUNK]
        dtT_c = dtT[:, r0:r0 + CHUNK]
        acs = _dot_exact_rhs(tril, dt_c * a_row)
        acsT = _dot_exact_lhs(dtT_c * a_col, triu)
        e_exp = _expand_heads(jnp.exp2(acs), e3[...])
        w_exp = _expand_heads(jnp.exp2(acs[CHUNK - 1:CHUNK, :] - acs) * dt_c, e3[...])
        return dict(xs=xs_c, xs_b=xs_c.astype(BF16), bm=bm[r0:r0 + CHUNK].astype(BF16),
                    cm=cm[r0:r0 + CHUNK].astype(BF16), dtT=dtT_c, acs=acs, acsT=acsT,
                    e_exp=e_exp, xsw=(xs_c * w_exp).astype(BF16))

    def group_cb(p, g):
        return _dot_nt(p["cm"][:, g * STATE:(g + 1) * STATE], p["bm"][:, g * STATE:(g + 1) * STATE])

    y_chunks = []
    late_proj = []
    nxt = chunk_prologue(0)
    for c in range(n_chunks):
        p = nxt
        if c == 0:
            late_proj.append(_dot(hb, _w(w_main, OFF_SSD_Z, OFF_SSD_Z + D_SSD)))
        if c + 1 < n_chunks:
            nxt = chunk_prologue(c + 1)
        acs, acsT, dtT_c, xs_cb = p["acs"], p["acsT"], p["dtT"], p["xs_b"]

        y_parts = []
        cb_next = group_cb(p, 0)
        for g in range(GROUPS):
            cb = cb_next
            if g + 1 < GROUPS:
                cb_next = group_cb(p, g + 1)
            for q in range(HEADS_PER_GROUP // 2):
                pair = g * (HEADS_PER_GROUP // 2) + q
                xp = xs_cb[:, pair * LANES:(pair + 1) * LANES]
                zero = jnp.zeros_like(xp)
                ms = []
                for half in range(2):
                    h = 2 * pair + half
                    seg = acs[:, h:h + 1] - acsT[h:h + 1, :]
                    m = jnp.exp2(jnp.where(causal, seg, -jnp.inf)) * cb * dtT_c[h:h + 1, :]
                    ms.append(m.astype(BF16))
                x2 = jnp.concatenate([jnp.where(lo_half, xp, zero), jnp.where(lo_half, zero, xp)], axis=0)
                y_parts.append(_dot(jnp.concatenate(ms, axis=1), x2))
        y_diag = jnp.concatenate(y_parts, axis=1)

        h_prev = hT[...]
        h_prev_b = h_prev.astype(BF16)
        off_parts, st_parts = [], []
        for g in range(GROUPS):
            sl = slice(g * GROUP_W, (g + 1) * GROUP_W)
            off_parts.append(_dot(p["cm"][:, g * STATE:(g + 1) * STATE], h_prev_b[:, sl]))
            st_parts.append(_dot_tn(p["bm"][:, g * STATE:(g + 1) * STATE], p["xsw"][:, sl]))
        y_off = jnp.concatenate(off_parts, axis=1) * p["e_exp"]
        hT[...] = p["e_exp"][CHUNK - 1:CHUNK, :] * h_prev + jnp.concatenate(st_parts, axis=1)
        y_chunks.append(y_diag + y_off + dexp[...] * p["xs"])
    ssd_z, = late_proj

    halves = [(0, tc)] if n_chunks == 1 else [(0, tc // 2), (tc // 2, tc)]
    y_ssd_b, p_ssd, merged_b, outs = {}, {}, {}, {}

    def norm_half(i):
        r0, r1 = halves[i]
        yh = jnp.concatenate(y_chunks[r0 // CHUNK:r1 // CHUNK], axis=0)
        y_ssd_b[i] = _gated_rmsnorm(yh, ssd_z[r0:r1], normw[...]).astype(BF16)

    def merge_half(i):
        r0, r1 = halves[i]
        gm = _sigmoid(merge_logits[r0:r1])
        merged_b[i] = (gm[:, :D_MODEL] * p_lru[r0:r1] + gm[:, D_MODEL:] * p_ssd[i]).astype(BF16)

    def finish_half(i):
        r0, r1 = halves[i]
        o = _time_permute(outs[i], (r1 - r0) // CHUNK, False)
        y_ref[0, r0:r1, :] = _layer_norm(ALPHA * x[r0:r1] + gate * o, lng[...], lnb[...])

    norm_half(0)
    for i in range(len(halves)):
        p_ssd[i] = _dot(y_ssd_b[i], _w(wsp))
        if i + 1 < len(halves):
            norm_half(i + 1)
        merge_half(i)
        outs[i] = _dot(merged_b[i], _w(wo))
        if i >= 1:
            finish_half(i - 1)
    finish_half(len(halves) - 1)

    @pl.when(j == last)
    def _():
        for i in range(CONV_W - 1):
            r = (i + 1) * SUBLANES - 1
            lc_ref[0, 0, i:i + 1, :] = ltail[r:r + 1, :]
            sc_ref[0, 0, i:i + 1, :] = stail[r:r + 1, :]
        lh_ref[0] = hl[...]
        hfin = hT[...]
        for q in range(D_SSD // LANES):
            sh_ref[0, q * LANES:(q + 1) * LANES, :] = hfin[:, q * LANES:(q + 1) * LANES].T


def _sample_in_kernel(x_ref, mod_ref, lh0_ref, lst_ref, sst_ref, w_main, lcw, lcb,
                      w_gate, ba, bx, lam, scw, scb, dtb, alog, wlp, e3,
                      lh_ref, lco_ref, sco_ref, plru_ref, xbc_ref, xdt_ref, da_ref, z_ref, gm_ref,
                      hb_s, proj):
    i = pl.program_id(0)

    @pl.when(i == 0)
    def _():
        mod = mod_ref[...]
        hb_s[...] = (x_ref[...] * (1.0 + mod[:, D_MODEL:2 * D_MODEL]) + mod[:, :D_MODEL]).astype(BF16)

    proj[i] = _dot(hb_s[...], w_main[...])

    @pl.when(i == pl.num_programs(0) - 1)
    def _():
        _sample_in_rest(proj, lh0_ref, lst_ref, sst_ref, lcw, lcb, w_gate, ba, bx, lam, scw, scb, dtb,
                        alog, wlp, e3, lh_ref, lco_ref, sco_ref, plru_ref, xbc_ref, xdt_ref, da_ref,
                        z_ref, gm_ref)


def _sample_in_rest(proj, lh0_ref, lst_ref, sst_ref, lcw, lcb, w_gate, ba, bx, lam, scw, scb, dtb,
                    alog, wlp, e3, lh_ref, lco_ref, sco_ref, plru_ref, xbc_ref, xdt_ref, da_ref,
                    z_ref, gm_ref):
    def cols(c0, c1):
        return jnp.concatenate([proj[b] for b in range(c0 // SAMPLE_COLS, -(-c1 // SAMPLE_COLS))],
                               axis=1)[:, c0 % SAMPLE_COLS:c0 % SAMPLE_COLS + c1 - c0]

    lru_x = cols(OFF_LRU_X, OFF_LRU_X + D_LRU)
    xbc_raw = cols(OFF_XBC, OFF_XBC + CONV_DIM)
    u = lcb[...] + lcw[CONV_W - 1:CONV_W, :] * lru_x
    xbc = scb[...] + scw[CONV_W - 1:CONV_W, :] * xbc_raw
    for k in range(CONV_W - 1):
        u = u + lcw[k:k + 1, :] * lst_ref[k]
        xbc = xbc + scw[k:k + 1, :] * sst_ref[k]
    for k in range(CONV_W - 2):
        lco_ref[k] = lst_ref[k + 1]
        sco_ref[k] = sst_ref[k + 1]
    lco_ref[CONV_W - 2] = lru_x
    sco_ref[CONV_W - 2] = xbc_raw
    xbc = _silu(xbc)
    xbc_ref[...] = xbc

    a, mult, ig = _lru_gates(u, w_gate, ba[...], bx[...], lam[...])
    h_new = a * lh0_ref[...] + mult * ig * u
    lh_ref[...] = h_new
    lru_z = cols(OFF_LRU_Z, OFF_LRU_Z + D_LRU)
    plru_ref[...] = _dot((h_new * _silu(lru_z)).astype(BF16), _w(wlp))

    tail_proj = cols(N_MAIN, N_IN)
    dt = _softplus(tail_proj[:, :LANES] + dtb[...])
    da_ref[...] = jnp.exp(dt * (-jnp.exp(alog[...])))
    xdt_ref[...] = xbc[:, :D_SSD] * _expand_heads(dt, e3[...])
    z_ref[...] = cols(OFF_SSD_Z, OFF_SSD_Z + D_SSD)
    gm_ref[...] = _sigmoid(tail_proj[:, HEADS:HEADS + 2 * D_MODEL])


def _sample_state_kernel(da_ref, h0_ref, xdt_ref, b_ref, c_ref, ho_ref, y_ref):
    i = pl.program_id(0)
    bb = h0_ref.shape[0]
    n_terms = 6
    pad_rows = LANES - n_terms * bb

    def stack(terms, width):
        return jnp.concatenate(list(terms) + [jnp.zeros((pad_rows, width), F32)], axis=0)

    xh, xm, xl = _split3(xdt_ref[...])
    bh, bm, bl = _split3(b_ref[...])
    x6t = stack((xh, xh, xm, xm, xh, xl), D_SSD).T.astype(BF16)
    b6 = stack((bh, bm, bh, bm, bl, bh), GROUPS * STATE)
    ch, cmid, cl = _split3(c_ref[...])
    c3 = jnp.concatenate([ch, cmid, cl], axis=0).astype(BF16)
    krow = lax.broadcasted_iota(jnp.int32, (LANES, 1), 0)
    yrow = lax.broadcasted_iota(jnp.int32, (3 * bb, 1), 0)
    for b in range(bb):
        own = jnp.logical_and(jnp.bitwise_and(krow, bb - 1) == b, krow < n_terms * bb)
        rhs = jnp.where(own, b6, 0.0).astype(BF16)
        y_own = jnp.bitwise_and(yrow, bb - 1) == b
        y_parts = []
        for g in range(GROUPS):
            st = _dot(x6t[g * GROUP_W:(g + 1) * GROUP_W, :], rhs[:, g * STATE:(g + 1) * STATE])
            hn_parts = []
            for hh in range(HEADS_PER_GROUP):
                h = g * HEADS_PER_GROUP + hh
                rows = slice(h * HEAD_DIM, (h + 1) * HEAD_DIM)
                hn = da_ref[i * bb + b, h] * h0_ref[b, rows, :] + st[hh * HEAD_DIM:(hh + 1) * HEAD_DIM, :]
                ho_ref[b, rows, :] = hn
                hn_parts.append(hn.astype(BF16))
            hn_g = jnp.concatenate(hn_parts, axis=0)
            r = _dot_nt(c3[:, g * STATE:(g + 1) * STATE], hn_g)
            y_parts.append(jnp.sum(jnp.where(y_own, r, 0.0), axis=0, keepdims=True))
        y_ref[b:b + 1, :] = jnp.concatenate(y_parts, axis=1)


def _sample_out_kernel(x_ref, mod_ref, yraw_ref, xbc_ref, z_ref, gm_ref, plru_ref,
                       dexp, normw, wsp, wo, lng, lnb, o_ref):
    x = x_ref[...]
    gate = mod_ref[:, 2 * D_MODEL:]
    y = yraw_ref[...] + dexp[...] * xbc_ref[:, :D_SSD]
    y_ssd = _gated_rmsnorm(y, z_ref[...], normw[...])
    p_ssd = _dot(y_ssd.astype(BF16), _w(wsp))
    gm = gm_ref[...]
    merged = gm[:, :D_MODEL] * plru_ref[...] + gm[:, D_MODEL:] * p_ssd
    out = _dot(merged.astype(BF16), _w(wo))
    o_ref[...] = _layer_norm(ALPHA * x + gate * out, lng[...], lnb[...])


def _resident(shape, grid_rank):
    zeros = (0,) * len(shape)
    if grid_rank == 1:
        imap = lambda i: zeros
    else:
        imap = lambda b, j: zeros
    return pl.BlockSpec(shape, imap, pipeline_mode=pl.Buffered(1))


def _whole(shape):
    return jax.ShapeDtypeStruct(shape, F32)


def kernel(x_prompt, x_sample, state_lru_h, state_lru_conv, state_ssd_h, state_ssd_conv, c_prompt, c_sample, w_cond, b_cond, w_in, lru_conv_w, lru_conv_b, lru_wa, lru_ba, lru_wx, lru_bx, lru_lambda, ssd_conv_w, ssd_conv_b, ssd_dt_bias, ssd_a_log, ssd_d, ssd_norm_w, w_lru_proj, w_ssd_proj, w_out, ln_g, ln_b):
    nbp, seq, _ = x_prompt.shape
    nbs = x_sample.shape[0]
    tc = PROMPT_TC
    assert seq % tc == 0 and tc % CHUNK == 0 and nbs % SAMPLE_BB == 0 and SAMPLE_BB == SUBLANES

    w_in0 = w_in[0]
    w_main = w_in0.astype(BF16)
    w_dt_cols = w_in0[:, N_MAIN:N_MAIN + HEADS]
    w_dtT = w_dt_cols.T.astype(BF16)
    nq = D_LRU // LANES
    wa = lru_wa[0].reshape(nq, 2, LRU_BLOCK_W, LRU_BLOCK_W)
    wx = lru_wx[0].reshape(nq, 2, LRU_BLOCK_W, LRU_BLOCK_W)
    zb = jnp.zeros((nq, LRU_BLOCK_W, LRU_BLOCK_W), F32)

    def blockdiag(w):
        top = jnp.concatenate([w[:, 0], zb], axis=2)
        bot = jnp.concatenate([zb, w[:, 1]], axis=2)
        return jnp.concatenate([top, bot], axis=1)

    w_gate = jnp.concatenate([blockdiag(wa), blockdiag(wx)], axis=2).astype(BF16)
    row = lambda v: v.reshape(1, -1)
    ba, bx, lam = row(lru_ba[0]), row(lru_bx[0]), row(lru_lambda[0])
    lcw, lcb = lru_conv_w[0], row(lru_conv_b[0])
    scw, scb = ssd_conv_w[0], row(ssd_conv_b[0])
    dtb = jnp.pad(row(ssd_dt_bias[0]), ((0, 0), (0, LANES - HEADS)))
    alog = jnp.pad(row(ssd_a_log[0]), ((0, 0), (0, LANES - HEADS)))
    dtbT = ssd_dt_bias[0].reshape(HEADS, 1)
    alogT = ssd_a_log[0].reshape(HEADS, 1)
    dexp = row(jnp.repeat(ssd_d[0], HEAD_DIM))
    normw = row(ssd_norm_w[0])
    wlp = w_lru_proj[0].astype(BF16)
    wsp = w_ssd_proj[0].astype(BF16)
    wo = w_out[0].astype(BF16)
    lng, lnb = row(ln_g[0]), row(ln_b[0])
    k_idx = jnp.arange(LANES)[:, None]
    c_idx = jnp.arange(D_SSD)[None, :]
    e3 = jnp.logical_and(k_idx % HEADS == c_idx // HEAD_DIM, k_idx < 3 * HEADS).astype(BF16)

    c_all = jnp.concatenate([c_prompt, c_sample], axis=0)
    mod = pl.pallas_call(
        _cond_kernel,
        out_shape=_whole((nbp + nbs, 3 * D_MODEL)),
        name="cond",
    )(c_all, w_cond[0], row(b_cond[0]))
    mod_p = mod[:nbp].reshape(nbp, 1, 3 * D_MODEL)
    mod_s = mod[nbp:]

    weights = (w_main, w_dtT, lcw, lcb, w_gate, ba, bx, lam, scw, scb, dtb, dtbT,
               alog, alogT, dexp, normw, wlp, wsp, wo, lng, lnb, e3)
    n_chunks = tc // CHUNK
    y_p, lh_p, lc_p, sh_p, sc_p = pl.pallas_call(
        _prompt_kernel,
        grid=(nbp, seq // tc),
        in_specs=[pl.BlockSpec((1, tc, D_MODEL), lambda b, j: (b, j, 0)),
                  pl.BlockSpec((1, 1, 3 * D_MODEL), lambda b, j: (b, 0, 0))]
                 + [_resident(w.shape, 2) for w in weights],
        out_specs=[pl.BlockSpec((1, tc, D_MODEL), lambda b, j: (b, j, 0)),
                   pl.BlockSpec((1, 1, D_LRU), lambda b, j: (b, 0, 0)),
                   pl.BlockSpec((1, 1, CONV_W - 1, D_LRU), lambda b, j: (0, b, 0, 0)),
                   pl.BlockSpec((1, D_SSD, STATE), lambda b, j: (b, 0, 0)),
                   pl.BlockSpec((1, 1, CONV_W - 1, CONV_DIM), lambda b, j: (0, b, 0, 0))],
        out_shape=[_whole((nbp, seq, D_MODEL)), _whole((nbp, 1, D_LRU)),
                   _whole((1, nbp, CONV_W - 1, D_LRU)), _whole((nbp, D_SSD, STATE)),
                   _whole((1, nbp, CONV_W - 1, CONV_DIM))],
        scratch_shapes=[pltpu.VMEM((n_chunks, HALO + CHUNK, D_LRU), F32),
                        pltpu.VMEM((n_chunks, HALO + CHUNK, CONV_DIM), F32),
                        pltpu.VMEM((HALO, D_LRU), F32),
                        pltpu.VMEM((HALO, CONV_DIM), F32),
                        pltpu.VMEM((1, D_LRU), F32),
                        pltpu.VMEM((STATE, D_SSD), F32)],
        compiler_params=pltpu.CompilerParams(
            dimension_semantics=("arbitrary", "arbitrary"), vmem_limit_bytes=VMEM_LIMIT),
        name="prompt",
    )(x_prompt, mod_p, *weights)

    xs2 = x_sample.reshape(nbs, D_MODEL)
    def whole_block(shape):
        zeros = (0,) * len(shape)
        return pl.BlockSpec(shape, lambda i: zeros)

    n_cb = -(-N_IN // SAMPLE_COLS)
    si_in = (xs2, mod_s, state_lru_h[0], jnp.transpose(state_lru_conv[0], (1, 0, 2)),
             jnp.transpose(state_ssd_conv[0], (1, 0, 2)), w_main,
             lcw, lcb, w_gate, ba, bx, lam, scw, scb, dtb, alog, wlp, e3)
    si_out = [_whole((nbs, D_LRU)), _whole((CONV_W - 1, nbs, D_LRU)),
              _whole((CONV_W - 1, nbs, CONV_DIM)), _whole((nbs, D_MODEL)),
              _whole((nbs, CONV_DIM)), _whole((nbs, D_SSD)), _whole((nbs, LANES)),
              _whole((nbs, D_SSD)), _whole((nbs, 2 * D_MODEL))]
    lh_s, lco, sco, p_lru, xbc_s, xdt, da, z_s, gm_s = pl.pallas_call(
        _sample_in_kernel,
        grid=(n_cb,),
        in_specs=[pl.BlockSpec((D_MODEL, SAMPLE_COLS), lambda i: (0, i)) if a is w_main
                  else whole_block(a.shape) for a in si_in],
        out_specs=[whole_block(o.shape) for o in si_out],
        out_shape=si_out,
        scratch_shapes=[pltpu.VMEM((nbs, D_MODEL), BF16),
                        pltpu.VMEM((n_cb, nbs, SAMPLE_COLS), F32)],
        compiler_params=pltpu.CompilerParams(
            dimension_semantics=("arbitrary",), vmem_limit_bytes=VMEM_LIMIT),
        name="sample_in",
    )(*si_in)

    bb = SAMPLE_BB
    n_bc = GROUPS * STATE
    sh_s, y_raw = pl.pallas_call(
        _sample_state_kernel,
        grid=(nbs // bb,),
        in_specs=[pl.BlockSpec(memory_space=pltpu.SMEM),
                  pl.BlockSpec((bb, D_SSD, STATE), lambda i: (i, 0, 0)),
                  pl.BlockSpec((bb, D_SSD), lambda i: (i, 0)),
                  pl.BlockSpec((bb, n_bc), lambda i: (i, D_SSD // n_bc)),
                  pl.BlockSpec((bb, n_bc), lambda i: (i, D_SSD // n_bc + 1))],
        out_specs=[pl.BlockSpec((bb, D_SSD, STATE), lambda i: (i, 0, 0)),
                   pl.BlockSpec((bb, D_SSD), lambda i: (i, 0))],
        out_shape=[_whole((nbs, D_SSD, STATE)), _whole((nbs, D_SSD))],
        compiler_params=pltpu.CompilerParams(
            dimension_semantics=("arbitrary",), vmem_limit_bytes=VMEM_LIMIT),
        name="sample_state",
    )(da[:, :HEADS], state_ssd_h[0].reshape(nbs, D_SSD, STATE), xdt, xbc_s, xbc_s)

    y_s = pl.pallas_call(
        _sample_out_kernel,
        out_shape=_whole((nbs, D_MODEL)),
        compiler_params=pltpu.CompilerParams(vmem_limit_bytes=VMEM_LIMIT),
        name="sample_out",
    )(xs2, mod_s, y_raw, xbc_s, z_s, gm_s, p_lru, dexp, normw, wsp, wo, lng, lnb)

    return (y_p,
            y_s.reshape(nbs, 1, D_MODEL),
            lh_p.reshape(1, nbp, D_LRU),
            lc_p,
            sh_p.reshape(1, nbp, HEADS, HEAD_DIM, STATE),
            sc_p,
            lh_s.reshape(1, nbs, D_LRU),
            jnp.transpose(lco, (1, 0, 2)).reshape(1, nbs, CONV_W - 1, D_LRU),
            sh_s.reshape(1, nbs, HEADS, HEAD_DIM, STATE),
            jnp.transpose(sco, (1, 0, 2)).reshape(1, nbs, CONV_W - 1, CONV_DIM))
```
